```python
import math
import jax, jax.numpy as jnp
from jax import lax
import numpy as np

D_MODEL = 1024
BATCH = 2
SEQ = 16384
DEPTH = 1

HEAD_DIM = 64
D_MIX = D_MODEL
D_MOBA = D_MIX // 2
D_RET = D_MIX - D_MOBA
N_HEADS_MOBA = D_MOBA // HEAD_DIM
N_HEADS_RET = D_RET // HEAD_DIM
N_PROJ = 3 * D_MOBA + 4 * D_RET
MOBA_BLOCK = 256
MOBA_TOPK = 3
Q_BLOCK = 128
RET_CHUNK = 256
REL_BUCKETS = 32
REL_MAX_DIST = 128
ROPE_BASE = 10000.0
D_FF = ((8 * D_MODEL // 3 + 255) // 256) * 256
CONV_WIDTH = 3
EPS = 1e-6
NEG = -1e30

kernel_name = 'hybrid_moba_retention_block'


def rmsnorm(x, g):
    xf = x.astype(jnp.float32)
    y = xf * lax.rsqrt(jnp.mean(xf * xf, axis=-1, keepdims=True) + EPS)
    return (y * g.astype(jnp.float32)).astype(x.dtype)


def t5_bucket(rel):
    n = jnp.maximum(rel, 0)
    max_exact = REL_BUCKETS // 2
    n_f = jnp.maximum(n, 1).astype(jnp.float32)
    large = max_exact + (jnp.log(n_f / max_exact) / math.log(REL_MAX_DIST / max_exact)
                         * (REL_BUCKETS - max_exact)).astype(jnp.int32)
    large = jnp.minimum(large, REL_BUCKETS - 1)
    return jnp.where(n < max_exact, n, large)


def moba_attention(q, k, v, rel_bias):
    B, H, S, D = q.shape
    nb = S // MOBA_BLOCK
    n_sel = min(MOBA_TOPK, nb)
    scale = D ** -0.5
    kb = k.reshape(B, H, nb, MOBA_BLOCK, D)
    vb = v.reshape(B, H, nb, MOBA_BLOCK, D)
    k_mean = jnp.mean(kb.astype(jnp.float32), axis=3).astype(k.dtype)
    bias_t = rel_bias.T.astype(jnp.float32)
    head_ix = jnp.arange(H)[None, :, None, None, None]
    blk_pos = jnp.arange(MOBA_BLOCK, dtype=jnp.int32)
    gather = jax.vmap(jax.vmap(lambda blocks, ix: blocks[ix]))

    def one_query_block(i):
        q0 = i * Q_BLOCK
        qi = lax.dynamic_slice_in_dim(q, q0, Q_BLOCK, axis=2)
        t = q0 + jnp.arange(Q_BLOCK, dtype=jnp.int32)
        c = q0 // MOBA_BLOCK
        gate = jnp.einsum('bhqd,bhnd->bhqn', qi, k_mean).astype(jnp.float32)
        gate = jnp.where(jnp.arange(nb) < c, gate, -jnp.inf)
        _, idx = lax.top_k(gate, n_sel)
        valid = idx < c
        k_sel = gather(kb, idx)
        v_sel = gather(vb, idx)
        s_sel = jnp.einsum('bhqd,bhqnkd->bhqnk', qi, k_sel).astype(jnp.float32) * scale
        pos_sel = idx[..., None] * MOBA_BLOCK + blk_pos
        bias_sel = bias_t[head_ix, t5_bucket(t[:, None, None] - pos_sel)]
        s_sel = jnp.where(valid[..., None], s_sel + bias_sel, NEG)
        k_own = lax.dynamic_slice_in_dim(k, c * MOBA_BLOCK, MOBA_BLOCK, axis=2)
        v_own = lax.dynamic_slice_in_dim(v, c * MOBA_BLOCK, MOBA_BLOCK, axis=2)
        rel_own = t[:, None] - (c * MOBA_BLOCK + blk_pos)[None, :]
        s_own = jnp.einsum('bhqd,bhkd->bhqk', qi, k_own).astype(jnp.float32) * scale
        bias_own = bias_t[:, t5_bucket(rel_own)]
        s_own = jnp.where(rel_own >= 0, s_own + bias_own, NEG)
        logits = jnp.concatenate(
            [s_sel.reshape(B, H, Q_BLOCK, n_sel * MOBA_BLOCK), s_own], axis=-1)
        p = jax.nn.softmax(logits, axis=-1).astype(v.dtype)
        p_sel = p[..., :n_sel * MOBA_BLOCK].reshape(B, H, Q_BLOCK, n_sel, MOBA_BLOCK)
        p_own = p[..., n_sel * MOBA_BLOCK:]
        return (jnp.einsum('bhqnk,bhqnkd->bhqd', p_sel, v_sel)
                + jnp.einsum('bhqk,bhkd->bhqd', p_own, v_own))

    outs = lax.map(one_query_block, jnp.arange(S // Q_BLOCK, dtype=jnp.int32))
    return outs.transpose(1, 2, 0, 3, 4).reshape(B, H, S, D)


def rotary(x):
    S, D = x.shape[2], x.shape[3]
    inv = ROPE_BASE ** (-jnp.arange(0, D, 2, dtype=jnp.float32) / D)
    ang = jnp.arange(S, dtype=jnp.float32)[:, None] * inv[None, :]
    cos, sin = jnp.cos(ang), jnp.sin(ang)
    x1, x2 = x[..., :D // 2], x[..., D // 2:]
    return jnp.concatenate([x1 * cos - x2 * sin, x1 * sin + x2 * cos], axis=-1)


def retention(q, k, v, g):
    B, H, S, D = q.shape
    C = RET_CHUNK
    NC = S // C
    qf = rotary(q.astype(jnp.float32))
    kf = rotary(k.astype(jnp.float32)) * (D ** -0.5)
    vf = v.astype(jnp.float32)
    log_gamma = jnp.log(1.0 - 2.0 ** (-5.0 - jnp.arange(H, dtype=jnp.float32)))
    n = jnp.arange(C, dtype=jnp.float32)
    rel = n[:, None] - n[None, :]
    decay_mask = jnp.where(rel >= 0, jnp.exp(jnp.maximum(rel, 0.0)[None] * log_gamma[:, None, None]), 0.0)
    qc = qf.reshape(B, H, NC, C, D)
    kc = kf.reshape(B, H, NC, C, D)
    vc = vf.reshape(B, H, NC, C, D)
    scores = jnp.einsum('bhcnd,bhcmd->bhcnm', qc, kc) * decay_mask[None, :, None]
    y_inner = jnp.einsum('bhcnm,bhcme->bhcne', scores, vc)
    decay_end = jnp.exp((C - 1.0 - n)[None, :] * log_gamma[:, None])
    kv = jnp.einsum('bhcmd,bhcme->bhcde', kc * decay_end[None, :, None, :, None], vc)
    chunk_decay = jnp.exp(C * log_gamma)[None, :, None, None]

    def step(state, kv_c):
        return state * chunk_decay + kv_c, state

    _, r_prev = lax.scan(step, jnp.zeros((B, H, D, D), jnp.float32), jnp.moveaxis(kv, 2, 0))
    r_prev = jnp.moveaxis(r_prev, 0, 2)
    decay_start = jnp.exp((n + 1.0)[None, :] * log_gamma[:, None])
    y_cross = jnp.einsum('bhcnd,bhcde->bhcne', qc, r_prev) * decay_start[None, :, None, :, None]
    y = (y_inner + y_cross).reshape(B, H, S, D)
    mu = jnp.mean(y, axis=-1, keepdims=True)
    var = jnp.mean(jnp.square(y - mu), axis=-1, keepdims=True)
    y = (y - mu) * lax.rsqrt(var + EPS)
    y = y.transpose(0, 2, 1, 3).reshape(B, S, H * D)
    return (jax.nn.silu(g.astype(jnp.float32)) * y).astype(q.dtype)


def setup_inputs(seed: int = 0) -> dict:
    key = jax.random.key(seed)
    ks = jax.random.split(key, 12)
    L = DEPTH

    def nrm(k, shape, scale):
        return jax.random.normal(k, shape, jnp.float32) * scale

    return {
        'x': nrm(ks[0], (BATCH, SEQ, D_MODEL), 1.0),
        'norm_mix_pre': 1.0 + nrm(ks[1], (L, D_MODEL), 0.05),
        'w_in': nrm(ks[2], (L, D_MODEL, N_PROJ), D_MODEL ** -0.5),
        'rel_bias': nrm(ks[3], (REL_BUCKETS, N_HEADS_MOBA), 0.5),
        'w_out': nrm(ks[4], (L, D_MIX, D_MODEL), D_MIX ** -0.5),
        'norm_mix_post': 1.0 + nrm(ks[5], (L, D_MODEL), 0.05),
        'norm_ffn_pre': 1.0 + nrm(ks[6], (L, D_MODEL), 0.05),
        'w_up': nrm(ks[7], (L, D_MODEL, 2 * D_FF), D_MODEL ** -0.5),
        'conv_w': nrm(ks[8], (L, CONV_WIDTH, 2 * D_FF), CONV_WIDTH ** -0.5),
        'conv_b': nrm(ks[9], (L, 2 * D_FF), 0.02),
        'w_down': nrm(ks[10], (L, D_FF, D_MODEL), D_FF ** -0.5),
        'norm_ffn_post': 1.0 + nrm(ks[11], (L, D_MODEL), 0.05),
    }


def reference(x, norm_mix_pre, w_in, rel_bias, w_out, norm_mix_post,
              norm_ffn_pre, w_up, conv_w, conv_b, w_down, norm_ffn_post):
    B, S, _ = x.shape
    S_pad = -(-S // MOBA_BLOCK) * MOBA_BLOCK

    def heads(t, n_heads):
        return t.reshape(B, S_pad, n_heads, HEAD_DIM).transpose(0, 2, 1, 3)

    for l in range(DEPTH):
        h = rmsnorm(x, norm_mix_pre[l])
        proj = jnp.einsum('bsd,dn->bsn', h, w_in[l])
        proj = jnp.pad(proj, ((0, 0), (0, S_pad - S), (0, 0)))
        qa, ka, va, qr, kr, vr, gr = jnp.split(
            proj, np.cumsum([D_MOBA, D_MOBA, D_MOBA, D_RET, D_RET, D_RET])[:].tolist(), axis=-1)
        ya = moba_attention(heads(qa, N_HEADS_MOBA), heads(ka, N_HEADS_MOBA),
                            heads(va, N_HEADS_MOBA), rel_bias)
        ya = ya.transpose(0, 2, 1, 3).reshape(B, S_pad, D_MOBA)
        yr = retention(heads(qr, N_HEADS_RET), heads(kr, N_HEADS_RET),
                       heads(vr, N_HEADS_RET), gr)
        y = jnp.concatenate([ya, yr], axis=-1)[:, :S]
        y = jnp.einsum('bsn,nd->bsd', y, w_out[l])
        x = x + rmsnorm(y, norm_mix_post[l])
        h = rmsnorm(x, norm_ffn_pre[l])
        u = jnp.einsum('bsd,df->bsf', h, w_up[l])
        up = jnp.pad(u, ((0, 0), (CONV_WIDTH - 1, 0), (0, 0)))
        u = conv_b[l] + sum(conv_w[l, j] * up[:, j:j + S] for j in range(CONV_WIDTH))
        a, b = jnp.split(u, 2, axis=-1)
        f = jax.nn.gelu(a, approximate=True) * b
        y = jnp.einsum('bsf,fd->bsd', f, w_down[l])
        x = x + rmsnorm(y, norm_ffn_post[l])
    return x
```

```python
import functools
import math

import jax
import jax.numpy as jnp
from jax import lax
from jax.experimental import pallas as pl
from jax.experimental.pallas import tpu as pltpu

HEAD_DIM = 64
PAIR = 2 * HEAD_DIM
MOBA_BLOCK = 256
MOBA_TOPK = 3
RET_CHUNK = 256
REL_BUCKETS = 32
REL_MAX_DIST = 128
ROPE_BASE = 10000.0
CONV_WIDTH = 3
EPS = 1e-6
NEG = -1e30
V_ROWS = HEAD_DIM + 16
ROW_TILE = 512
FF_CHUNK = 256
VMEM_LIMIT = 56 * 1024 * 1024

NT = (((1,), (1,)), ((), ()))
TN = (((0,), (0,)), ((), ()))


def _rms(x, g):
    return x * lax.rsqrt(jnp.mean(x * x, axis=-1, keepdims=True) + EPS) * g


def _bias_kernel(rb_ref, o_ref):
    h = pl.program_id(0)
    key = lax.broadcasted_iota(jnp.int32, (MOBA_BLOCK, MOBA_BLOCK), 0)
    qry = lax.broadcasted_iota(jnp.int32, (MOBA_BLOCK, MOBA_BLOCK), 1)
    max_exact = REL_BUCKETS // 2
    for which in range(2):
        rel = qry - key + which * MOBA_BLOCK
        n = jnp.maximum(rel, 0)
        n_f = jnp.maximum(n, 1).astype(jnp.float32)
        large = max_exact + (jnp.log(n_f / max_exact) / math.log(REL_MAX_DIST / max_exact)
                             * (REL_BUCKETS - max_exact)).astype(jnp.int32)
        large = jnp.minimum(large, REL_BUCKETS - 1)
        bucket = jnp.where(n < max_exact, n, large)
        val = jnp.zeros((MOBA_BLOCK, MOBA_BLOCK), jnp.float32)
        for b in range(REL_BUCKETS):
            val = jnp.where(bucket == b, rb_ref[b, h], val)
        o_ref[0, which] = jnp.where(rel >= 0, val, NEG)


def _bias_tables(rel_bias):
    n_heads = rel_bias.shape[1]
    return pl.pallas_call(
        _bias_kernel,
        grid=(n_heads,),
        in_specs=[pl.BlockSpec(memory_space=pltpu.SMEM)],
        out_specs=pl.BlockSpec((1, 2, MOBA_BLOCK, MOBA_BLOCK), lambda h: (h, 0, 0, 0)),
        out_shape=jax.ShapeDtypeStruct((n_heads, 2, MOBA_BLOCK, MOBA_BLOCK), jnp.float32),
        name="bias_tables",
    )(rel_bias)


def _proj_kernel(x_ref, g_ref, wqk_ref, wvt_ref, wqkr_ref, wvgr_ref, cos_ref, sin_ref,
                 qa_ref, ka_ref, vt_ref, gate_ref, qr_ref, kr_ref, vr_ref, gr_ref,
                 kmt_ref, *, tiles_per_batch, d_a, d_r):
    t_in_b = pl.program_id(0) % tiles_per_batch
    n_heads = d_a // HEAD_DIM
    tm = x_ref.shape[0]
    n_blocks = tiles_per_batch * (tm // MOBA_BLOCK)

    @pl.when(t_in_b == 0)
    def _():
        kmt_ref[...] = jnp.zeros_like(kmt_ref)

    hb = _rms(x_ref[...], g_ref[...]).astype(jnp.bfloat16)

    qk = jnp.dot(hb, wqk_ref[...], preferred_element_type=jnp.float32)
    q = qk[:, :d_a]
    k = qk[:, d_a:]
    qa_ref[...] = (q * HEAD_DIM ** -0.5).astype(jnp.bfloat16)
    ka_ref[...] = k.astype(jnp.bfloat16)

    lane_head = lax.broadcasted_iota(jnp.int32, (1, d_a), 1) // HEAD_DIM
    for half in range(tm // MOBA_BLOCK):
        rows = slice(half * MOBA_BLOCK, (half + 1) * MOBA_BLOCK)
        gate_ref[:, rows] = lax.dot_general(
            kmt_ref[...], q[rows], NT, precision=lax.Precision.HIGHEST,
            preferred_element_type=jnp.float32)
        k_mean = jnp.mean(k[rows], axis=0, keepdims=True)
        j = t_in_b * (tm // MOBA_BLOCK) + half
        for hh in range(n_heads):
            kmt_ref[pl.ds(hh * n_blocks + j, 1), :] = jnp.where(lane_head == hh, k_mean, 0.0)

    vt = lax.dot_general(wvt_ref[...], hb, NT, preferred_element_type=jnp.float32)
    vt = vt.astype(jnp.bfloat16)
    for hh in range(n_heads):
        vt_ref[hh, 0:HEAD_DIM, :] = vt[hh * HEAD_DIM:(hh + 1) * HEAD_DIM, :]
        vt_ref[hh, HEAD_DIM:V_ROWS, :] = jnp.ones((V_ROWS - HEAD_DIM, tm), jnp.bfloat16)

    qkr = jnp.dot(hb, wqkr_ref[...], preferred_element_type=jnp.float32)
    cos = cos_ref[...]
    sin = sin_ref[...]
    first_half = (lax.broadcasted_iota(jnp.int32, (tm, PAIR), 1) % HEAD_DIM) < HEAD_DIM // 2
    for cidx in range(2 * d_r // PAIR):
        xc = qkr[:, cidx * PAIR:(cidx + 1) * PAIR]
        sw = jnp.where(first_half,
                       pltpu.roll(xc, PAIR - HEAD_DIM // 2, axis=1),
                       pltpu.roll(xc, HEAD_DIM // 2, axis=1))
        rc = xc * cos + sw * sin
        if cidx < d_r // PAIR:
            qr_ref[:, cidx * PAIR:(cidx + 1) * PAIR] = rc.astype(jnp.bfloat16)
        else:
            c2 = cidx - d_r // PAIR
            kr_ref[:, c2 * PAIR:(c2 + 1) * PAIR] = (rc * HEAD_DIM ** -0.5).astype(jnp.bfloat16)

    vg = jnp.dot(hb, wvgr_ref[...], preferred_element_type=jnp.float32)
    vr_ref[...] = vg[:, :d_r].astype(jnp.bfloat16)
    gr_ref[...] = vg[:, d_r:].astype(jnp.bfloat16)


def _const_spec(shape):
    zeros = (0,) * len(shape)
    return pl.BlockSpec(shape, lambda *_: zeros, pipeline_mode=pl.Buffered(1))


def _projection(x2, g, wqk, wvt, wqkr, wvgr, cos_t, sin_t, batch, seq):
    t, d = x2.shape
    d_a = wqk.shape[1] // 2
    d_r = wqkr.shape[1] // 2
    n_heads = d_a // HEAD_DIM
    tm = ROW_TILE
    tpb = seq // tm
    n_blocks = seq // MOBA_BLOCK
    row = lambda w: pl.BlockSpec((tm, w), lambda i: (i, 0))
    bf = jnp.bfloat16
    return pl.pallas_call(
        functools.partial(_proj_kernel, tiles_per_batch=tpb, d_a=d_a, d_r=d_r),
        grid=(t // tm,),
        in_specs=[row(d), _const_spec((1, d)), _const_spec(wqk.shape), _const_spec(wvt.shape),
                  _const_spec(wqkr.shape), _const_spec(wvgr.shape),
                  pl.BlockSpec((tm, PAIR), lambda i: (i % tpb, 0)),
                  pl.BlockSpec((tm, PAIR), lambda i: (i % tpb, 0))],
        out_specs=[row(d_a), row(d_a),
                   pl.BlockSpec((None, n_heads, V_ROWS, tm), lambda i: (i // tpb, 0, 0, i % tpb)),
                   pl.BlockSpec((None, n_heads * n_blocks, tm), lambda i: (i // tpb, 0, i % tpb)),
                   row(d_r), row(d_r), row(d_r), row(d_r)],
        out_shape=[jax.ShapeDtypeStruct((t, d_a), bf), jax.ShapeDtypeStruct((t, d_a), bf),
                   jax.ShapeDtypeStruct((batch, n_heads, V_ROWS, seq), bf),
                   jax.ShapeDtypeStruct((batch, n_heads * n_blocks, seq), jnp.float32),
                   jax.ShapeDtypeStruct((t, d_r), bf), jax.ShapeDtypeStruct((t, d_r), bf),
                   jax.ShapeDtypeStruct((t, d_r), bf), jax.ShapeDtypeStruct((t, d_r), bf)],
        scratch_shapes=[pltpu.VMEM((n_heads * n_blocks, d_a), jnp.float32)],
        compiler_params=pltpu.CompilerParams(
            dimension_semantics=("arbitrary",), vmem_limit_bytes=VMEM_LIMIT),
        name="projection",
    )(x2, g, wqk, wvt, wqkr, wvgr, cos_t, sin_t)


def _moba_kernel(rb_ref, q_ref, k_ref, vt_ref, gate_ref, bias_ref, o_ref, sel_ref, *, n_blocks):
    hp = pl.program_id(1)
    c = pl.program_id(2)
    q = q_ref[...]
    lane = lax.broadcasted_iota(jnp.int32, q.shape, 1)
    qm = [jnp.where(lane < HEAD_DIM, q, jnp.zeros_like(q)),
          jnp.where(lane >= HEAD_DIM, q, jnp.zeros_like(q))]

    blk = lax.broadcasted_iota(jnp.int32, (n_blocks, MOBA_BLOCK), 0)
    for h in range(2):
        g = jnp.where(blk < c, gate_ref[h * n_blocks:(h + 1) * n_blocks, :], -jnp.inf)
        sel = jnp.zeros(g.shape, jnp.float32)
        for _ in range(MOBA_TOPK):
            top = jnp.max(g, axis=0, keepdims=True)
            idx = jnp.min(jnp.where(g == top, blk, n_blocks), axis=0, keepdims=True)
            pick = (blk == idx) & (blk < c)
            sel = jnp.where(pick, 1.0, sel)
            g = jnp.where(pick, -jnp.inf, g)
        sel_ref[h] = sel

    def tile(h, j, bias):
        rows = pl.ds(pl.multiple_of(j * MOBA_BLOCK, MOBA_BLOCK), MOBA_BLOCK)
        s = lax.dot_general(k_ref[rows, :], qm[h], NT, preferred_element_type=jnp.float32) + bias
        m = jnp.max(s, axis=0, keepdims=True)
        p = jnp.exp(s - m).astype(jnp.bfloat16)
        return m, jnp.dot(vt_ref[h, :, rows], p, preferred_element_type=jnp.float32)

    def merge(state, h, j, bias):
        m, acc = state
        mj, oj = tile(h, j, bias)
        chosen = sel_ref[h, pl.ds(j, 1), :] > 0.5
        m_new = jnp.where(chosen, jnp.maximum(m, mj), m)
        beta = jnp.where(chosen, jnp.exp(mj - m_new), 0.0)
        return m_new, acc * jnp.exp(m - m_new) + oj * beta

    far_bias = [rb_ref[REL_BUCKETS - 1, 2 * hp + h] for h in range(2)]
    prev = jnp.maximum(c - 1, 0)
    states = []
    for h in range(2):
        st = tile(h, c, bias_ref[h, 0])
        states.append(merge(st, h, prev, bias_ref[h, 1]))

    def body(j, carry):
        return tuple(merge(carry[h], h, j, far_bias[h]) for h in range(2))

    states = lax.fori_loop(0, prev, body, tuple(states))

    outs = []
    for h in range(2):
        _, acc = states[h]
        outs.append((acc[:HEAD_DIM] / acc[HEAD_DIM:HEAD_DIM + 1]).T)
    o_ref[...] = jnp.concatenate(outs, axis=1).astype(o_ref.dtype)


def _moba(rel_bias, qa, ka, vt, gate, bias, batch, seq):
    d_a = qa.shape[1]
    n_blocks = seq // MOBA_BLOCK
    qa3 = qa.reshape(batch, seq, d_a)
    ka3 = ka.reshape(batch, seq, d_a)
    out = pl.pallas_call(
        functools.partial(_moba_kernel, n_blocks=n_blocks),
        grid=(batch, d_a // PAIR, n_blocks),
        in_specs=[pl.BlockSpec(memory_space=pltpu.SMEM),
                  pl.BlockSpec((None, MOBA_BLOCK, PAIR), lambda b, p, c: (b, c, p)),
                  pl.BlockSpec((None, seq, PAIR), lambda b, p, c: (b, 0, p)),
                  pl.BlockSpec((None, 2, V_ROWS, seq), lambda b, p, c: (b, p, 0, 0)),
                  pl.BlockSpec((None, 2 * n_blocks, MOBA_BLOCK), lambda b, p, c: (b, p, c)),
                  pl.BlockSpec((2, 2, MOBA_BLOCK, MOBA_BLOCK), lambda b, p, c: (p, 0, 0, 0))],
        out_specs=pl.BlockSpec((None, MOBA_BLOCK, PAIR), lambda b, p, c: (b, c, p)),
        out_shape=jax.ShapeDtypeStruct((batch, seq, d_a), jnp.bfloat16),
        scratch_shapes=[pltpu.VMEM((2, n_blocks, MOBA_BLOCK), jnp.float32)],
        compiler_params=pltpu.CompilerParams(
            dimension_semantics=("arbitrary", "arbitrary", "arbitrary"),
            vmem_limit_bytes=VMEM_LIMIT),
        name="moba",
    )(rel_bias, qa3, ka3, vt, gate, bias)
    return out.reshape(batch * seq, d_a)


def _ret_kernel(q_ref, k_ref, v_ref, g_ref, dmask_ref, dstart_ref, dend_ref, cdec_ref,
                o_ref, state_ref):
    @pl.when(pl.program_id(2) == 0)
    def _():
        state_ref[...] = jnp.zeros_like(state_ref)

    q = q_ref[...]
    k = k_ref[...]
    v = v_ref[...]
    lane = lax.broadcasted_iota(jnp.int32, q.shape, 1)
    in_head = [lane < HEAD_DIM, lane >= HEAD_DIM]

    inner = jnp.zeros(q.shape, jnp.float32)
    for h in range(2):
        qm = jnp.where(in_head[h], q, jnp.zeros_like(q))
        sc = lax.dot_general(qm, k, NT, preferred_element_type=jnp.float32) * dmask_ref[h]
        yh = jnp.dot(sc.astype(jnp.bfloat16), v, preferred_element_type=jnp.float32)
        inner = jnp.where(in_head[h], yh, inner)

    state = state_ref[...]
    cross = jnp.dot(q, state.astype(jnp.bfloat16), preferred_element_type=jnp.float32)
    y = inner + cross * dstart_ref[...]

    kd = (k.astype(jnp.float32) * dend_ref[...]).astype(jnp.bfloat16)
    kv = lax.dot_general(kd, v, TN, preferred_element_type=jnp.float32)
    r = lax.broadcasted_iota(jnp.int32, state.shape, 0) // HEAD_DIM
    cc = lax.broadcasted_iota(jnp.int32, state.shape, 1) // HEAD_DIM
    same_head = r == cc
    state_ref[...] = jnp.where(same_head, state * cdec_ref[...] + kv, 0.0)

    avg = jnp.where(same_head, 1.0 / HEAD_DIM, 0.0)
    mu = jnp.dot(y, avg, precision=lax.Precision.HIGHEST, preferred_element_type=jnp.float32)
    d = y - mu
    var = jnp.dot(d * d, avg, precision=lax.Precision.HIGHEST, preferred_element_type=jnp.float32)
    yn = d * lax.rsqrt(var + EPS)
    g = g_ref[...].astype(jnp.float32)
    o_ref[...] = (g * jax.nn.sigmoid(g) * yn).astype(o_ref.dtype)


def _retention(qr, kr, vr, gr, dmask, dstart, dend, cdec, batch, seq):
    d_r = qr.shape[1]
    n_chunks = seq // RET_CHUNK
    blk = pl.BlockSpec((None, RET_CHUNK, PAIR), lambda b, p, c: (b, c, p))
    r3 = lambda a: a.reshape(batch, seq, d_r)
    out = pl.pallas_call(
        _ret_kernel,
        grid=(batch, d_r // PAIR, n_chunks),
        in_specs=[blk, blk, blk, blk,
                  pl.BlockSpec((2, RET_CHUNK, RET_CHUNK), lambda b, p, c: (p, 0, 0)),
                  pl.BlockSpec((RET_CHUNK, PAIR), lambda b, p, c: (0, p)),
                  pl.BlockSpec((RET_CHUNK, PAIR), lambda b, p, c: (0, p)),
                  pl.BlockSpec((1, PAIR), lambda b, p, c: (0, p))],
        out_specs=blk,
        out_shape=jax.ShapeDtypeStruct((batch, seq, d_r), jnp.bfloat16),
        scratch_shapes=[pltpu.VMEM((PAIR, PAIR), jnp.float32)],
        compiler_params=pltpu.CompilerParams(
            dimension_semantics=("arbitrary", "arbitrary", "arbitrary")),
        name="retention",
    )(r3(qr), r3(kr), r3(vr), r3(gr), dmask, dstart, dend, cdec)
    return out.reshape(batch * seq, d_r)


def _ffn_kernel(ya_ref, yr_ref, x_ref, woa_ref, wor_ref, g1_ref, g2_ref, wup_ref, cw_ref, cb_ref,
                wdn_ref, g3_ref, o_ref, tail_ref, *, tiles_per_batch, d_ff):
    i = pl.program_id(0)
    tm = x_ref.shape[0]

    @pl.when(i == 0)
    def _():
        tail_ref[...] = jnp.zeros_like(tail_ref)

    y = (jnp.dot(ya_ref[...], woa_ref[...], preferred_element_type=jnp.float32)
         + jnp.dot(yr_ref[...], wor_ref[...], preferred_element_type=jnp.float32))
    x1 = x_ref[...] + _rms(y, g1_ref[...])
    h2 = _rms(x1, g2_ref[...]).astype(jnp.bfloat16)

    keep_tail = (i % tiles_per_batch) != 0
    row = lax.broadcasted_iota(jnp.int32, (tm, FF_CHUNK), 0)

    def conv(cols):
        u = jnp.dot(h2, wup_ref[:, cols], preferred_element_type=jnp.float32)
        tail = jnp.where(keep_tail, tail_ref[:, cols], 0.0)
        tail_ref[:, cols] = u[tm - 8:, :]
        t1 = tail[7:8, :]
        t2 = tail[6:7, :]
        u1 = jnp.where(row == 0, t1, pltpu.roll(u, 1, axis=0))
        u2 = jnp.where(row == 0, t2, jnp.where(row == 1, t1, pltpu.roll(u, 2, axis=0)))
        w = cw_ref[:, cols]
        return cb_ref[:, cols] + w[2:3] * u + w[1:2] * u1 + w[0:1] * u2

    acc = jnp.zeros((tm, o_ref.shape[1]), jnp.float32)
    for ci in range(d_ff // FF_CHUNK):
        a = conv(slice(ci * FF_CHUNK, (ci + 1) * FF_CHUNK))
        b = conv(slice(d_ff + ci * FF_CHUNK, d_ff + (ci + 1) * FF_CHUNK))
        f = (jax.nn.gelu(a, approximate=True) * b).astype(jnp.bfloat16)
        acc = acc + jnp.dot(f, wdn_ref[ci * FF_CHUNK:(ci + 1) * FF_CHUNK, :],
                            preferred_element_type=jnp.float32)
    o_ref[...] = x1 + _rms(acc, g3_ref[...])


def _out_ffn(ya, yr, x2, woa, wor, g1, g2, wup, cw, cb, wdn, g3, seq):
    t, d = x2.shape
    d_ff = wdn.shape[0]
    tm = ROW_TILE
    row = lambda w: pl.BlockSpec((tm, w), lambda i: (i, 0))
    return pl.pallas_call(
        functools.partial(_ffn_kernel, tiles_per_batch=seq // tm, d_ff=d_ff),
        grid=(t // tm,),
        in_specs=[row(ya.shape[1]), row(yr.shape[1]), row(d),
                  _const_spec(woa.shape), _const_spec(wor.shape),
                  _const_spec((1, d)), _const_spec((1, d)), _const_spec(wup.shape),
                  _const_spec(cw.shape), _const_spec(cb.shape), _const_spec(wdn.shape),
                  _const_spec((1, d))],
        out_specs=row(d),
        out_shape=jax.ShapeDtypeStruct((t, d), jnp.float32),
        scratch_shapes=[pltpu.VMEM((8, 2 * d_ff), jnp.float32)],
        compiler_params=pltpu.CompilerParams(
            dimension_semantics=("arbitrary",), vmem_limit_bytes=VMEM_LIMIT),
        name="out_ffn",
    )(ya, yr, x2, woa, wor, g1, g2, wup, cw, cb, wdn, g3)


def _rotary_tables(seq):
    inv = ROPE_BASE ** (-jnp.arange(0, HEAD_DIM, 2, dtype=jnp.float32) / HEAD_DIM)
    ang = jnp.arange(seq, dtype=jnp.float32)[:, None] * inv[None, :]
    cos, sin = jnp.cos(ang), jnp.sin(ang)
    cos_t = jnp.tile(jnp.concatenate([cos, cos], axis=1), (1, PAIR // HEAD_DIM))
    sin_t = jnp.tile(jnp.concatenate([-sin, sin], axis=1), (1, PAIR // HEAD_DIM))
    return cos_t, sin_t


def _decay_tables(n_heads):
    c = RET_CHUNK
    log_gamma = jnp.log(1.0 - 2.0 ** (-5.0 - jnp.arange(n_heads, dtype=jnp.float32)))
    n = jnp.arange(c, dtype=jnp.float32)
    rel = n[:, None] - n[None, :]
    dmask = jnp.where(rel >= 0, jnp.exp(jnp.maximum(rel, 0.0)[None] * log_gamma[:, None, None]), 0.0)
    dend = jnp.exp((c - 1.0 - n)[:, None] * log_gamma[None, :])
    dstart = jnp.exp((n + 1.0)[:, None] * log_gamma[None, :])
    cdec = jnp.exp(c * log_gamma)[None, :]
    wide = lambda a: jnp.repeat(a, HEAD_DIM, axis=1)
    return dmask, wide(dstart), wide(dend), wide(cdec)


def kernel(x, norm_mix_pre, w_in, rel_bias, w_out, norm_mix_post, norm_ffn_pre, w_up, conv_w,
           conv_b, w_down, norm_ffn_post):
    batch, seq, d = x.shape
    depth = w_in.shape[0]
    d_a = w_out.shape[1] // 2
    d_r = w_out.shape[1] - d_a
    assert seq % ROW_TILE == 0 and ROW_TILE % MOBA_BLOCK == 0 and MOBA_BLOCK == RET_CHUNK
    assert w_in.shape[2] == 3 * d_a + 4 * d_r and d_a % PAIR == 0 and d_r % PAIR == 0
    assert w_down.shape[1] % FF_CHUNK == 0

    bf = jnp.bfloat16
    cos_t, sin_t = _rotary_tables(seq)
    dmask, dstart, dend, cdec = _decay_tables(d_r // HEAD_DIM)
    bias = _bias_tables(rel_bias)

    x2 = x.reshape(batch * seq, d)
    for l in range(depth):
        w = w_in[l].astype(bf)
        wqk = w[:, :2 * d_a]
        wvt = w[:, 2 * d_a:3 * d_a].T
        wqkr = w[:, 3 * d_a:3 * d_a + 2 * d_r]
        wvgr = w[:, 3 * d_a + 2 * d_r:]
        qa, ka, vt, gate, qr, kr, vr, gr = _projection(
            x2, norm_mix_pre[l][None], wqk, wvt, wqkr, wvgr, cos_t, sin_t, batch, seq)
        ya = _moba(rel_bias, qa, ka, vt, gate, bias, batch, seq)
        yr = _retention(qr, kr, vr, gr, dmask, dstart, dend, cdec, batch, seq)
        wo = w_out[l].astype(bf)
        x2 = _out_ffn(ya, yr, x2, wo[:d_a], wo[d_a:], norm_mix_post[l][None],
                      norm_ffn_pre[l][None], w_up[l].astype(bf), conv_w[l], conv_b[l][None],
                      w_down[l].astype(bf), norm_ffn_post[l][None], seq)
    return x2.reshape(batch, seq, d)
```

```python
import functools
import math

import jax
import jax.numpy as jnp
from jax import lax
from jax.experimental import pallas as pl
from jax.experimental.pallas import tpu as pltpu

HEAD_DIM = 64
PAIR = 2 * HEAD_DIM
MOBA_BLOCK = 256
MOBA_TOPK = 3
RET_CHUNK = 256
REL_BUCKETS = 32
REL_MAX_DIST = 128
ROPE_BASE = 10000.0
CONV_WIDTH = 3
EPS = 1e-6
NEG = -1e30
LOG2E = math.log2(math.e)
Q_SCALE = HEAD_DIM ** -0.5 * LOG2E
V_ROWS = HEAD_DIM + 16
ROW_TILE = 512
FF_CHUNK = 256
VMEM_LIMIT = 56 * 1024 * 1024

NT = (((1,), (1,)), ((), ()))
TN = (((0,), (0,)), ((), ()))


def _rms(x, g):
    return x * lax.rsqrt(jnp.mean(x * x, axis=-1, keepdims=True) + EPS) * g


def _bias_kernel(rb_ref, o_ref):
    h = pl.program_id(0)
    key = lax.broadcasted_iota(jnp.int32, (MOBA_BLOCK, MOBA_BLOCK), 0)
    qry = lax.broadcasted_iota(jnp.int32, (MOBA_BLOCK, MOBA_BLOCK), 1)
    max_exact = REL_BUCKETS // 2
    for which in range(2):
        rel = qry - key + which * MOBA_BLOCK
        n = jnp.maximum(rel, 0)
        n_f = jnp.maximum(n, 1).astype(jnp.float32)
        large = max_exact + (jnp.log(n_f / max_exact) / math.log(REL_MAX_DIST / max_exact)
                             * (REL_BUCKETS - max_exact)).astype(jnp.int32)
        large = jnp.minimum(large, REL_BUCKETS - 1)
        bucket = jnp.where(n < max_exact, n, large)
        val = jnp.zeros((MOBA_BLOCK, MOBA_BLOCK), jnp.float32)
        for b in range(REL_BUCKETS):
            val = jnp.where(bucket == b, rb_ref[b, h], val)
        o_ref[0, which] = jnp.where(rel >= 0, val * LOG2E, NEG)


def _bias_tables(rel_bias):
    n_heads = rel_bias.shape[1]
    return pl.pallas_call(
        _bias_kernel,
        grid=(n_heads,),
        in_specs=[pl.BlockSpec(memory_space=pltpu.SMEM)],
        out_specs=pl.BlockSpec((1, 2, MOBA_BLOCK, MOBA_BLOCK), lambda h: (h, 0, 0, 0)),
        out_shape=jax.ShapeDtypeStruct((n_heads, 2, MOBA_BLOCK, MOBA_BLOCK), jnp.float32),
        name="bias_tables",
    )(rel_bias)


def _proj_kernel(x_ref, g_ref, wqvt_ref, wk_ref, wqkr_ref, wvgr_ref, cos_ref, sin_ref,
                 qat_ref, ka_ref, vt_ref, gate_ref, qr_ref, kr_ref, vr_ref, gr_ref,
                 kmt_ref, *, tiles_per_batch, d_a, d_r):
    t_in_b = pl.program_id(0) % tiles_per_batch
    n_heads = d_a // HEAD_DIM
    tm = x_ref.shape[0]
    n_blocks = tiles_per_batch * (tm // MOBA_BLOCK)

    @pl.when(t_in_b == 0)
    def _():
        kmt_ref[...] = jnp.zeros_like(kmt_ref)

    hb = _rms(x_ref[...], g_ref[...]).astype(jnp.bfloat16)

    qvt = lax.dot_general(wqvt_ref[...], hb, NT, preferred_element_type=jnp.float32)
    qt = qvt[:d_a]
    qat_ref[...] = (qt * Q_SCALE).astype(jnp.bfloat16)
    k = jnp.dot(hb, wk_ref[...], preferred_element_type=jnp.float32)
    ka_ref[...] = k.astype(jnp.bfloat16)

    lane_head = lax.broadcasted_iota(jnp.int32, (1, d_a), 1) // HEAD_DIM
    for half in range(tm // MOBA_BLOCK):
        rows = slice(half * MOBA_BLOCK, (half + 1) * MOBA_BLOCK)
        gate_ref[:, rows] = jnp.dot(
            kmt_ref[...], qt[:, rows], precision=lax.Precision.HIGHEST,
            preferred_element_type=jnp.float32)
        k_mean = jnp.mean(k[rows], axis=0, keepdims=True)
        j = t_in_b * (tm // MOBA_BLOCK) + half
        for hh in range(n_heads):
            kmt_ref[pl.ds(hh * n_blocks + j, 1), :] = jnp.where(lane_head == hh, k_mean, 0.0)

    vt = qvt[d_a:].astype(jnp.bfloat16)
    for hh in range(n_heads):
        vt_ref[hh, 0:HEAD_DIM, :] = vt[hh * HEAD_DIM:(hh + 1) * HEAD_DIM, :]
        vt_ref[hh, HEAD_DIM:V_ROWS, :] = jnp.ones((V_ROWS - HEAD_DIM, tm), jnp.bfloat16)

    qkr = jnp.dot(hb, wqkr_ref[...], preferred_element_type=jnp.float32)
    cos = cos_ref[...]
    sin = sin_ref[...]
    first_half = (lax.broadcasted_iota(jnp.int32, (tm, PAIR), 1) % HEAD_DIM) < HEAD_DIM // 2
    for cidx in range(2 * d_r // PAIR):
        xc = qkr[:, cidx * PAIR:(cidx + 1) * PAIR]
        sw = jnp.where(first_half,
                       pltpu.roll(xc, PAIR - HEAD_DIM // 2, axis=1),
                       pltpu.roll(xc, HEAD_DIM // 2, axis=1))
        rc = xc * cos + sw * sin
        if cidx < d_r // PAIR:
            qr_ref[:, cidx * PAIR:(cidx + 1) * PAIR] = rc.astype(jnp.bfloat16)
        else:
            c2 = cidx - d_r // PAIR
            kr_ref[:, c2 * PAIR:(c2 + 1) * PAIR] = (rc * HEAD_DIM ** -0.5).astype(jnp.bfloat16)

    vg = jnp.dot(hb, wvgr_ref[...], preferred_element_type=jnp.float32)
    vr_ref[...] = vg[:, :d_r].astype(jnp.bfloat16)
    gr_ref[...] = vg[:, d_r:].astype(jnp.bfloat16)


def _const_spec(shape):
    zeros = (0,) * len(shape)
    return pl.BlockSpec(shape, lambda *_: zeros, pipeline_mode=pl.Buffered(1))


def _projection(x2, g, wqvt, wk, wqkr, wvgr, cos_t, sin_t, batch, seq):
    t, d = x2.shape
    d_a = wk.shape[1]
    d_r = wqkr.shape[1] // 2
    n_heads = d_a // HEAD_DIM
    tm = ROW_TILE
    tpb = seq // tm
    n_blocks = seq // MOBA_BLOCK
    row = lambda w: pl.BlockSpec((tm, w), lambda i: (i, 0))
    bf = jnp.bfloat16
    return pl.pallas_call(
        functools.partial(_proj_kernel, tiles_per_batch=tpb, d_a=d_a, d_r=d_r),
        grid=(t // tm,),
        in_specs=[row(d), _const_spec((1, d)), _const_spec(wqvt.shape), _const_spec(wk.shape),
                  _const_spec(wqkr.shape), _const_spec(wvgr.shape),
                  pl.BlockSpec((tm, PAIR), lambda i: (i % tpb, 0)),
                  pl.BlockSpec((tm, PAIR), lambda i: (i % tpb, 0))],
        out_specs=[pl.BlockSpec((None, d_a, tm), lambda i: (i // tpb, 0, i % tpb)), row(d_a),
                   pl.BlockSpec((None, n_heads, V_ROWS, tm), lambda i: (i // tpb, 0, 0, i % tpb)),
                   pl.BlockSpec((None, n_heads * n_blocks, tm), lambda i: (i // tpb, 0, i % tpb)),
                   row(d_r), row(d_r), row(d_r), row(d_r)],
        out_shape=[jax.ShapeDtypeStruct((batch, d_a, seq), bf), jax.ShapeDtypeStruct((t, d_a), bf),
                   jax.ShapeDtypeStruct((batch, n_heads, V_ROWS, seq), bf),
                   jax.ShapeDtypeStruct((batch, n_heads * n_blocks, seq), jnp.float32),
                   jax.ShapeDtypeStruct((t, d_r), bf), jax.ShapeDtypeStruct((t, d_r), bf),
                   jax.ShapeDtypeStruct((t, d_r), bf), jax.ShapeDtypeStruct((t, d_r), bf)],
        scratch_shapes=[pltpu.VMEM((n_heads * n_blocks, d_a), jnp.float32)],
        compiler_params=pltpu.CompilerParams(
            dimension_semantics=("arbitrary",), vmem_limit_bytes=VMEM_LIMIT),
        name="projection",
    )(x2, g, wqvt, wk, wqkr, wvgr, cos_t, sin_t)


def _moba_kernel(rb_ref, qt_ref, k_ref, vt_ref, gate_ref, bias_ref, o_ref, sel_ref, s_ref, mj_ref,
                 *, n_blocks):
    hp = pl.program_id(1)
    c = pl.program_id(2)
    qt = qt_ref[...]
    feat = lax.broadcasted_iota(jnp.int32, qt.shape, 0)
    qm = [jnp.where(feat < HEAD_DIM, qt, jnp.zeros_like(qt)),
          jnp.where(feat >= HEAD_DIM, qt, jnp.zeros_like(qt))]

    blk = lax.broadcasted_iota(jnp.int32, (n_blocks, MOBA_BLOCK), 0)
    for h in range(2):
        g = jnp.where(blk < c, gate_ref[h * n_blocks:(h + 1) * n_blocks, :], -jnp.inf)
        sel = jnp.zeros(g.shape, jnp.float32)
        for _ in range(MOBA_TOPK):
            top = jnp.max(g, axis=0, keepdims=True)
            idx = jnp.min(jnp.where(g == top, blk, n_blocks), axis=0, keepdims=True)
            pick = (blk == idx) & (blk < c)
            sel = jnp.where(pick, 1.0, sel)
            g = jnp.where(pick, -jnp.inf, g)
        sel_ref[h] = sel

    def tile(h, j, bias_tile, bias_const):
        rows = pl.ds(pl.multiple_of(j * MOBA_BLOCK, MOBA_BLOCK), MOBA_BLOCK)
        s = jnp.dot(k_ref[rows, :], qm[h], preferred_element_type=jnp.float32)
        if bias_tile is not None:
            s = s + bias_tile
        m = jnp.max(s, axis=0, keepdims=True)
        p = jnp.exp2(s - m).astype(jnp.bfloat16)
        if bias_const is not None:
            m = m + bias_const
        return m, jnp.dot(vt_ref[h, :, rows], p, preferred_element_type=jnp.float32)

    def merge(state, h, j, bias_tile, bias_const):
        m, acc = state
        mj, oj = tile(h, j, bias_tile, bias_const)
        chosen = sel_ref[h, pl.ds(j, 1), :] > 0.5
        m_new = jnp.where(chosen, jnp.maximum(m, mj), m)
        beta = jnp.where(chosen, jnp.exp2(mj - m_new), 0.0)
        return m_new, acc * jnp.exp2(m - m_new) + oj * beta

    far_bias = [rb_ref[REL_BUCKETS - 1, 2 * hp + h] * LOG2E for h in range(2)]
    prev = jnp.maximum(c - 1, 0)
    states = []
    for h in range(2):
        st = tile(h, c, bias_ref[h, 0], None)
        states.append(merge(st, h, prev, bias_ref[h, 1], None))

    n_far = prev
    last = jnp.maximum(n_far - 1, 0)

    def block_rows(j):
        return pl.ds(pl.multiple_of(j * MOBA_BLOCK, MOBA_BLOCK), MOBA_BLOCK)

    def scores(slot, j):
        for h in range(2):
            s = jnp.dot(k_ref[block_rows(j), :], qm[h], preferred_element_type=jnp.float32)
            s_ref[slot, h] = s
            mj_ref[slot, h] = jnp.max(s, axis=0, keepdims=True)

    def consume(slot, j, valid, carry):
        probs = []
        for h in range(2):
            mj = mj_ref[slot, h]
            probs.append((mj, jnp.exp2(s_ref[slot, h] - mj).astype(jnp.bfloat16)))
        out = []
        for h in range(2):
            mj, p = probs[h]
            oj = jnp.dot(vt_ref[h, :, block_rows(j)], p, preferred_element_type=jnp.float32)
            m, acc = carry[h]
            mj = mj + far_bias[h]
            chosen = (sel_ref[h, pl.ds(j, 1), :] > 0.5) & valid
            m_new = jnp.where(chosen, jnp.maximum(m, mj), m)
            beta = jnp.where(chosen, jnp.exp2(mj - m_new), 0.0)
            out.append((m_new, acc * jnp.exp2(m - m_new) + oj * beta))
        return tuple(out)

    scores(0, 0)

    def body(t, carry):
        j0 = 2 * t
        j1 = jnp.minimum(j0 + 1, last)
        j2 = jnp.minimum(j0 + 2, last)
        scores(1, j1)
        carry = consume(0, j0, True, carry)
        scores(0, j2)
        return consume(1, j1, j0 + 1 < n_far, carry)

    states = lax.fori_loop(0, (n_far + 1) // 2, body, tuple(states))

    outs = []
    for h in range(2):
        _, acc = states[h]
        outs.append((acc[:HEAD_DIM] / acc[HEAD_DIM:HEAD_DIM + 1]).T)
    o_ref[...] = jnp.concatenate(outs, axis=1).astype(o_ref.dtype)


def _moba(rel_bias, qat, ka, vt, gate, bias, batch, seq):
    d_a = ka.shape[1]
    n_blocks = seq // MOBA_BLOCK
    ka3 = ka.reshape(batch, seq, d_a)
    out = pl.pallas_call(
        functools.partial(_moba_kernel, n_blocks=n_blocks),
        grid=(batch, d_a // PAIR, n_blocks),
        in_specs=[pl.BlockSpec(memory_space=pltpu.SMEM),
                  pl.BlockSpec((None, PAIR, MOBA_BLOCK), lambda b, p, c: (b, p, c)),
                  pl.BlockSpec((None, seq, PAIR), lambda b, p, c: (b, 0, p)),
                  pl.BlockSpec((None, 2, V_ROWS, seq), lambda b, p, c: (b, p, 0, 0)),
                  pl.BlockSpec((None, 2 * n_blocks, MOBA_BLOCK), lambda b, p, c: (b, p, c)),
                  pl.BlockSpec((2, 2, MOBA_BLOCK, MOBA_BLOCK), lambda b, p, c: (p, 0, 0, 0))],
        out_specs=pl.BlockSpec((None, MOBA_BLOCK, PAIR), lambda b, p, c: (b, c, p)),
        out_shape=jax.ShapeDtypeStruct((batch, seq, d_a), jnp.bfloat16),
        scratch_shapes=[pltpu.VMEM((2, n_blocks, MOBA_BLOCK), jnp.float32),
                        pltpu.VMEM((2, 2, MOBA_BLOCK, MOBA_BLOCK), jnp.float32),
                        pltpu.VMEM((2, 2, 1, MOBA_BLOCK), jnp.float32)],
        compiler_params=pltpu.CompilerParams(
            dimension_semantics=("arbitrary", "arbitrary", "arbitrary"),
            vmem_limit_bytes=VMEM_LIMIT),
        name="moba",
    )(rel_bias, qat, ka3, vt, gate, bias)
    return out.reshape(batch * seq, d_a)


def _ret_kernel(q_ref, k_ref, v_ref, g_ref, dmask_ref, dstart_ref, dend_ref, cdec_ref,
                o_ref, state_ref):
    @pl.when(pl.program_id(2) == 0)
    def _():
        state_ref[...] = jnp.zeros_like(state_ref)

    q = q_ref[...]
    k = k_ref[...]
    v = v_ref[...]
    lane = lax.broadcasted_iota(jnp.int32, q.shape, 1)
    in_head = [lane < HEAD_DIM, lane >= HEAD_DIM]

    inner = jnp.zeros(q.shape, jnp.float32)
    for h in range(2):
        qm = jnp.where(in_head[h], q, jnp.zeros_like(q))
        sc = lax.dot_general(qm, k, NT, preferred_element_type=jnp.float32) * dmask_ref[h]
        yh = jnp.dot(sc.astype(jnp.bfloat16), v, preferred_element_type=jnp.float32)
        inner = jnp.where(in_head[h], yh, inner)

    state = state_ref[...]
    cross = jnp.dot(q, state.astype(jnp.bfloat16), preferred_element_type=jnp.float32)
    y = inner + cross * dstart_ref[...]

    kd = (k.astype(jnp.float32) * dend_ref[...]).astype(jnp.bfloat16)
    kv = lax.dot_general(kd, v, TN, preferred_element_type=jnp.float32)
    r = lax.broadcasted_iota(jnp.int32, state.shape, 0) // HEAD_DIM
    cc = lax.broadcasted_iota(jnp.int32, state.shape, 1) // HEAD_DIM
    same_head = r == cc
    state_ref[...] = jnp.where(same_head, state * cdec_ref[...] + kv, 0.0)

    avg = jnp.where(same_head, 1.0 / HEAD_DIM, 0.0)
    mu = jnp.dot(y, avg, precision=lax.Precision.HIGHEST, preferred_element_type=jnp.float32)
    d = y - mu
    var = jnp.dot(d * d, avg, precision=lax.Precision.HIGHEST, preferred_element_type=jnp.float32)
    yn = d * lax.rsqrt(var + EPS)
    g = g_ref[...].astype(jnp.float32)
    o_ref[...] = (g * jax.nn.sigmoid(g) * yn).astype(o_ref.dtype)


def _retention(qr, kr, vr, gr, dmask, dstart, dend, cdec, batch, seq):
    d_r = qr.shape[1]
    n_chunks = seq // RET_CHUNK
    blk = pl.BlockSpec((None, RET_CHUNK, PAIR), lambda b, p, c: (b, c, p))
    r3 = lambda a: a.reshape(batch, seq, d_r)
    out = pl.pallas_call(
        _ret_kernel,
        grid=(batch, d_r // PAIR, n_chunks),
        in_specs=[blk, blk, blk, blk,
                  pl.BlockSpec((2, RET_CHUNK, RET_CHUNK), lambda b, p, c: (p, 0, 0)),
                  pl.BlockSpec((RET_CHUNK, PAIR), lambda b, p, c: (0, p)),
                  pl.BlockSpec((RET_CHUNK, PAIR), lambda b, p, c: (0, p)),
                  pl.BlockSpec((1, PAIR), lambda b, p, c: (0, p))],
        out_specs=blk,
        out_shape=jax.ShapeDtypeStruct((batch, seq, d_r), jnp.bfloat16),
        scratch_shapes=[pltpu.VMEM((PAIR, PAIR), jnp.float32)],
        compiler_params=pltpu.CompilerParams(
            dimension_semantics=("arbitrary", "arbitrary", "arbitrary")),
        name="retention",
    )(r3(qr), r3(kr), r3(vr), r3(gr), dmask, dstart, dend, cdec)
    return out.reshape(batch * seq, d_r)


def _ffn_kernel(ya_ref, yr_ref, x_ref, woa_ref, wor_ref, g1_ref, g2_ref, wup_ref, cw_ref, cb_ref,
                wdn_ref, g3_ref, o_ref, tail_ref, *, tiles_per_batch, d_ff):
    i = pl.program_id(0)
    tm = x_ref.shape[0]

    @pl.when(i == 0)
    def _():
        tail_ref[...] = jnp.zeros_like(tail_ref)

    y = (jnp.dot(ya_ref[...], woa_ref[...], preferred_element_type=jnp.float32)
         + jnp.dot(yr_ref[...], wor_ref[...], preferred_element_type=jnp.float32))
    x1 = x_ref[...] + _rms(y, g1_ref[...])
    h2 = _rms(x1, g2_ref[...]).astype(jnp.bfloat16)

    keep_tail = (i % tiles_per_batch) != 0
    row = lax.broadcasted_iota(jnp.int32, (tm, FF_CHUNK), 0)

    def conv(cols):
        u = jnp.dot(h2, wup_ref[:, cols], preferred_element_type=jnp.float32)
        tail = jnp.where(keep_tail, tail_ref[:, cols], 0.0)
        tail_ref[:, cols] = u[tm - 8:, :]
        t1 = tail[7:8, :]
        t2 = tail[6:7, :]
        u1 = jnp.where(row == 0, t1, pltpu.roll(u, 1, axis=0))
        u2 = jnp.where(row == 0, t2, jnp.where(row == 1, t1, pltpu.roll(u, 2, axis=0)))
        w = cw_ref[:, cols]
        return cb_ref[:, cols] + w[2:3] * u + w[1:2] * u1 + w[0:1] * u2

    acc = jnp.zeros((tm, o_ref.shape[1]), jnp.float32)
    for ci in range(d_ff // FF_CHUNK):
        a = conv(slice(ci * FF_CHUNK, (ci + 1) * FF_CHUNK))
        b = conv(slice(d_ff + ci * FF_CHUNK, d_ff + (ci + 1) * FF_CHUNK))
        f = (jax.nn.gelu(a, approximate=True) * b).astype(jnp.bfloat16)
        acc = acc + jnp.dot(f, wdn_ref[ci * FF_CHUNK:(ci + 1) * FF_CHUNK, :],
                            preferred_element_type=jnp.float32)
    o_ref[...] = x1 + _rms(acc, g3_ref[...])


def _out_ffn(ya, yr, x2, woa, wor, g1, g2, wup, cw, cb, wdn, g3, seq):
    t, d = x2.shape
    d_ff = wdn.shape[0]
    tm = ROW_TILE
    row = lambda w: pl.BlockSpec((tm, w), lambda i: (i, 0))
    return pl.pallas_call(
        functools.partial(_ffn_kernel, tiles_per_batch=seq // tm, d_ff=d_ff),
        grid=(t // tm,),
        in_specs=[row(ya.shape[1]), row(yr.shape[1]), row(d),
                  _const_spec(woa.shape), _const_spec(wor.shape),
                  _const_spec((1, d)), _const_spec((1, d)), _const_spec(wup.shape),
                  _const_spec(cw.shape), _const_spec(cb.shape), _const_spec(wdn.shape),
                  _const_spec((1, d))],
        out_specs=row(d),
        out_shape=jax.ShapeDtypeStruct((t, d), jnp.float32),
        scratch_shapes=[pltpu.VMEM((8, 2 * d_ff), jnp.float32)],
        compiler_params=pltpu.CompilerParams(
            dimension_semantics=("arbitrary",), vmem_limit_bytes=VMEM_LIMIT),
        name="out_ffn",
    )(ya, yr, x2, woa, wor, g1, g2, wup, cw, cb, wdn, g3)


def _rotary_tables(seq):
    inv = ROPE_BASE ** (-jnp.arange(0, HEAD_DIM, 2, dtype=jnp.float32) / HEAD_DIM)
    ang = jnp.arange(seq, dtype=jnp.float32)[:, None] * inv[None, :]
    cos, sin = jnp.cos(ang), jnp.sin(ang)
    cos_t = jnp.tile(jnp.concatenate([cos, cos], axis=1), (1, PAIR // HEAD_DIM))
    sin_t = jnp.tile(jnp.concatenate([-sin, sin], axis=1), (1, PAIR // HEAD_DIM))
    return cos_t, sin_t


def _decay_tables(n_heads):
    c = RET_CHUNK
    log_gamma = jnp.log(1.0 - 2.0 ** (-5.0 - jnp.arange(n_heads, dtype=jnp.float32)))
    n = jnp.arange(c, dtype=jnp.float32)
    rel = n[:, None] - n[None, :]
    dmask = jnp.where(rel >= 0, jnp.exp(jnp.maximum(rel, 0.0)[None] * log_gamma[:, None, None]), 0.0)
    dend = jnp.exp((c - 1.0 - n)[:, None] * log_gamma[None, :])
    dstart = jnp.exp((n + 1.0)[:, None] * log_gamma[None, :])
    cdec = jnp.exp(c * log_gamma)[None, :]
    wide = lambda a: jnp.repeat(a, HEAD_DIM, axis=1)
    return dmask, wide(dstart), wide(dend), wide(cdec)


def kernel(x, norm_mix_pre, w_in, rel_bias, w_out, norm_mix_post, norm_ffn_pre, w_up, conv_w,
           conv_b, w_down, norm_ffn_post):
    batch, seq, d = x.shape
    depth = w_in.shape[0]
    d_a = w_out.shape[1] // 2
    d_r = w_out.shape[1] - d_a
    assert seq % ROW_TILE == 0 and ROW_TILE % MOBA_BLOCK == 0 and MOBA_BLOCK == RET_CHUNK
    assert w_in.shape[2] == 3 * d_a + 4 * d_r and d_a % PAIR == 0 and d_r % PAIR == 0
    assert w_down.shape[1] % FF_CHUNK == 0

    bf = jnp.bfloat16
    cos_t, sin_t = _rotary_tables(seq)
    dmask, dstart, dend, cdec = _decay_tables(d_r // HEAD_DIM)
    bias = _bias_tables(rel_bias)

    x2 = x.reshape(batch * seq, d)
    for l in range(depth):
        w = w_in[l].astype(bf)
        wqvt = jnp.concatenate([w[:, :d_a], w[:, 2 * d_a:3 * d_a]], axis=1).T
        wk = w[:, d_a:2 * d_a]
        wqkr = w[:, 3 * d_a:3 * d_a + 2 * d_r]
        wvgr = w[:, 3 * d_a + 2 * d_r:]
        qat, ka, vt, gate, qr, kr, vr, gr = _projection(
            x2, norm_mix_pre[l][None], wqvt, wk, wqkr, wvgr, cos_t, sin_t, batch, seq)
        ya = _moba(rel_bias, qat, ka, vt, gate, bias, batch, seq)
        yr = _retention(qr, kr, vr, gr, dmask, dstart, dend, cdec, batch, seq)
        wo = w_out[l].astype(bf)
        x2 = _out_ffn(ya, yr, x2, wo[:d_a], wo[d_a:], norm_mix_post[l][None],
                      norm_ffn_pre[l][None], w_up[l].astype(bf), conv_w[l], conv_b[l][None],
                      w_down[l].astype(bf), norm_ffn_post[l][None], seq)
    return x2.reshape(batch, seq, d)
```

```python
import functools
import math

import jax
import jax.numpy as jnp
from jax import lax
from jax.experimental import pallas as pl
from jax.experimental.pallas import tpu as pltpu

HEAD_DIM = 64
PAIR = 2 * HEAD_DIM
MOBA_BLOCK = 256
MOBA_TOPK = 3
RET_CHUNK = 256
REL_BUCKETS = 32
REL_MAX_DIST = 128
ROPE_BASE = 10000.0
CONV_WIDTH = 3
EPS = 1e-6
NEG = -1e30
LOG2E = math.log2(math.e)
Q_SCALE = HEAD_DIM ** -0.5 * LOG2E
V_ROWS = HEAD_DIM + 16
ROW_TILE = 512
FF_CHUNK = 256
VMEM_LIMIT = 56 * 1024 * 1024

NT = (((1,), (1,)), ((), ()))
TN = (((0,), (0,)), ((), ()))


def _rms(x, g):
    return x * lax.rsqrt(jnp.mean(x * x, axis=-1, keepdims=True) + EPS) * g


def _bias_kernel(rb_ref, o_ref):
    h = pl.program_id(0)
    key = lax.broadcasted_iota(jnp.int32, (MOBA_BLOCK, MOBA_BLOCK), 0)
    qry = lax.broadcasted_iota(jnp.int32, (MOBA_BLOCK, MOBA_BLOCK), 1)
    max_exact = REL_BUCKETS // 2
    for which in range(2):
        rel = qry - key + which * MOBA_BLOCK
        n = jnp.maximum(rel, 0)
        n_f = jnp.maximum(n, 1).astype(jnp.float32)
        large = max_exact + (jnp.log(n_f / max_exact) / math.log(REL_MAX_DIST / max_exact)
                             * (REL_BUCKETS - max_exact)).astype(jnp.int32)
        large = jnp.minimum(large, REL_BUCKETS - 1)
        bucket = jnp.where(n < max_exact, n, large)
        val = jnp.zeros((MOBA_BLOCK, MOBA_BLOCK), jnp.float32)
        for b in range(REL_BUCKETS):
            val = jnp.where(bucket == b, rb_ref[b, h], val)
        o_ref[0, which] = jnp.where(rel >= 0, val * LOG2E, NEG)


def _bias_tables(rel_bias):
    n_heads = rel_bias.shape[1]
    return pl.pallas_call(
        _bias_kernel,
        grid=(n_heads,),
        in_specs=[pl.BlockSpec(memory_space=pltpu.SMEM)],
        out_specs=pl.BlockSpec((1, 2, MOBA_BLOCK, MOBA_BLOCK), lambda h: (h, 0, 0, 0)),
        out_shape=jax.ShapeDtypeStruct((n_heads, 2, MOBA_BLOCK, MOBA_BLOCK), jnp.float32),
        name="bias_tables",
    )(rel_bias)


def _proj_kernel(x_ref, g_ref, wqvt_ref, wk_ref, wqkr_ref, wvgr_ref, cos_ref, sin_ref,
                 qat_ref, ka_ref, vt_ref, gate_ref, qr_ref, kr_ref, vr_ref, gr_ref,
                 kmt_ref, *, tiles_per_batch, d_a, d_r):
    t_in_b = pl.program_id(0) % tiles_per_batch
    n_heads = d_a // HEAD_DIM
    tm = x_ref.shape[0]
    n_blocks = tiles_per_batch * (tm // MOBA_BLOCK)

    @pl.when(t_in_b == 0)
    def _():
        kmt_ref[...] = jnp.zeros_like(kmt_ref)

    hb = _rms(x_ref[...], g_ref[...]).astype(jnp.bfloat16)

    qvt = lax.dot_general(wqvt_ref[...], hb, NT, preferred_element_type=jnp.float32)
    qt = qvt[:d_a]
    qat_ref[...] = (qt * Q_SCALE).astype(jnp.bfloat16)
    k = jnp.dot(hb, wk_ref[...], preferred_element_type=jnp.float32)
    ka_ref[...] = k.astype(jnp.bfloat16)

    lane_head = lax.broadcasted_iota(jnp.int32, (1, d_a), 1) // HEAD_DIM
    for half in range(tm // MOBA_BLOCK):
        rows = slice(half * MOBA_BLOCK, (half + 1) * MOBA_BLOCK)
        gate_ref[:, rows] = jnp.dot(
            kmt_ref[...], qt[:, rows], precision=lax.Precision.HIGHEST,
            preferred_element_type=jnp.float32)
        k_mean = jnp.mean(k[rows], axis=0, keepdims=True)
        j = t_in_b * (tm // MOBA_BLOCK) + half
        for hh in range(n_heads):
            kmt_ref[pl.ds(hh * n_blocks + j, 1), :] = jnp.where(lane_head == hh, k_mean, 0.0)

    vt = qvt[d_a:].astype(jnp.bfloat16)
    for hh in range(n_heads):
        vt_ref[hh, 0:HEAD_DIM, :] = vt[hh * HEAD_DIM:(hh + 1) * HEAD_DIM, :]
        vt_ref[hh, HEAD_DIM:V_ROWS, :] = jnp.ones((V_ROWS - HEAD_DIM, tm), jnp.bfloat16)

    qkr = jnp.dot(hb, wqkr_ref[...], preferred_element_type=jnp.float32)
    cos = cos_ref[...]
    sin = sin_ref[...]
    first_half = (lax.broadcasted_iota(jnp.int32, (tm, PAIR), 1) % HEAD_DIM) < HEAD_DIM // 2
    for cidx in range(2 * d_r // PAIR):
        xc = qkr[:, cidx * PAIR:(cidx + 1) * PAIR]
        sw = jnp.where(first_half,
                       pltpu.roll(xc, PAIR - HEAD_DIM // 2, axis=1),
                       pltpu.roll(xc, HEAD_DIM // 2, axis=1))
        rc = xc * cos + sw * sin
        if cidx < d_r // PAIR:
            qr_ref[:, cidx * PAIR:(cidx + 1) * PAIR] = rc.astype(jnp.bfloat16)
        else:
            c2 = cidx - d_r // PAIR
            kr_ref[:, c2 * PAIR:(c2 + 1) * PAIR] = (rc * HEAD_DIM ** -0.5).astype(jnp.bfloat16)

    vg = jnp.dot(hb, wvgr_ref[...], preferred_element_type=jnp.float32)
    vr_ref[...] = vg[:, :d_r].astype(jnp.bfloat16)
    gr_ref[...] = vg[:, d_r:].astype(jnp.bfloat16)


def _const_spec(shape):
    zeros = (0,) * len(shape)
    return pl.BlockSpec(shape, lambda *_: zeros, pipeline_mode=pl.Buffered(1))


def _projection(x2, g, wqvt, wk, wqkr, wvgr, cos_t, sin_t, batch, seq):
    t, d = x2.shape
    d_a = wk.shape[1]
    d_r = wqkr.shape[1] // 2
    n_heads = d_a // HEAD_DIM
    tm = ROW_TILE
    tpb = seq // tm
    n_blocks = seq // MOBA_BLOCK
    row = lambda w: pl.BlockSpec((tm, w), lambda i: (i, 0))
    bf = jnp.bfloat16
    return pl.pallas_call(
        functools.partial(_proj_kernel, tiles_per_batch=tpb, d_a=d_a, d_r=d_r),
        grid=(t // tm,),
        in_specs=[row(d), _const_spec((1, d)), _const_spec(wqvt.shape), _const_spec(wk.shape),
                  _const_spec(wqkr.shape), _const_spec(wvgr.shape),
                  pl.BlockSpec((tm, PAIR), lambda i: (i % tpb, 0)),
                  pl.BlockSpec((tm, PAIR), lambda i: (i % tpb, 0))],
        out_specs=[pl.BlockSpec((None, d_a, tm), lambda i: (i // tpb, 0, i % tpb)), row(d_a),
                   pl.BlockSpec((None, n_heads, V_ROWS, tm), lambda i: (i // tpb, 0, 0, i % tpb)),
                   pl.BlockSpec((None, n_heads * n_blocks, tm), lambda i: (i // tpb, 0, i % tpb)),
                   row(d_r), row(d_r), row(d_r), row(d_r)],
        out_shape=[jax.ShapeDtypeStruct((batch, d_a, seq), bf), jax.ShapeDtypeStruct((t, d_a), bf),
                   jax.ShapeDtypeStruct((batch, n_heads, V_ROWS, seq), bf),
                   jax.ShapeDtypeStruct((batch, n_heads * n_blocks, seq), jnp.float32),
                   jax.ShapeDtypeStruct((t, d_r), bf), jax.ShapeDtypeStruct((t, d_r), bf),
                   jax.ShapeDtypeStruct((t, d_r), bf), jax.ShapeDtypeStruct((t, d_r), bf)],
        scratch_shapes=[pltpu.VMEM((n_heads * n_blocks, d_a), jnp.float32)],
        compiler_params=pltpu.CompilerParams(
            dimension_semantics=("arbitrary",), vmem_limit_bytes=VMEM_LIMIT),
        name="projection",
    )(x2, g, wqvt, wk, wqkr, wvgr, cos_t, sin_t)


def _moba_kernel(rb_ref, qt_ref, k_ref, vt_ref, gate_ref, bias_ref, o_ref, sel_ref, s_ref, mj_ref,
                 p_ref, *, n_blocks):
    hp = pl.program_id(1)
    c = pl.program_id(2)
    qt = qt_ref[...]
    feat = lax.broadcasted_iota(jnp.int32, qt.shape, 0)
    qm = [jnp.where(feat < HEAD_DIM, qt, jnp.zeros_like(qt)),
          jnp.where(feat >= HEAD_DIM, qt, jnp.zeros_like(qt))]

    slot_prev, slot_own = 2, 3
    far_bias = [rb_ref[REL_BUCKETS - 1, 2 * hp + h] * LOG2E for h in range(2)]
    prev = jnp.maximum(c - 1, 0)
    n_far = prev
    last = jnp.maximum(n_far - 1, 0)

    def block_rows(j):
        return pl.ds(pl.multiple_of(j * MOBA_BLOCK, MOBA_BLOCK), MOBA_BLOCK)

    def scores(slot, j, bias_index=None):
        for h in range(2):
            s = jnp.dot(k_ref[block_rows(j), :], qm[h], preferred_element_type=jnp.float32)
            if bias_index is not None:
                s = s + bias_ref[h, bias_index]
            s_ref[slot, h] = s
            mj_ref[slot, h] = jnp.max(s, axis=0, keepdims=True)

    def probs(slot):
        return [jnp.exp2((s_ref[slot, h] - mj_ref[slot, h]).astype(jnp.bfloat16))
                for h in range(2)]

    def weighted(j, p):
        return [jnp.dot(vt_ref[h, :, block_rows(j)], p[h], preferred_element_type=jnp.float32)
                for h in range(2)]

    def merge(states, mj, oj, j, valid):
        out = []
        for h in range(2):
            m, acc = states[h]
            chosen = (sel_ref[h, pl.ds(j, 1), :] > 0.5) & valid
            m_new = jnp.where(chosen, jnp.maximum(m, mj[h]), m)
            beta = jnp.where(chosen, jnp.exp2(mj[h] - m_new), 0.0)
            out.append((m_new, acc * jnp.exp2(m - m_new) + oj[h] * beta))
        return tuple(out)

    scores(slot_own, c, 0)
    scores(slot_prev, prev, 1)
    scores(0, 0)

    blk = lax.broadcasted_iota(jnp.int32, (n_blocks, MOBA_BLOCK), 0)
    for h in range(2):
        g = jnp.where(blk < c, gate_ref[h * n_blocks:(h + 1) * n_blocks, :], -jnp.inf)
        sel = jnp.zeros(g.shape, jnp.float32)
        for _ in range(MOBA_TOPK):
            top = jnp.max(g, axis=0, keepdims=True)
            idx = jnp.min(jnp.where(g == top, blk, n_blocks), axis=0, keepdims=True)
            pick = (blk == idx) & (blk < c)
            sel = jnp.where(pick, 1.0, sel)
            g = jnp.where(pick, -jnp.inf, g)
        sel_ref[h] = sel

    o_own = weighted(c, probs(slot_own))
    o_prev = weighted(prev, probs(slot_prev))
    states = tuple((mj_ref[slot_own, h], o_own[h]) for h in range(2))
    states = merge(states, [mj_ref[slot_prev, h] for h in range(2)], o_prev, prev, True)

    p_ref[...] = jnp.zeros_like(p_ref)
    pend_m = tuple(jnp.zeros((1, MOBA_BLOCK), jnp.float32) for _ in range(2))

    def body(t, carry):
        states, pend_m, pend_j, pend_valid = carry
        j0 = 2 * t
        j1 = jnp.minimum(j0 + 1, last)
        j2 = jnp.minimum(j0 + 2, last)
        o_pend = weighted(pend_j, [p_ref[h] for h in range(2)])
        scores(1, j1)
        o0 = weighted(j0, probs(0))
        m0 = [mj_ref[0, h] + far_bias[h] for h in range(2)]
        states = merge(states, pend_m, o_pend, pend_j, pend_valid > 0)
        scores(0, j2)
        p1 = probs(1)
        for h in range(2):
            p_ref[h] = p1[h]
        m1 = tuple(mj_ref[1, h] + far_bias[h] for h in range(2))
        states = merge(states, m0, o0, j0, True)
        return states, m1, j1, (j0 + 1 < n_far).astype(jnp.int32)

    states, pend_m, pend_j, pend_valid = lax.fori_loop(
        0, (n_far + 1) // 2, body, (states, pend_m, jnp.int32(0), jnp.int32(0)))
    o_pend = weighted(pend_j, [p_ref[h] for h in range(2)])
    states = merge(states, pend_m, o_pend, pend_j, pend_valid > 0)

    outs = []
    for h in range(2):
        _, acc = states[h]
        outs.append((acc[:HEAD_DIM] / acc[HEAD_DIM:HEAD_DIM + 1]).T)
    o_ref[...] = jnp.concatenate(outs, axis=1).astype(o_ref.dtype)


def _moba(rel_bias, qat, ka, vt, gate, bias, batch, seq):
    d_a = ka.shape[1]
    n_blocks = seq // MOBA_BLOCK
    ka3 = ka.reshape(batch, seq, d_a)
    out = pl.pallas_call(
        functools.partial(_moba_kernel, n_blocks=n_blocks),
        grid=(batch, d_a // PAIR, n_blocks),
        in_specs=[pl.BlockSpec(memory_space=pltpu.SMEM),
                  pl.BlockSpec((None, PAIR, MOBA_BLOCK), lambda b, p, c: (b, p, c)),
                  pl.BlockSpec((None, seq, PAIR), lambda b, p, c: (b, 0, p)),
                  pl.BlockSpec((None, 2, V_ROWS, seq), lambda b, p, c: (b, p, 0, 0)),
                  pl.BlockSpec((None, 2 * n_blocks, MOBA_BLOCK), lambda b, p, c: (b, p, c)),
                  pl.BlockSpec((2, 2, MOBA_BLOCK, MOBA_BLOCK), lambda b, p, c: (p, 0, 0, 0))],
        out_specs=pl.BlockSpec((None, MOBA_BLOCK, PAIR), lambda b, p, c: (b, c, p)),
        out_shape=jax.ShapeDtypeStruct((batch, seq, d_a), jnp.bfloat16),
        scratch_shapes=[pltpu.VMEM((2, n_blocks, MOBA_BLOCK), jnp.float32),
                        pltpu.VMEM((4, 2, MOBA_BLOCK, MOBA_BLOCK), jnp.float32),
                        pltpu.VMEM((4, 2, 1, MOBA_BLOCK), jnp.float32),
                        pltpu.VMEM((2, MOBA_BLOCK, MOBA_BLOCK), jnp.bfloat16)],
        compiler_params=pltpu.CompilerParams(
            dimension_semantics=("arbitrary", "arbitrary", "arbitrary"),
            vmem_limit_bytes=VMEM_LIMIT),
        name="moba",
    )(rel_bias, qat, ka3, vt, gate, bias)
    return out.reshape(batch * seq, d_a)


def _ret_kernel(q_ref, k_ref, v_ref, g_ref, dmask_ref, dstart_ref, dend_ref, cdec_ref,
                o_ref, state_ref):
    @pl.when(pl.program_id(2) == 0)
    def _():
        state_ref[...] = jnp.zeros_like(state_ref)

    q = q_ref[...]
    k = k_ref[...]
    v = v_ref[...]
    lane = lax.broadcasted_iota(jnp.int32, q.shape, 1)
    in_head = [lane < HEAD_DIM, lane >= HEAD_DIM]

    inner = jnp.zeros(q.shape, jnp.float32)
    for h in range(2):
        qm = jnp.where(in_head[h], q, jnp.zeros_like(q))
        sc = lax.dot_general(qm, k, NT, preferred_element_type=jnp.float32) * dmask_ref[h]
        yh = jnp.dot(sc.astype(jnp.bfloat16), v, preferred_element_type=jnp.float32)
        inner = jnp.where(in_head[h], yh, inner)

    state = state_ref[...]
    cross = jnp.dot(q, state.astype(jnp.bfloat16), preferred_element_type=jnp.float32)
    y = inner + cross * dstart_ref[...]

    kd = (k.astype(jnp.float32) * dend_ref[...]).astype(jnp.bfloat16)
    kv = lax.dot_general(kd, v, TN, preferred_element_type=jnp.float32)
    r = lax.broadcasted_iota(jnp.int32, state.shape, 0) // HEAD_DIM
    cc = lax.broadcasted_iota(jnp.int32, state.shape, 1) // HEAD_DIM
    same_head = r == cc
    state_ref[...] = jnp.where(same_head, state * cdec_ref[...] + kv, 0.0)

    avg = jnp.where(same_head, 1.0 / HEAD_DIM, 0.0)
    mu = jnp.dot(y, avg, precision=lax.Precision.HIGHEST, preferred_element_type=jnp.float32)
    d = y - mu
    var = jnp.dot(d * d, avg, precision=lax.Precision.HIGHEST, preferred_element_type=jnp.float32)
    yn = d * lax.rsqrt(var + EPS)
    g = g_ref[...].astype(jnp.float32)
    o_ref[...] = (g * jax.nn.sigmoid(g) * yn).astype(o_ref.dtype)


def _retention(qr, kr, vr, gr, dmask, dstart, dend, cdec, batch, seq):
    d_r = qr.shape[1]
    n_chunks = seq // RET_CHUNK
    blk = pl.BlockSpec((None, RET_CHUNK, PAIR), lambda b, p, c: (b, c, p))
    r3 = lambda a: a.reshape(batch, seq, d_r)
    out = pl.pallas_call(
        _ret_kernel,
        grid=(batch, d_r // PAIR, n_chunks),
        in_specs=[blk, blk, blk, blk,
                  pl.BlockSpec((2, RET_CHUNK, RET_CHUNK), lambda b, p, c: (p, 0, 0)),
                  pl.BlockSpec((RET_CHUNK, PAIR), lambda b, p, c: (0, p)),
                  pl.BlockSpec((RET_CHUNK, PAIR), lambda b, p, c: (0, p)),
                  pl.BlockSpec((1, PAIR), lambda b, p, c: (0, p))],
        out_specs=blk,
        out_shape=jax.ShapeDtypeStruct((batch, seq, d_r), jnp.bfloat16),
        scratch_shapes=[pltpu.VMEM((PAIR, PAIR), jnp.float32)],
        compiler_params=pltpu.CompilerParams(
            dimension_semantics=("arbitrary", "arbitrary", "arbitrary")),
        name="retention",
    )(r3(qr), r3(kr), r3(vr), r3(gr), dmask, dstart, dend, cdec)
    return out.reshape(batch * seq, d_r)


def _ffn_kernel(ya_ref, yr_ref, x_ref, woa_ref, wor_ref, g1_ref, g2_ref, wup_ref, cw_ref, cb_ref,
                wdn_ref, g3_ref, o_ref, tail_ref, *, tiles_per_batch, d_ff):
    i = pl.program_id(0)
    tm = x_ref.shape[0]

    @pl.when(i == 0)
    def _():
        tail_ref[...] = jnp.zeros_like(tail_ref)

    y = (jnp.dot(ya_ref[...], woa_ref[...], preferred_element_type=jnp.float32)
         + jnp.dot(yr_ref[...], wor_ref[...], preferred_element_type=jnp.float32))
    x1 = x_ref[...] + _rms(y, g1_ref[...])
    h2 = _rms(x1, g2_ref[...]).astype(jnp.bfloat16)

    keep_tail = (i % tiles_per_batch) != 0
    row = lax.broadcasted_iota(jnp.int32, (tm, FF_CHUNK), 0)

    def conv(cols):
        u = jnp.dot(h2, wup_ref[:, cols], preferred_element_type=jnp.float32)
        tail = jnp.where(keep_tail, tail_ref[:, cols], 0.0)
        tail_ref[:, cols] = u[tm - 8:, :]
        t1 = tail[7:8, :]
        t2 = tail[6:7, :]
        u1 = jnp.where(row == 0, t1, pltpu.roll(u, 1, axis=0))
        u2 = jnp.where(row == 0, t2, jnp.where(row == 1, t1, pltpu.roll(u, 2, axis=0)))
        w = cw_ref[:, cols]
        return cb_ref[:, cols] + w[2:3] * u + w[1:2] * u1 + w[0:1] * u2

    acc = jnp.zeros((tm, o_ref.shape[1]), jnp.float32)
    for ci in range(d_ff // FF_CHUNK):
        a = conv(slice(ci * FF_CHUNK, (ci + 1) * FF_CHUNK))
        b = conv(slice(d_ff + ci * FF_CHUNK, d_ff + (ci + 1) * FF_CHUNK))
        f = (jax.nn.gelu(a, approximate=True) * b).astype(jnp.bfloat16)
        acc = acc + jnp.dot(f, wdn_ref[ci * FF_CHUNK:(ci + 1) * FF_CHUNK, :],
                            preferred_element_type=jnp.float32)
    o_ref[...] = x1 + _rms(acc, g3_ref[...])


def _out_ffn(ya, yr, x2, woa, wor, g1, g2, wup, cw, cb, wdn, g3, seq):
    t, d = x2.shape
    d_ff = wdn.shape[0]
    tm = ROW_TILE
    row = lambda w: pl.BlockSpec((tm, w), lambda i: (i, 0))
    return pl.pallas_call(
        functools.partial(_ffn_kernel, tiles_per_batch=seq // tm, d_ff=d_ff),
        grid=(t // tm,),
        in_specs=[row(ya.shape[1]), row(yr.shape[1]), row(d),
                  _const_spec(woa.shape), _const_spec(wor.shape),
                  _const_spec((1, d)), _const_spec((1, d)), _const_spec(wup.shape),
                  _const_spec(cw.shape), _const_spec(cb.shape), _const_spec(wdn.shape),
                  _const_spec((1, d))],
        out_specs=row(d),
        out_shape=jax.ShapeDtypeStruct((t, d), jnp.float32),
        scratch_shapes=[pltpu.VMEM((8, 2 * d_ff), jnp.float32)],
        compiler_params=pltpu.CompilerParams(
            dimension_semantics=("arbitrary",), vmem_limit_bytes=VMEM_LIMIT),
        name="out_ffn",
    )(ya, yr, x2, woa, wor, g1, g2, wup, cw, cb, wdn, g3)


def _rotary_tables(seq):
    inv = ROPE_BASE ** (-jnp.arange(0, HEAD_DIM, 2, dtype=jnp.float32) / HEAD_DIM)
    ang = jnp.arange(seq, dtype=jnp.float32)[:, None] * inv[None, :]
    cos, sin = jnp.cos(ang), jnp.sin(ang)
    cos_t = jnp.tile(jnp.concatenate([cos, cos], axis=1), (1, PAIR // HEAD_DIM))
    sin_t = jnp.tile(jnp.concatenate([-sin, sin], axis=1), (1, PAIR // HEAD_DIM))
    return cos_t, sin_t


def _decay_tables(n_heads):
    c = RET_CHUNK
    log_gamma = jnp.log(1.0 - 2.0 ** (-5.0 - jnp.arange(n_heads, dtype=jnp.float32)))
    n = jnp.arange(c, dtype=jnp.float32)
    rel = n[:, None] - n[None, :]
    dmask = jnp.where(rel >= 0, jnp.exp(jnp.maximum(rel, 0.0)[None] * log_gamma[:, None, None]), 0.0)
    dend = jnp.exp((c - 1.0 - n)[:, None] * log_gamma[None, :])
    dstart = jnp.exp((n + 1.0)[:, None] * log_gamma[None, :])
    cdec = jnp.exp(c * log_gamma)[None, :]
    wide = lambda a: jnp.repeat(a, HEAD_DIM, axis=1)
    return dmask, wide(dstart), wide(dend), wide(cdec)


def kernel(x, norm_mix_pre, w_in, rel_bias, w_out, norm_mix_post, norm_ffn_pre, w_up, conv_w,
           conv_b, w_down, norm_ffn_post):
    batch, seq, d = x.shape
    depth = w_in.shape[0]
    d_a = w_out.shape[1] // 2
    d_r = w_out.shape[1] - d_a
    assert seq % ROW_TILE == 0 and ROW_TILE % MOBA_BLOCK == 0 and MOBA_BLOCK == RET_CHUNK
    assert w_in.shape[2] == 3 * d_a + 4 * d_r and d_a % PAIR == 0 and d_r % PAIR == 0
    assert w_down.shape[1] % FF_CHUNK == 0

    bf = jnp.bfloat16
    cos_t, sin_t = _rotary_tables(seq)
    dmask, dstart, dend, cdec = _decay_tables(d_r // HEAD_DIM)
    bias = _bias_tables(rel_bias)

    x2 = x.reshape(batch * seq, d)
    for l in range(depth):
        w = w_in[l].astype(bf)
        wqvt = jnp.concatenate([w[:, :d_a], w[:, 2 * d_a:3 * d_a]], axis=1).T
        wk = w[:, d_a:2 * d_a]
        wqkr = w[:, 3 * d_a:3 * d_a + 2 * d_r]
        wvgr = w[:, 3 * d_a + 2 * d_r:]
        qat, ka, vt, gate, qr, kr, vr, gr = _projection(
            x2, norm_mix_pre[l][None], wqvt, wk, wqkr, wvgr, cos_t, sin_t, batch, seq)
        ya = _moba(rel_bias, qat, ka, vt, gate, bias, batch, seq)
        yr = _retention(qr, kr, vr, gr, dmask, dstart, dend, cdec, batch, seq)
        wo = w_out[l].astype(bf)
        x2 = _out_ffn(ya, yr, x2, wo[:d_a], wo[d_a:], norm_mix_post[l][None],
                      norm_ffn_pre[l][None], w_up[l].astype(bf), conv_w[l], conv_b[l][None],
                      w_down[l].astype(bf), norm_ffn_post[l][None], seq)
    return x2.reshape(batch, seq, d)
```

```python
import functools
import math

import jax
import jax.numpy as jnp
from jax import lax
from jax.experimental import pallas as pl
from jax.experimental.pallas import tpu as pltpu

HEAD_DIM = 64
PAIR = 2 * HEAD_DIM
MOBA_BLOCK = 256
MOBA_TOPK = 3
RET_CHUNK = 256
REL_BUCKETS = 32
REL_MAX_DIST = 128
ROPE_BASE = 10000.0
CONV_WIDTH = 3
EPS = 1e-6
NEG = -1e30
LOG2E = math.log2(math.e)
Q_SCALE = HEAD_DIM ** -0.5 * LOG2E
V_ROWS = HEAD_DIM + 16
ROW_TILE = 512
FF_CHUNK = 256
HALO = 8
VMEM_LIMIT = 56 * 1024 * 1024

NT = (((1,), (1,)), ((), ()))
TN = (((0,), (0,)), ((), ()))


def _rms(x, g):
    return x * lax.rsqrt(jnp.mean(x * x, axis=-1, keepdims=True) + EPS) * g


def _bias_kernel(rb_ref, o_ref):
    h = pl.program_id(0)
    key = lax.broadcasted_iota(jnp.int32, (MOBA_BLOCK, MOBA_BLOCK), 0)
    qry = lax.broadcasted_iota(jnp.int32, (MOBA_BLOCK, MOBA_BLOCK), 1)
    max_exact = REL_BUCKETS // 2
    for which in range(2):
        rel = qry - key + which * MOBA_BLOCK
        n = jnp.maximum(rel, 0)
        n_f = jnp.maximum(n, 1).astype(jnp.float32)
        large = max_exact + (jnp.log(n_f / max_exact) / math.log(REL_MAX_DIST / max_exact)
                             * (REL_BUCKETS - max_exact)).astype(jnp.int32)
        large = jnp.minimum(large, REL_BUCKETS - 1)
        bucket = jnp.where(n < max_exact, n, large)
        val = jnp.zeros((MOBA_BLOCK, MOBA_BLOCK), jnp.float32)
        for b in range(REL_BUCKETS):
            val = jnp.where(bucket == b, rb_ref[b, h], val)
        o_ref[0, which] = jnp.where(rel >= 0, val * LOG2E, NEG)


def _bias_tables(rel_bias):
    n_heads = rel_bias.shape[1]
    return pl.pallas_call(
        _bias_kernel,
        grid=(n_heads,),
        in_specs=[pl.BlockSpec(memory_space=pltpu.SMEM)],
        out_specs=pl.BlockSpec((1, 2, MOBA_BLOCK, MOBA_BLOCK), lambda h: (h, 0, 0, 0)),
        out_shape=jax.ShapeDtypeStruct((n_heads, 2, MOBA_BLOCK, MOBA_BLOCK), jnp.float32),
        name="bias_tables",
    )(rel_bias)


def _proj_kernel(x_ref, g_ref, wqvt_ref, wk_ref, wqkr_ref, wvgr_ref, cos_ref, sin_ref,
                 qat_ref, ka_ref, vt_ref, gate_ref, qr_ref, kr_ref, vr_ref, gr_ref,
                 kmt_ref, *, tiles_per_batch, d_a, d_r):
    t_in_b = pl.program_id(0) % tiles_per_batch
    n_heads = d_a // HEAD_DIM
    tm = x_ref.shape[0]
    n_blocks = tiles_per_batch * (tm // MOBA_BLOCK)

    @pl.when(t_in_b == 0)
    def _():
        kmt_ref[...] = jnp.zeros_like(kmt_ref)

    hb = _rms(x_ref[...], g_ref[...]).astype(jnp.bfloat16)

    qvt = lax.dot_general(wqvt_ref[...], hb, NT, preferred_element_type=jnp.float32)
    qt = qvt[:d_a]
    qat_ref[...] = (qt * Q_SCALE).astype(jnp.bfloat16)
    k = jnp.dot(hb, wk_ref[...], preferred_element_type=jnp.float32)
    ka_ref[...] = k.astype(jnp.bfloat16)

    lane_head = lax.broadcasted_iota(jnp.int32, (1, d_a), 1) // HEAD_DIM
    for half in range(tm // MOBA_BLOCK):
        rows = slice(half * MOBA_BLOCK, (half + 1) * MOBA_BLOCK)
        gate_ref[:, rows] = jnp.dot(
            kmt_ref[...].astype(jnp.bfloat16), qt[:, rows].astype(jnp.bfloat16),
            preferred_element_type=jnp.float32)
        k_mean = jnp.mean(k[rows], axis=0, keepdims=True)
        j = t_in_b * (tm // MOBA_BLOCK) + half
        for hh in range(n_heads):
            kmt_ref[pl.ds(hh * n_blocks + j, 1), :] = jnp.where(lane_head == hh, k_mean, 0.0)

    vt = qvt[d_a:].astype(jnp.bfloat16)
    for hh in range(n_heads):
        vt_ref[hh, 0:HEAD_DIM, :] = vt[hh * HEAD_DIM:(hh + 1) * HEAD_DIM, :]
        vt_ref[hh, HEAD_DIM:V_ROWS, :] = jnp.ones((V_ROWS - HEAD_DIM, tm), jnp.bfloat16)

    qkr = jnp.dot(hb, wqkr_ref[...], preferred_element_type=jnp.float32)
    cos = cos_ref[...]
    sin = sin_ref[...]
    first_half = (lax.broadcasted_iota(jnp.int32, (tm, PAIR), 1) % HEAD_DIM) < HEAD_DIM // 2
    for cidx in range(2 * d_r // PAIR):
        xc = qkr[:, cidx * PAIR:(cidx + 1) * PAIR]
        sw = jnp.where(first_half,
                       pltpu.roll(xc, PAIR - HEAD_DIM // 2, axis=1),
                       pltpu.roll(xc, HEAD_DIM // 2, axis=1))
        rc = xc * cos + sw * sin
        if cidx < d_r // PAIR:
            qr_ref[:, cidx * PAIR:(cidx + 1) * PAIR] = rc.astype(jnp.bfloat16)
        else:
            c2 = cidx - d_r // PAIR
            kr_ref[:, c2 * PAIR:(c2 + 1) * PAIR] = (rc * HEAD_DIM ** -0.5).astype(jnp.bfloat16)

    vg = jnp.dot(hb, wvgr_ref[...], preferred_element_type=jnp.float32)
    vr_ref[...] = vg[:, :d_r].astype(jnp.bfloat16)
    gr_ref[...] = vg[:, d_r:].astype(jnp.bfloat16)


def _const_spec(shape):
    zeros = (0,) * len(shape)
    return pl.BlockSpec(shape, lambda *_: zeros, pipeline_mode=pl.Buffered(1))


def _projection(x2, g, wqvt, wk, wqkr, wvgr, cos_t, sin_t, batch, seq):
    t, d = x2.shape
    d_a = wk.shape[1]
    d_r = wqkr.shape[1] // 2
    n_heads = d_a // HEAD_DIM
    tm = ROW_TILE
    tpb = seq // tm
    n_blocks = seq // MOBA_BLOCK
    row = lambda w: pl.BlockSpec((tm, w), lambda i: (i, 0))
    bf = jnp.bfloat16
    return pl.pallas_call(
        functools.partial(_proj_kernel, tiles_per_batch=tpb, d_a=d_a, d_r=d_r),
        grid=(t // tm,),
        in_specs=[row(d), _const_spec((1, d)), _const_spec(wqvt.shape), _const_spec(wk.shape),
                  _const_spec(wqkr.shape), _const_spec(wvgr.shape),
                  pl.BlockSpec((tm, PAIR), lambda i: (i % tpb, 0)),
                  pl.BlockSpec((tm, PAIR), lambda i: (i % tpb, 0))],
        out_specs=[pl.BlockSpec((None, d_a, tm), lambda i: (i // tpb, 0, i % tpb)), row(d_a),
                   pl.BlockSpec((None, n_heads, V_ROWS, tm), lambda i: (i // tpb, 0, 0, i % tpb)),
                   pl.BlockSpec((None, n_heads * n_blocks, tm), lambda i: (i // tpb, 0, i % tpb)),
                   row(d_r), row(d_r), row(d_r), row(d_r)],
        out_shape=[jax.ShapeDtypeStruct((batch, d_a, seq), bf), jax.ShapeDtypeStruct((t, d_a), bf),
                   jax.ShapeDtypeStruct((batch, n_heads, V_ROWS, seq), bf),
                   jax.ShapeDtypeStruct((batch, n_heads * n_blocks, seq), jnp.float32),
                   jax.ShapeDtypeStruct((t, d_r), bf), jax.ShapeDtypeStruct((t, d_r), bf),
                   jax.ShapeDtypeStruct((t, d_r), bf), jax.ShapeDtypeStruct((t, d_r), bf)],
        scratch_shapes=[pltpu.VMEM((n_heads * n_blocks, d_a), jnp.float32)],
        compiler_params=pltpu.CompilerParams(
            dimension_semantics=("arbitrary",), vmem_limit_bytes=VMEM_LIMIT),
        name="projection",
    )(x2, g, wqvt, wk, wqkr, wvgr, cos_t, sin_t)


def _moba_kernel(rb_ref, qt_ref, k_ref, vt_ref, gate_ref, bias_ref, o_ref, sel_ref, s_ref, mj_ref,
                 p_ref, *, n_blocks):
    hp = pl.program_id(1)
    c = pl.program_id(2)
    qt = qt_ref[...]
    feat = lax.broadcasted_iota(jnp.int32, qt.shape, 0)
    qm = [jnp.where(feat < HEAD_DIM, qt, jnp.zeros_like(qt)),
          jnp.where(feat >= HEAD_DIM, qt, jnp.zeros_like(qt))]

    slot_prev, slot_own = 2, 3
    far_bias = [rb_ref[REL_BUCKETS - 1, 2 * hp + h] * LOG2E for h in range(2)]
    prev = jnp.maximum(c - 1, 0)
    n_far = prev
    last = jnp.maximum(n_far - 1, 0)

    def block_rows(j):
        return pl.ds(pl.multiple_of(j * MOBA_BLOCK, MOBA_BLOCK), MOBA_BLOCK)

    def scores(slot, j, bias_index=None):
        for h in range(2):
            s = jnp.dot(k_ref[block_rows(j), :], qm[h], preferred_element_type=jnp.float32)
            if bias_index is not None:
                s = s + bias_ref[h, bias_index]
            s_ref[slot, h] = s
            mj_ref[slot, h] = jnp.max(s, axis=0, keepdims=True)

    def probs(slot):
        return [jnp.exp2((s_ref[slot, h] - mj_ref[slot, h]).astype(jnp.bfloat16))
                for h in range(2)]

    def weighted(j, p):
        return [jnp.dot(vt_ref[h, :, block_rows(j)], p[h], preferred_element_type=jnp.float32)
                for h in range(2)]

    def merge(states, mj, oj, j, valid):
        out = []
        for h in range(2):
            m, acc = states[h]
            chosen = (sel_ref[h, pl.ds(j, 1), :] > 0.5) & valid
            m_new = jnp.where(chosen, jnp.maximum(m, mj[h]), m)
            beta = jnp.where(chosen, jnp.exp2(mj[h] - m_new), 0.0)
            out.append((m_new, acc * jnp.exp2(m - m_new) + oj[h] * beta))
        return tuple(out)

    scores(slot_own, c, 0)
    scores(slot_prev, prev, 1)
    scores(0, 0)

    blk = lax.broadcasted_iota(jnp.int32, (n_blocks, MOBA_BLOCK), 0)
    for h in range(2):
        g = jnp.where(blk < c, gate_ref[h * n_blocks:(h + 1) * n_blocks, :], -jnp.inf)
        sel = jnp.zeros(g.shape, jnp.float32)
        for _ in range(MOBA_TOPK):
            top = jnp.max(g, axis=0, keepdims=True)
            idx = jnp.min(jnp.where(g == top, blk, n_blocks), axis=0, keepdims=True)
            pick = (blk == idx) & (blk < c)
            sel = jnp.where(pick, 1.0, sel)
            g = jnp.where(pick, -jnp.inf, g)
        sel_ref[h] = sel

    o_own = weighted(c, probs(slot_own))
    o_prev = weighted(prev, probs(slot_prev))
    states = tuple((mj_ref[slot_own, h], o_own[h]) for h in range(2))
    states = merge(states, [mj_ref[slot_prev, h] for h in range(2)], o_prev, prev, True)

    p_ref[...] = jnp.zeros_like(p_ref)
    pend_m = tuple(jnp.zeros((1, MOBA_BLOCK), jnp.float32) for _ in range(2))

    def body(t, carry):
        states, pend_m, pend_j, pend_valid = carry
        j0 = 2 * t
        j1 = jnp.minimum(j0 + 1, last)
        j2 = jnp.minimum(j0 + 2, last)
        o_pend = weighted(pend_j, [p_ref[h] for h in range(2)])
        scores(1, j1)
        o0 = weighted(j0, probs(0))
        m0 = [mj_ref[0, h] + far_bias[h] for h in range(2)]
        states = merge(states, pend_m, o_pend, pend_j, pend_valid > 0)
        scores(0, j2)
        p1 = probs(1)
        for h in range(2):
            p_ref[h] = p1[h]
        m1 = tuple(mj_ref[1, h] + far_bias[h] for h in range(2))
        states = merge(states, m0, o0, j0, True)
        return states, m1, j1, (j0 + 1 < n_far).astype(jnp.int32)

    states, pend_m, pend_j, pend_valid = lax.fori_loop(
        0, (n_far + 1) // 2, body, (states, pend_m, jnp.int32(0), jnp.int32(0)))
    o_pend = weighted(pend_j, [p_ref[h] for h in range(2)])
    states = merge(states, pend_m, o_pend, pend_j, pend_valid > 0)

    outs = []
    for h in range(2):
        _, acc = states[h]
        outs.append((acc[:HEAD_DIM] / acc[HEAD_DIM:HEAD_DIM + 1]).T)
    o_ref[...] = jnp.concatenate(outs, axis=1).astype(o_ref.dtype)


def _moba(rel_bias, qat, ka, vt, gate, bias, batch, seq):
    d_a = ka.shape[1]
    n_blocks = seq // MOBA_BLOCK
    ka3 = ka.reshape(batch, seq, d_a)
    out = pl.pallas_call(
        functools.partial(_moba_kernel, n_blocks=n_blocks),
        grid=(batch, d_a // PAIR, n_blocks),
        in_specs=[pl.BlockSpec(memory_space=pltpu.SMEM),
                  pl.BlockSpec((None, PAIR, MOBA_BLOCK), lambda b, p, c: (b, p, c)),
                  pl.BlockSpec((None, seq, PAIR), lambda b, p, c: (b, 0, p)),
                  pl.BlockSpec((None, 2, V_ROWS, seq), lambda b, p, c: (b, p, 0, 0)),
                  pl.BlockSpec((None, 2 * n_blocks, MOBA_BLOCK), lambda b, p, c: (b, p, c)),
                  pl.BlockSpec((2, 2, MOBA_BLOCK, MOBA_BLOCK), lambda b, p, c: (p, 0, 0, 0))],
        out_specs=pl.BlockSpec((None, MOBA_BLOCK, PAIR), lambda b, p, c: (b, c, p)),
        out_shape=jax.ShapeDtypeStruct((batch, seq, d_a), jnp.bfloat16),
        scratch_shapes=[pltpu.VMEM((2, n_blocks, MOBA_BLOCK), jnp.float32),
                        pltpu.VMEM((4, 2, MOBA_BLOCK, MOBA_BLOCK), jnp.float32),
                        pltpu.VMEM((4, 2, 1, MOBA_BLOCK), jnp.float32),
                        pltpu.VMEM((2, MOBA_BLOCK, MOBA_BLOCK), jnp.bfloat16)],
        compiler_params=pltpu.CompilerParams(
            dimension_semantics=("arbitrary", "arbitrary", "arbitrary"),
            vmem_limit_bytes=VMEM_LIMIT),
        name="moba",
    )(rel_bias, qat, ka3, vt, gate, bias)
    return out.reshape(batch * seq, d_a)


def _ret_kernel(q_ref, k_ref, v_ref, g_ref, dmask_ref, dstart_ref, dend_ref, cdec_ref,
                o_ref, state_ref):
    @pl.when(pl.program_id(1) == 0)
    def _():
        state_ref[...] = jnp.zeros_like(state_ref)

    n_pairs = state_ref.shape[0]
    bf = jnp.bfloat16
    f32 = jnp.float32
    cols = [slice(p * PAIR, (p + 1) * PAIR) for p in range(n_pairs)]
    lane = lax.broadcasted_iota(jnp.int32, (RET_CHUNK, PAIR), 1)
    in_head = [lane < HEAD_DIM, lane >= HEAD_DIM]
    r = lax.broadcasted_iota(jnp.int32, (PAIR, PAIR), 0) // HEAD_DIM
    cc = lax.broadcasted_iota(jnp.int32, (PAIR, PAIR), 1) // HEAD_DIM
    same_head = r == cc
    avg = jnp.where(same_head, 1.0 / HEAD_DIM, 0.0).astype(bf)

    q = [q_ref[:, c] for c in cols]
    k = [k_ref[:, c] for c in cols]
    v = [v_ref[:, c] for c in cols]

    sc = [[lax.dot_general(jnp.where(in_head[h], q[p], jnp.zeros_like(q[p])), k[p], NT,
                           preferred_element_type=f32) for h in range(2)]
          for p in range(n_pairs)]
    state = [state_ref[p] for p in range(n_pairs)]
    cross = [jnp.dot(q[p], state[p].astype(bf), preferred_element_type=f32)
             for p in range(n_pairs)]
    kv = [lax.dot_general((k[p].astype(f32) * dend_ref[:, cols[p]]).astype(bf), v[p], TN,
                          preferred_element_type=f32) for p in range(n_pairs)]
    for p in range(n_pairs):
        state_ref[p] = jnp.where(same_head, state[p] * cdec_ref[:, cols[p]] + kv[p], 0.0)

    y = []
    for p in range(n_pairs):
        yh = [jnp.dot((sc[p][h] * dmask_ref[2 * p + h]).astype(bf), v[p],
                      preferred_element_type=f32) for h in range(2)]
        y.append(jnp.where(in_head[0], yh[0], yh[1]) + cross[p] * dstart_ref[:, cols[p]])

    def head_mean(t):
        return jnp.dot(t.astype(bf), avg, preferred_element_type=f32)

    y_hi = [t.astype(bf) for t in y]
    mu = [jnp.dot(y_hi[p], avg, preferred_element_type=f32)
          + head_mean(y[p] - y_hi[p].astype(f32)) for p in range(n_pairs)]
    d = [y[p] - mu[p] for p in range(n_pairs)]
    var = [head_mean(d[p] * d[p]) for p in range(n_pairs)]
    for p in range(n_pairs):
        g = g_ref[:, cols[p]].astype(f32)
        o_ref[:, cols[p]] = (g * jax.nn.sigmoid(g) * d[p] * lax.rsqrt(var[p] + EPS)).astype(bf)


def _retention(qr, kr, vr, gr, dmask, dstart, dend, cdec, batch, seq):
    d_r = qr.shape[1]
    n_chunks = seq // RET_CHUNK
    blk = pl.BlockSpec((None, RET_CHUNK, d_r), lambda b, c: (b, c, 0))
    r3 = lambda a: a.reshape(batch, seq, d_r)
    out = pl.pallas_call(
        _ret_kernel,
        grid=(batch, n_chunks),
        in_specs=[blk, blk, blk, blk, _const_spec(dmask.shape), _const_spec(dstart.shape),
                  _const_spec(dend.shape), _const_spec(cdec.shape)],
        out_specs=blk,
        out_shape=jax.ShapeDtypeStruct((batch, seq, d_r), jnp.bfloat16),
        scratch_shapes=[pltpu.VMEM((d_r // PAIR, PAIR, PAIR), jnp.float32)],
        compiler_params=pltpu.CompilerParams(
            dimension_semantics=("arbitrary", "arbitrary"), vmem_limit_bytes=VMEM_LIMIT),
        name="retention",
    )(r3(qr), r3(kr), r3(vr), r3(gr), dmask, dstart, dend, cdec)
    return out.reshape(batch * seq, d_r)


def _ffn_kernel(ya_ref, yr_ref, x_ref, woa_ref, wor_ref, g1_ref, g2_ref, wup_ref, cw_ref, cb_ref,
                wdn_ref, g3_ref, o_ref, u_ref, *, tiles_per_batch, d_ff):
    tm = x_ref.shape[0]

    @pl.when(pl.program_id(0) % tiles_per_batch == 0)
    def _():
        u_ref[0:HALO, :] = jnp.zeros((HALO, u_ref.shape[1]), jnp.float32)

    y = (jnp.dot(ya_ref[...], woa_ref[...], preferred_element_type=jnp.float32)
         + jnp.dot(yr_ref[...], wor_ref[...], preferred_element_type=jnp.float32))
    x1 = x_ref[...] + _rms(y, g1_ref[...])
    h2 = _rms(x1, g2_ref[...]).astype(jnp.bfloat16)

    def chunk_cols(ci):
        lo = ci * FF_CHUNK
        return slice(lo, lo + FF_CHUNK), slice(d_ff + lo, d_ff + lo + FF_CHUNK)

    def up(ci):
        for cols in chunk_cols(ci):
            u_ref[HALO:HALO + tm, cols] = jnp.dot(h2, wup_ref[:, cols],
                                                  preferred_element_type=jnp.float32)

    def conv(cols):
        w = cw_ref[:, cols]
        return (cb_ref[:, cols]
                + w[2:3] * u_ref[HALO:HALO + tm, cols]
                + w[1:2] * u_ref[HALO - 1:HALO - 1 + tm, cols]
                + w[0:1] * u_ref[HALO - 2:HALO - 2 + tm, cols])

    n_chunks = d_ff // FF_CHUNK
    acc = jnp.zeros((tm, o_ref.shape[1]), jnp.float32)
    up(0)
    for ci in range(n_chunks):
        if ci + 1 < n_chunks:
            up(ci + 1)
        cols_a, cols_b = chunk_cols(ci)
        f = (jax.nn.gelu(conv(cols_a), approximate=True) * conv(cols_b)).astype(jnp.bfloat16)
        acc = acc + jnp.dot(f, wdn_ref[ci * FF_CHUNK:(ci + 1) * FF_CHUNK, :],
                            preferred_element_type=jnp.float32)
    u_ref[0:HALO, :] = u_ref[tm:tm + HALO, :]
    o_ref[...] = x1 + _rms(acc, g3_ref[...])


def _out_ffn(ya, yr, x2, woa, wor, g1, g2, wup, cw, cb, wdn, g3, seq):
    t, d = x2.shape
    d_ff = wdn.shape[0]
    tm = ROW_TILE
    row = lambda w: pl.BlockSpec((tm, w), lambda i: (i, 0))
    return pl.pallas_call(
        functools.partial(_ffn_kernel, tiles_per_batch=seq // tm, d_ff=d_ff),
        grid=(t // tm,),
        in_specs=[row(ya.shape[1]), row(yr.shape[1]), row(d),
                  _const_spec(woa.shape), _const_spec(wor.shape),
                  _const_spec((1, d)), _const_spec((1, d)), _const_spec(wup.shape),
                  _const_spec(cw.shape), _const_spec(cb.shape), _const_spec(wdn.shape),
                  _const_spec((1, d))],
        out_specs=row(d),
        out_shape=jax.ShapeDtypeStruct((t, d), jnp.float32),
        scratch_shapes=[pltpu.VMEM((HALO + tm, 2 * d_ff), jnp.float32)],
        compiler_params=pltpu.CompilerParams(
            dimension_semantics=("arbitrary",), vmem_limit_bytes=VMEM_LIMIT),
        name="out_ffn",
    )(ya, yr, x2, woa, wor, g1, g2, wup, cw, cb, wdn, g3)


def _rotary_tables(seq):
    inv = ROPE_BASE ** (-jnp.arange(0, HEAD_DIM, 2, dtype=jnp.float32) / HEAD_DIM)
    ang = jnp.arange(seq, dtype=jnp.float32)[:, None] * inv[None, :]
    cos, sin = jnp.cos(ang), jnp.sin(ang)
    cos_t = jnp.tile(jnp.concatenate([cos, cos], axis=1), (1, PAIR // HEAD_DIM))
    sin_t = jnp.tile(jnp.concatenate([-sin, sin], axis=1), (1, PAIR // HEAD_DIM))
    return cos_t, sin_t


def _decay_tables(n_heads):
    c = RET_CHUNK
    log_gamma = jnp.log(1.0 - 2.0 ** (-5.0 - jnp.arange(n_heads, dtype=jnp.float32)))
    n = jnp.arange(c, dtype=jnp.float32)
    rel = n[:, None] - n[None, :]
    dmask = jnp.where(rel >= 0, jnp.exp(jnp.maximum(rel, 0.0)[None] * log_gamma[:, None, None]), 0.0)
    dend = jnp.exp((c - 1.0 - n)[:, None] * log_gamma[None, :])
    dstart = jnp.exp((n + 1.0)[:, None] * log_gamma[None, :])
    cdec = jnp.exp(c * log_gamma)[None, :]
    wide = lambda a: jnp.repeat(a, HEAD_DIM, axis=1)
    return dmask, wide(dstart), wide(dend), wide(cdec)


def kernel(x, norm_mix_pre, w_in, rel_bias, w_out, norm_mix_post, norm_ffn_pre, w_up, conv_w,
           conv_b, w_down, norm_ffn_post):
    batch, seq, d = x.shape
    depth = w_in.shape[0]
    d_a = w_out.shape[1] // 2
    d_r = w_out.shape[1] - d_a
    assert seq % ROW_TILE == 0 and ROW_TILE % MOBA_BLOCK == 0 and MOBA_BLOCK == RET_CHUNK
    assert w_in.shape[2] == 3 * d_a + 4 * d_r and d_a % PAIR == 0 and d_r % PAIR == 0
    assert w_down.shape[1] % FF_CHUNK == 0

    bf = jnp.bfloat16
    cos_t, sin_t = _rotary_tables(seq)
    dmask, dstart, dend, cdec = _decay_tables(d_r // HEAD_DIM)
    bias = _bias_tables(rel_bias)

    x2 = x.reshape(batch * seq, d)
    for l in range(depth):
        w = w_in[l].astype(bf)
        wqvt = jnp.concatenate([w[:, :d_a], w[:, 2 * d_a:3 * d_a]], axis=1).T
        wk = w[:, d_a:2 * d_a]
        wqkr = w[:, 3 * d_a:3 * d_a + 2 * d_r]
        wvgr = w[:, 3 * d_a + 2 * d_r:]
        qat, ka, vt, gate, qr, kr, vr, gr = _projection(
            x2, norm_mix_pre[l][None], wqvt, wk, wqkr, wvgr, cos_t, sin_t, batch, seq)
        ya = _moba(rel_bias, qat, ka, vt, gate, bias, batch, seq)
        yr = _retention(qr, kr, vr, gr, dmask, dstart, dend, cdec, batch, seq)
        wo = w_out[l].astype(bf)
        x2 = _out_ffn(ya, yr, x2, wo[:d_a], wo[d_a:], norm_mix_post[l][None],
                      norm_ffn_pre[l][None], w_up[l].astype(bf), conv_w[l], conv_b[l][None],
                      w_down[l].astype(bf), norm_ffn_post[l][None], seq)
    return x2.reshape(batch, seq, d)
```

```python
import functools
import math

import jax
import jax.numpy as jnp
from jax import lax
from jax.experimental import pallas as pl
from jax.experimental.pallas import tpu as pltpu

HEAD_DIM = 64
PAIR = 2 * HEAD_DIM
MOBA_BLOCK = 256
MOBA_TOPK = 3
RET_CHUNK = 256
REL_BUCKETS = 32
REL_MAX_DIST = 128
ROPE_BASE = 10000.0
CONV_WIDTH = 3
EPS = 1e-6
NEG = -1e30
LOG2E = math.log2(math.e)
Q_SCALE = HEAD_DIM ** -0.5 * LOG2E
V_ROWS = HEAD_DIM + 16
ROW_TILE = 512
FF_CHUNK = 256
FAR_UNROLL = 4
FAR_AHEAD = 2
HALO = 8
VMEM_LIMIT = 56 * 1024 * 1024

NT = (((1,), (1,)), ((), ()))
TN = (((0,), (0,)), ((), ()))


def _rms(x, g):
    return x * lax.rsqrt(jnp.mean(x * x, axis=-1, keepdims=True) + EPS) * g


def _bias_kernel(rb_ref, o_ref):
    h = pl.program_id(0)
    key = lax.broadcasted_iota(jnp.int32, (MOBA_BLOCK, MOBA_BLOCK), 0)
    qry = lax.broadcasted_iota(jnp.int32, (MOBA_BLOCK, MOBA_BLOCK), 1)
    max_exact = REL_BUCKETS // 2
    for which in range(2):
        rel = qry - key + which * MOBA_BLOCK
        n = jnp.maximum(rel, 0)
        n_f = jnp.maximum(n, 1).astype(jnp.float32)
        large = max_exact + (jnp.log(n_f / max_exact) / math.log(REL_MAX_DIST / max_exact)
                             * (REL_BUCKETS - max_exact)).astype(jnp.int32)
        large = jnp.minimum(large, REL_BUCKETS - 1)
        bucket = jnp.where(n < max_exact, n, large)
        val = jnp.zeros((MOBA_BLOCK, MOBA_BLOCK), jnp.float32)
        for b in range(REL_BUCKETS):
            val = jnp.where(bucket == b, rb_ref[b, h], val)
        o_ref[0, which] = jnp.where(rel >= 0, val * LOG2E, NEG)


def _bias_tables(rel_bias):
    n_heads = rel_bias.shape[1]
    return pl.pallas_call(
        _bias_kernel,
        grid=(n_heads,),
        in_specs=[pl.BlockSpec(memory_space=pltpu.SMEM)],
        out_specs=pl.BlockSpec((1, 2, MOBA_BLOCK, MOBA_BLOCK), lambda h: (h, 0, 0, 0)),
        out_shape=jax.ShapeDtypeStruct((n_heads, 2, MOBA_BLOCK, MOBA_BLOCK), jnp.float32),
        name="bias_tables",
    )(rel_bias)


def _proj_kernel(x_ref, g_ref, wqvt_ref, wk_ref, wqkr_ref, wvgr_ref, cos_ref, sin_ref,
                 qat_ref, ka_ref, vt_ref, gate_ref, qr_ref, kr_ref, vr_ref, gr_ref,
                 kmt_ref, *, tiles_per_batch, d_a, d_r):
    t_in_b = pl.program_id(0) % tiles_per_batch
    n_heads = d_a // HEAD_DIM
    tm = x_ref.shape[0]
    n_blocks = tiles_per_batch * (tm // MOBA_BLOCK)

    @pl.when(t_in_b == 0)
    def _():
        kmt_ref[...] = jnp.zeros_like(kmt_ref)

    hb = _rms(x_ref[...], g_ref[...]).astype(jnp.bfloat16)

    qvt = lax.dot_general(wqvt_ref[...], hb, NT, preferred_element_type=jnp.float32)
    qt = qvt[:d_a]
    qat_ref[...] = (qt * Q_SCALE).astype(jnp.bfloat16)
    k = jnp.dot(hb, wk_ref[...], preferred_element_type=jnp.float32)
    ka_ref[...] = k.astype(jnp.bfloat16)

    lane_head = lax.broadcasted_iota(jnp.int32, (1, d_a), 1) // HEAD_DIM
    for half in range(tm // MOBA_BLOCK):
        rows = slice(half * MOBA_BLOCK, (half + 1) * MOBA_BLOCK)
        gate_ref[:, rows] = jnp.dot(
            kmt_ref[...].astype(jnp.bfloat16), qt[:, rows].astype(jnp.bfloat16),
            preferred_element_type=jnp.float32)
        k_mean = jnp.mean(k[rows], axis=0, keepdims=True)
        j = t_in_b * (tm // MOBA_BLOCK) + half
        for hh in range(n_heads):
            kmt_ref[pl.ds(hh * n_blocks + j, 1), :] = jnp.where(lane_head == hh, k_mean, 0.0)

    vt = qvt[d_a:].astype(jnp.bfloat16)
    for hh in range(n_heads):
        vt_ref[hh, 0:HEAD_DIM, :] = vt[hh * HEAD_DIM:(hh + 1) * HEAD_DIM, :]
        vt_ref[hh, HEAD_DIM:V_ROWS, :] = jnp.ones((V_ROWS - HEAD_DIM, tm), jnp.bfloat16)

    qkr = jnp.dot(hb, wqkr_ref[...], preferred_element_type=jnp.float32)
    cos = cos_ref[...]
    sin = sin_ref[...]
    first_half = (lax.broadcasted_iota(jnp.int32, (tm, PAIR), 1) % HEAD_DIM) < HEAD_DIM // 2
    for cidx in range(2 * d_r // PAIR):
        xc = qkr[:, cidx * PAIR:(cidx + 1) * PAIR]
        sw = jnp.where(first_half,
                       pltpu.roll(xc, PAIR - HEAD_DIM // 2, axis=1),
                       pltpu.roll(xc, HEAD_DIM // 2, axis=1))
        rc = xc * cos + sw * sin
        if cidx < d_r // PAIR:
            qr_ref[:, cidx * PAIR:(cidx + 1) * PAIR] = rc.astype(jnp.bfloat16)
        else:
            c2 = cidx - d_r // PAIR
            kr_ref[:, c2 * PAIR:(c2 + 1) * PAIR] = (rc * HEAD_DIM ** -0.5).astype(jnp.bfloat16)

    vg = jnp.dot(hb, wvgr_ref[...], preferred_element_type=jnp.float32)
    vr_ref[...] = vg[:, :d_r].astype(jnp.bfloat16)
    gr_ref[...] = vg[:, d_r:].astype(jnp.bfloat16)


def _const_spec(shape):
    zeros = (0,) * len(shape)
    return pl.BlockSpec(shape, lambda *_: zeros, pipeline_mode=pl.Buffered(1))


def _projection(x2, g, wqvt, wk, wqkr, wvgr, cos_t, sin_t, batch, seq):
    t, d = x2.shape
    d_a = wk.shape[1]
    d_r = wqkr.shape[1] // 2
    n_heads = d_a // HEAD_DIM
    tm = ROW_TILE
    tpb = seq // tm
    n_blocks = seq // MOBA_BLOCK
    row = lambda w: pl.BlockSpec((tm, w), lambda i: (i, 0))
    bf = jnp.bfloat16
    return pl.pallas_call(
        functools.partial(_proj_kernel, tiles_per_batch=tpb, d_a=d_a, d_r=d_r),
        grid=(t // tm,),
        in_specs=[row(d), _const_spec((1, d)), _const_spec(wqvt.shape), _const_spec(wk.shape),
                  _const_spec(wqkr.shape), _const_spec(wvgr.shape),
                  pl.BlockSpec((tm, PAIR), lambda i: (i % tpb, 0)),
                  pl.BlockSpec((tm, PAIR), lambda i: (i % tpb, 0))],
        out_specs=[pl.BlockSpec((None, d_a, tm), lambda i: (i // tpb, 0, i % tpb)), row(d_a),
                   pl.BlockSpec((None, n_heads, V_ROWS, tm), lambda i: (i // tpb, 0, 0, i % tpb)),
                   pl.BlockSpec((None, n_heads * n_blocks, tm), lambda i: (i // tpb, 0, i % tpb)),
                   row(d_r), row(d_r), row(d_r), row(d_r)],
        out_shape=[jax.ShapeDtypeStruct((batch, d_a, seq), bf), jax.ShapeDtypeStruct((t, d_a), bf),
                   jax.ShapeDtypeStruct((batch, n_heads, V_ROWS, seq), bf),
                   jax.ShapeDtypeStruct((batch, n_heads * n_blocks, seq), jnp.float32),
                   jax.ShapeDtypeStruct((t, d_r), bf), jax.ShapeDtypeStruct((t, d_r), bf),
                   jax.ShapeDtypeStruct((t, d_r), bf), jax.ShapeDtypeStruct((t, d_r), bf)],
        scratch_shapes=[pltpu.VMEM((n_heads * n_blocks, d_a), jnp.float32)],
        compiler_params=pltpu.CompilerParams(
            dimension_semantics=("arbitrary",), vmem_limit_bytes=VMEM_LIMIT),
        name="projection",
    )(x2, g, wqvt, wk, wqkr, wvgr, cos_t, sin_t)


def _moba_kernel(rb_ref, qt_ref, k_ref, vt_ref, gate_ref, bias_ref, o_ref, sel_ref, s_ref, mj_ref,
                 p_ref, *, n_blocks):
    hp = pl.program_id(1)
    c = pl.program_id(2)
    qt = qt_ref[...]
    feat = lax.broadcasted_iota(jnp.int32, qt.shape, 0)
    qm = [jnp.where(feat < HEAD_DIM, qt, jnp.zeros_like(qt)),
          jnp.where(feat >= HEAD_DIM, qt, jnp.zeros_like(qt))]

    slot_prev, slot_own = FAR_UNROLL, FAR_UNROLL + 1
    far_bias = [rb_ref[REL_BUCKETS - 1, 2 * hp + h] * LOG2E for h in range(2)]
    prev = jnp.maximum(c - 1, 0)
    n_far = prev
    last = jnp.maximum(n_far - 1, 0)

    def block_rows(j):
        return pl.ds(pl.multiple_of(j * MOBA_BLOCK, MOBA_BLOCK), MOBA_BLOCK)

    def scores(slot, j, bias_index=None):
        for h in range(2):
            s = jnp.dot(k_ref[block_rows(j), :], qm[h], preferred_element_type=jnp.float32)
            if bias_index is not None:
                s = s + bias_ref[h, bias_index]
            s_ref[slot, h] = s
            mj_ref[slot, h] = jnp.max(s, axis=0, keepdims=True)

    def probs(slot):
        return [jnp.exp2((s_ref[slot, h] - mj_ref[slot, h]).astype(jnp.bfloat16))
                for h in range(2)]

    def weighted(j, p):
        return [jnp.dot(vt_ref[h, :, block_rows(j)], p[h], preferred_element_type=jnp.float32)
                for h in range(2)]

    def merge(states, mj, oj, j, valid):
        out = []
        for h in range(2):
            m, acc = states[h]
            chosen = (sel_ref[h, pl.ds(j, 1), :] > 0.5) & valid
            m_new = jnp.where(chosen, jnp.maximum(m, mj[h]), m)
            beta = jnp.where(chosen, jnp.exp2(mj[h] - m_new), 0.0)
            out.append((m_new, acc * jnp.exp2(m - m_new) + oj[h] * beta))
        return tuple(out)

    scores(slot_own, c, 0)
    scores(slot_prev, prev, 1)
    for i in range(FAR_AHEAD):
        scores(i, jnp.minimum(i, last))

    blk = lax.broadcasted_iota(jnp.int32, (n_blocks, MOBA_BLOCK), 0)
    for h in range(2):
        g = jnp.where(blk < c, gate_ref[h * n_blocks:(h + 1) * n_blocks, :], -jnp.inf)
        sel = jnp.zeros(g.shape, jnp.float32)
        for _ in range(MOBA_TOPK):
            top = jnp.max(g, axis=0, keepdims=True)
            idx = jnp.min(jnp.where(g == top, blk, n_blocks), axis=0, keepdims=True)
            pick = (blk == idx) & (blk < c)
            sel = jnp.where(pick, 1.0, sel)
            g = jnp.where(pick, -jnp.inf, g)
        sel_ref[h] = sel

    o_own = weighted(c, probs(slot_own))
    o_prev = weighted(prev, probs(slot_prev))
    states = tuple((mj_ref[slot_own, h], o_own[h]) for h in range(2))
    states = merge(states, [mj_ref[slot_prev, h] for h in range(2)], o_prev, prev, True)

    p_ref[...] = jnp.zeros_like(p_ref)
    pend_m = tuple(jnp.zeros((1, MOBA_BLOCK), jnp.float32) for _ in range(2))

    def body(t, carry):
        states, pend_m, pend_j, pend_valid = carry
        base = FAR_UNROLL * t
        js = [jnp.minimum(base + i, last) for i in range(FAR_UNROLL + FAR_AHEAD)]
        pending = (pend_m, weighted(pend_j, [p_ref[h] for h in range(2)]), pend_j, pend_valid > 0)
        for i in range(FAR_UNROLL):
            scores((i + FAR_AHEAD) % FAR_UNROLL, js[i + FAR_AHEAD])
            p = probs(i)
            m_i = tuple(mj_ref[i, h] + far_bias[h] for h in range(2))
            if i + 1 < FAR_UNROLL:
                o_i = weighted(js[i], p)
            else:
                for h in range(2):
                    p_ref[h] = p[h]
            states = merge(states, *pending)
            if i + 1 < FAR_UNROLL:
                pending = (m_i, o_i, js[i], base + i < n_far)
        return (states, m_i, js[FAR_UNROLL - 1],
                (base + FAR_UNROLL - 1 < n_far).astype(jnp.int32))

    states, pend_m, pend_j, pend_valid = lax.fori_loop(
        0, (n_far + FAR_UNROLL - 1) // FAR_UNROLL, body,
        (states, pend_m, jnp.int32(0), jnp.int32(0)))
    o_pend = weighted(pend_j, [p_ref[h] for h in range(2)])
    states = merge(states, pend_m, o_pend, pend_j, pend_valid > 0)

    outs = []
    for h in range(2):
        _, acc = states[h]
        outs.append((acc[:HEAD_DIM] / acc[HEAD_DIM:HEAD_DIM + 1]).T)
    o_ref[...] = jnp.concatenate(outs, axis=1).astype(o_ref.dtype)


def _moba(rel_bias, qat, ka, vt, gate, bias, batch, seq):
    d_a = ka.shape[1]
    n_blocks = seq // MOBA_BLOCK
    ka3 = ka.reshape(batch, seq, d_a)
    out = pl.pallas_call(
        functools.partial(_moba_kernel, n_blocks=n_blocks),
        grid=(batch, d_a // PAIR, n_blocks),
        in_specs=[pl.BlockSpec(memory_space=pltpu.SMEM),
                  pl.BlockSpec((None, PAIR, MOBA_BLOCK), lambda b, p, c: (b, p, c)),
                  pl.BlockSpec((None, seq, PAIR), lambda b, p, c: (b, 0, p)),
                  pl.BlockSpec((None, 2, V_ROWS, seq), lambda b, p, c: (b, p, 0, 0)),
                  pl.BlockSpec((None, 2 * n_blocks, MOBA_BLOCK), lambda b, p, c: (b, p, c)),
                  pl.BlockSpec((2, 2, MOBA_BLOCK, MOBA_BLOCK), lambda b, p, c: (p, 0, 0, 0))],
        out_specs=pl.BlockSpec((None, MOBA_BLOCK, PAIR), lambda b, p, c: (b, c, p)),
        out_shape=jax.ShapeDtypeStruct((batch, seq, d_a), jnp.bfloat16),
        scratch_shapes=[pltpu.VMEM((2, n_blocks, MOBA_BLOCK), jnp.float32),
                        pltpu.VMEM((FAR_UNROLL + 2, 2, MOBA_BLOCK, MOBA_BLOCK), jnp.float32),
                        pltpu.VMEM((FAR_UNROLL + 2, 2, 1, MOBA_BLOCK), jnp.float32),
                        pltpu.VMEM((2, MOBA_BLOCK, MOBA_BLOCK), jnp.bfloat16)],
        compiler_params=pltpu.CompilerParams(
            dimension_semantics=("arbitrary", "arbitrary", "arbitrary"),
            vmem_limit_bytes=VMEM_LIMIT),
        name="moba",
    )(rel_bias, qat, ka3, vt, gate, bias)
    return out.reshape(batch * seq, d_a)


def _ret_kernel(q_ref, k_ref, v_ref, g_ref, dmask_ref, dstart_ref, dend_ref, cdec_ref,
                o_ref, state_ref):
    @pl.when(pl.program_id(1) == 0)
    def _():
        state_ref[...] = jnp.zeros_like(state_ref)

    n_pairs = state_ref.shape[0]
    bf = jnp.bfloat16
    f32 = jnp.float32
    cols = [slice(p * PAIR, (p + 1) * PAIR) for p in range(n_pairs)]
    lane = lax.broadcasted_iota(jnp.int32, (RET_CHUNK, PAIR), 1)
    in_head = [lane < HEAD_DIM, lane >= HEAD_DIM]
    r = lax.broadcasted_iota(jnp.int32, (PAIR, PAIR), 0) // HEAD_DIM
    cc = lax.broadcasted_iota(jnp.int32, (PAIR, PAIR), 1) // HEAD_DIM
    same_head = r == cc
    avg = jnp.where(same_head, 1.0 / HEAD_DIM, 0.0).astype(bf)

    q = [q_ref[:, c] for c in cols]
    k = [k_ref[:, c] for c in cols]
    v = [v_ref[:, c] for c in cols]

    sc = [[lax.dot_general(jnp.where(in_head[h], q[p], jnp.zeros_like(q[p])), k[p], NT,
                           preferred_element_type=f32) for h in range(2)]
          for p in range(n_pairs)]
    state = [state_ref[p] for p in range(n_pairs)]
    cross = [jnp.dot(q[p], state[p].astype(bf), preferred_element_type=f32)
             for p in range(n_pairs)]
    kv = [lax.dot_general((k[p].astype(f32) * dend_ref[:, cols[p]]).astype(bf), v[p], TN,
                          preferred_element_type=f32) for p in range(n_pairs)]
    for p in range(n_pairs):
        state_ref[p] = jnp.where(same_head, state[p] * cdec_ref[:, cols[p]] + kv[p], 0.0)

    y = []
    for p in range(n_pairs):
        yh = [jnp.dot((sc[p][h] * dmask_ref[2 * p + h]).astype(bf), v[p],
                      preferred_element_type=f32) for h in range(2)]
        y.append(jnp.where(in_head[0], yh[0], yh[1]) + cross[p] * dstart_ref[:, cols[p]])

    def head_mean(t):
        return jnp.dot(t.astype(bf), avg, preferred_element_type=f32)

    y_hi = [t.astype(bf) for t in y]
    mu = [jnp.dot(y_hi[p], avg, preferred_element_type=f32)
          + head_mean(y[p] - y_hi[p].astype(f32)) for p in range(n_pairs)]
    d = [y[p] - mu[p] for p in range(n_pairs)]
    var = [head_mean(d[p] * d[p]) for p in range(n_pairs)]
    for p in range(n_pairs):
        g = g_ref[:, cols[p]].astype(f32)
        o_ref[:, cols[p]] = (g * jax.nn.sigmoid(g) * d[p] * lax.rsqrt(var[p] + EPS)).astype(bf)


def _retention(qr, kr, vr, gr, dmask, dstart, dend, cdec, batch, seq):
    d_r = qr.shape[1]
    n_chunks = seq // RET_CHUNK
    blk = pl.BlockSpec((None, RET_CHUNK, d_r), lambda b, c: (b, c, 0))
    r3 = lambda a: a.reshape(batch, seq, d_r)
    out = pl.pallas_call(
        _ret_kernel,
        grid=(batch, n_chunks),
        in_specs=[blk, blk, blk, blk, _const_spec(dmask.shape), _const_spec(dstart.shape),
                  _const_spec(dend.shape), _const_spec(cdec.shape)],
        out_specs=blk,
        out_shape=jax.ShapeDtypeStruct((batch, seq, d_r), jnp.bfloat16),
        scratch_shapes=[pltpu.VMEM((d_r // PAIR, PAIR, PAIR), jnp.float32)],
        compiler_params=pltpu.CompilerParams(
            dimension_semantics=("arbitrary", "arbitrary"), vmem_limit_bytes=VMEM_LIMIT),
        name="retention",
    )(r3(qr), r3(kr), r3(vr), r3(gr), dmask, dstart, dend, cdec)
    return out.reshape(batch * seq, d_r)


def _ffn_kernel(ya_ref, yr_ref, x_ref, woa_ref, wor_ref, g1_ref, g2_ref, wup_ref, cw_ref, cb_ref,
                wdn_ref, g3_ref, o_ref, u_ref, *, tiles_per_batch, d_ff):
    tm = x_ref.shape[0]

    @pl.when(pl.program_id(0) % tiles_per_batch == 0)
    def _():
        u_ref[0:HALO, :] = jnp.zeros((HALO, u_ref.shape[1]), jnp.float32)

    y = (jnp.dot(ya_ref[...], woa_ref[...], preferred_element_type=jnp.float32)
         + jnp.dot(yr_ref[...], wor_ref[...], preferred_element_type=jnp.float32))
    x1 = x_ref[...] + _rms(y, g1_ref[...])
    h2 = _rms(x1, g2_ref[...]).astype(jnp.bfloat16)

    def chunk_cols(ci):
        lo = ci * FF_CHUNK
        return slice(lo, lo + FF_CHUNK), slice(d_ff + lo, d_ff + lo + FF_CHUNK)

    def up(ci):
        for cols in chunk_cols(ci):
            u_ref[HALO:HALO + tm, cols] = jnp.dot(h2, wup_ref[:, cols],
                                                  preferred_element_type=jnp.float32)

    def conv(cols):
        w = cw_ref[:, cols]
        return (cb_ref[:, cols]
                + w[2:3] * u_ref[HALO:HALO + tm, cols]
                + w[1:2] * u_ref[HALO - 1:HALO - 1 + tm, cols]
                + w[0:1] * u_ref[HALO - 2:HALO - 2 + tm, cols])

    n_chunks = d_ff // FF_CHUNK
    acc = jnp.zeros((tm, o_ref.shape[1]), jnp.float32)
    up(0)
    for ci in range(n_chunks):
        if ci + 1 < n_chunks:
            up(ci + 1)
        cols_a, cols_b = chunk_cols(ci)
        f = (jax.nn.gelu(conv(cols_a), approximate=True) * conv(cols_b)).astype(jnp.bfloat16)
        acc = acc + jnp.dot(f, wdn_ref[ci * FF_CHUNK:(ci + 1) * FF_CHUNK, :],
                            preferred_element_type=jnp.float32)
    u_ref[0:HALO, :] = u_ref[tm:tm + HALO, :]
    o_ref[...] = x1 + _rms(acc, g3_ref[...])


def _out_ffn(ya, yr, x2, woa, wor, g1, g2, wup, cw, cb, wdn, g3, seq):
    t, d = x2.shape
    d_ff = wdn.shape[0]
    tm = ROW_TILE
    row = lambda w: pl.BlockSpec((tm, w), lambda i: (i, 0))
    return pl.pallas_call(
        functools.partial(_ffn_kernel, tiles_per_batch=seq // tm, d_ff=d_ff),
        grid=(t // tm,),
        in_specs=[row(ya.shape[1]), row(yr.shape[1]), row(d),
                  _const_spec(woa.shape), _const_spec(wor.shape),
                  _const_spec((1, d)), _const_spec((1, d)), _const_spec(wup.shape),
                  _const_spec(cw.shape), _const_spec(cb.shape), _const_spec(wdn.shape),
                  _const_spec((1, d))],
        out_specs=row(d),
        out_shape=jax.ShapeDtypeStruct((t, d), jnp.float32),
        scratch_shapes=[pltpu.VMEM((HALO + tm, 2 * d_ff), jnp.float32)],
        compiler_params=pltpu.CompilerParams(
            dimension_semantics=("arbitrary",), vmem_limit_bytes=VMEM_LIMIT),
        name="out_ffn",
    )(ya, yr, x2, woa, wor, g1, g2, wup, cw, cb, wdn, g3)


def _rotary_tables(seq):
    inv = ROPE_BASE ** (-jnp.arange(0, HEAD_DIM, 2, dtype=jnp.float32) / HEAD_DIM)
    ang = jnp.arange(seq, dtype=jnp.float32)[:, None] * inv[None, :]
    cos, sin = jnp.cos(ang), jnp.sin(ang)
    cos_t = jnp.tile(jnp.concatenate([cos, cos], axis=1), (1, PAIR // HEAD_DIM))
    sin_t = jnp.tile(jnp.concatenate([-sin, sin], axis=1), (1, PAIR // HEAD_DIM))
    return cos_t, sin_t


def _decay_tables(n_heads):
    c = RET_CHUNK
    log_gamma = jnp.log(1.0 - 2.0 ** (-5.0 - jnp.arange(n_heads, dtype=jnp.float32)))
    n = jnp.arange(c, dtype=jnp.float32)
    rel = n[:, None] - n[None, :]
    dmask = jnp.where(rel >= 0, jnp.exp(jnp.maximum(rel, 0.0)[None] * log_gamma[:, None, None]), 0.0)
    dend = jnp.exp((c - 1.0 - n)[:, None] * log_gamma[None, :])
    dstart = jnp.exp((n + 1.0)[:, None] * log_gamma[None, :])
    cdec = jnp.exp(c * log_gamma)[None, :]
    wide = lambda a: jnp.repeat(a, HEAD_DIM, axis=1)
    return dmask, wide(dstart), wide(dend), wide(cdec)


def kernel(x, norm_mix_pre, w_in, rel_bias, w_out, norm_mix_post, norm_ffn_pre, w_up, conv_w,
           conv_b, w_down, norm_ffn_post):
    batch, seq, d = x.shape
    depth = w_in.shape[0]
    d_a = w_out.shape[1] // 2
    d_r = w_out.shape[1] - d_a
    assert seq % ROW_TILE == 0 and ROW_TILE % MOBA_BLOCK == 0 and MOBA_BLOCK == RET_CHUNK
    assert w_in.shape[2] == 3 * d_a + 4 * d_r and d_a % PAIR == 0 and d_r % PAIR == 0
    assert w_down.shape[1] % FF_CHUNK == 0

    bf = jnp.bfloat16
    cos_t, sin_t = _rotary_tables(seq)
    dmask, dstart, dend, cdec = _decay_tables(d_r // HEAD_DIM)
    bias = _bias_tables(rel_bias)

    x2 = x.reshape(batch * seq, d)
    for l in range(depth):
        w = w_in[l].astype(bf)
        wqvt = jnp.concatenate([w[:, :d_a], w[:, 2 * d_a:3 * d_a]], axis=1).T
        wk = w[:, d_a:2 * d_a]
        wqkr = w[:, 3 * d_a:3 * d_a + 2 * d_r]
        wvgr = w[:, 3 * d_a + 2 * d_r:]
        qat, ka, vt, gate, qr, kr, vr, gr = _projection(
            x2, norm_mix_pre[l][None], wqvt, wk, wqkr, wvgr, cos_t, sin_t, batch, seq)
        ya = _moba(rel_bias, qat, ka, vt, gate, bias, batch, seq)
        yr = _retention(qr, kr, vr, gr, dmask, dstart, dend, cdec, batch, seq)
        wo = w_out[l].astype(bf)
        x2 = _out_ffn(ya, yr, x2, wo[:d_a], wo[d_a:], norm_mix_post[l][None],
                      norm_ffn_pre[l][None], w_up[l].astype(bf), conv_w[l], conv_b[l][None],
                      w_down[l].astype(bf), norm_ffn_post[l][None], seq)
    return x2.reshape(batch, seq, d)
```

```python
import functools
import math

import jax
import jax.numpy as jnp
from jax import lax
from jax.experimental import pallas as pl
from jax.experimental.pallas import tpu as pltpu

HEAD_DIM = 64
PAIR = 2 * HEAD_DIM
MOBA_BLOCK = 256
MOBA_TOPK = 3
RET_CHUNK = 256
REL_BUCKETS = 32
REL_MAX_DIST = 128
ROPE_BASE = 10000.0
CONV_WIDTH = 3
EPS = 1e-6
NEG = -1e30
LOG2E = math.log2(math.e)
Q_SCALE = HEAD_DIM ** -0.5 * LOG2E
V_ROWS = HEAD_DIM + 16
ROW_TILE = 512
FF_CHUNK = 256
FAR_UNROLL = 8
FAR_UNROLL_TAIL = 4
FAR_AHEAD = 2
HALO = 8
VMEM_LIMIT = 56 * 1024 * 1024

NT = (((1,), (1,)), ((), ()))
TN = (((0,), (0,)), ((), ()))


def _rms(x, g):
    return x * lax.rsqrt(jnp.mean(x * x, axis=-1, keepdims=True) + EPS) * g


def _bias_kernel(rb_ref, o_ref):
    h = pl.program_id(0)
    key = lax.broadcasted_iota(jnp.int32, (MOBA_BLOCK, MOBA_BLOCK), 0)
    qry = lax.broadcasted_iota(jnp.int32, (MOBA_BLOCK, MOBA_BLOCK), 1)
    max_exact = REL_BUCKETS // 2
    for which in range(2):
        rel = qry - key + which * MOBA_BLOCK
        n = jnp.maximum(rel, 0)
        n_f = jnp.maximum(n, 1).astype(jnp.float32)
        large = max_exact + (jnp.log(n_f / max_exact) / math.log(REL_MAX_DIST / max_exact)
                             * (REL_BUCKETS - max_exact)).astype(jnp.int32)
        large = jnp.minimum(large, REL_BUCKETS - 1)
        bucket = jnp.where(n < max_exact, n, large)
        val = jnp.zeros((MOBA_BLOCK, MOBA_BLOCK), jnp.float32)
        for b in range(REL_BUCKETS):
            val = jnp.where(bucket == b, rb_ref[b, h], val)
        o_ref[0, which] = jnp.where(rel >= 0, val * LOG2E, NEG)


def _bias_tables(rel_bias):
    n_heads = rel_bias.shape[1]
    return pl.pallas_call(
        _bias_kernel,
        grid=(n_heads,),
        in_specs=[pl.BlockSpec(memory_space=pltpu.SMEM)],
        out_specs=pl.BlockSpec((1, 2, MOBA_BLOCK, MOBA_BLOCK), lambda h: (h, 0, 0, 0)),
        out_shape=jax.ShapeDtypeStruct((n_heads, 2, MOBA_BLOCK, MOBA_BLOCK), jnp.float32),
        name="bias_tables",
    )(rel_bias)


def _proj_kernel(x_ref, g_ref, wqvt_ref, wk_ref, wqkr_ref, wvgr_ref, cos_ref, sin_ref,
                 qat_ref, ka_ref, vt_ref, gate_ref, qr_ref, kr_ref, vr_ref, gr_ref,
                 kmt_ref, *, tiles_per_batch, d_a, d_r):
    t_in_b = pl.program_id(0) % tiles_per_batch
    n_heads = d_a // HEAD_DIM
    tm = x_ref.shape[0]
    n_blocks = tiles_per_batch * (tm // MOBA_BLOCK)

    @pl.when(t_in_b == 0)
    def _():
        kmt_ref[...] = jnp.zeros_like(kmt_ref)

    hb = _rms(x_ref[...], g_ref[...]).astype(jnp.bfloat16)

    qvt = lax.dot_general(wqvt_ref[...], hb, NT, preferred_element_type=jnp.float32)
    qt = qvt[:d_a]
    qat_ref[...] = (qt * Q_SCALE).astype(jnp.bfloat16)
    k = jnp.dot(hb, wk_ref[...], preferred_element_type=jnp.float32)
    ka_ref[...] = k.astype(jnp.bfloat16)

    lane_head = lax.broadcasted_iota(jnp.int32, (1, d_a), 1) // HEAD_DIM
    for half in range(tm // MOBA_BLOCK):
        rows = slice(half * MOBA_BLOCK, (half + 1) * MOBA_BLOCK)
        gate_ref[:, rows] = jnp.dot(
            kmt_ref[...].astype(jnp.bfloat16), qt[:, rows].astype(jnp.bfloat16),
            preferred_element_type=jnp.float32)
        k_mean = jnp.mean(k[rows], axis=0, keepdims=True)
        j = t_in_b * (tm // MOBA_BLOCK) + half
        for hh in range(n_heads):
            kmt_ref[pl.ds(hh * n_blocks + j, 1), :] = jnp.where(lane_head == hh, k_mean, 0.0)

    vt = qvt[d_a:].astype(jnp.bfloat16)
    for hh in range(n_heads):
        vt_ref[hh, 0:HEAD_DIM, :] = vt[hh * HEAD_DIM:(hh + 1) * HEAD_DIM, :]
        vt_ref[hh, HEAD_DIM:V_ROWS, :] = jnp.ones((V_ROWS - HEAD_DIM, tm), jnp.bfloat16)

    qkr = jnp.dot(hb, wqkr_ref[...], preferred_element_type=jnp.float32)
    cos = cos_ref[...]
    sin = sin_ref[...]
    first_half = (lax.broadcasted_iota(jnp.int32, (tm, PAIR), 1) % HEAD_DIM) < HEAD_DIM // 2
    for cidx in range(2 * d_r // PAIR):
        xc = qkr[:, cidx * PAIR:(cidx + 1) * PAIR]
        sw = jnp.where(first_half,
                       pltpu.roll(xc, PAIR - HEAD_DIM // 2, axis=1),
                       pltpu.roll(xc, HEAD_DIM // 2, axis=1))
        rc = xc * cos + sw * sin
        if cidx < d_r // PAIR:
            qr_ref[:, cidx * PAIR:(cidx + 1) * PAIR] = rc.astype(jnp.bfloat16)
        else:
            c2 = cidx - d_r // PAIR
            kr_ref[:, c2 * PAIR:(c2 + 1) * PAIR] = (rc * HEAD_DIM ** -0.5).astype(jnp.bfloat16)

    vg = jnp.dot(hb, wvgr_ref[...], preferred_element_type=jnp.float32)
    vr_ref[...] = vg[:, :d_r].astype(jnp.bfloat16)
    gr_ref[...] = vg[:, d_r:].astype(jnp.bfloat16)


def _const_spec(shape):
    zeros = (0,) * len(shape)
    return pl.BlockSpec(shape, lambda *_: zeros, pipeline_mode=pl.Buffered(1))


def _projection(x2, g, wqvt, wk, wqkr, wvgr, cos_t, sin_t, batch, seq):
    t, d = x2.shape
    d_a = wk.shape[1]
    d_r = wqkr.shape[1] // 2
    n_heads = d_a // HEAD_DIM
    tm = ROW_TILE
    tpb = seq // tm
    n_blocks = seq // MOBA_BLOCK
    row = lambda w: pl.BlockSpec((tm, w), lambda i: (i, 0))
    bf = jnp.bfloat16
    return pl.pallas_call(
        functools.partial(_proj_kernel, tiles_per_batch=tpb, d_a=d_a, d_r=d_r),
        grid=(t // tm,),
        in_specs=[row(d), _const_spec((1, d)), _const_spec(wqvt.shape), _const_spec(wk.shape),
                  _const_spec(wqkr.shape), _const_spec(wvgr.shape),
                  pl.BlockSpec((tm, PAIR), lambda i: (i % tpb, 0)),
                  pl.BlockSpec((tm, PAIR), lambda i: (i % tpb, 0))],
        out_specs=[pl.BlockSpec((None, d_a, tm), lambda i: (i // tpb, 0, i % tpb)), row(d_a),
                   pl.BlockSpec((None, n_heads, V_ROWS, tm), lambda i: (i // tpb, 0, 0, i % tpb)),
                   pl.BlockSpec((None, n_heads * n_blocks, tm), lambda i: (i // tpb, 0, i % tpb)),
                   row(d_r), row(d_r), row(d_r), row(d_r)],
        out_shape=[jax.ShapeDtypeStruct((batch, d_a, seq), bf), jax.ShapeDtypeStruct((t, d_a), bf),
                   jax.ShapeDtypeStruct((batch, n_heads, V_ROWS, seq), bf),
                   jax.ShapeDtypeStruct((batch, n_heads * n_blocks, seq), jnp.float32),
                   jax.ShapeDtypeStruct((t, d_r), bf), jax.ShapeDtypeStruct((t, d_r), bf),
                   jax.ShapeDtypeStruct((t, d_r), bf), jax.ShapeDtypeStruct((t, d_r), bf)],
        scratch_shapes=[pltpu.VMEM((n_heads * n_blocks, d_a), jnp.float32)],
        compiler_params=pltpu.CompilerParams(
            dimension_semantics=("arbitrary",), vmem_limit_bytes=VMEM_LIMIT),
        name="projection",
    )(x2, g, wqvt, wk, wqkr, wvgr, cos_t, sin_t)


def _moba_kernel(rb_ref, qt_ref, k_ref, vt_ref, gate_ref, bias_ref, o_ref, sel_ref, s_ref, mj_ref,
                 p_ref, *, n_blocks):
    hp = pl.program_id(1)
    c = pl.program_id(2)
    qt = qt_ref[...]
    feat = lax.broadcasted_iota(jnp.int32, qt.shape, 0)
    qm = [jnp.where(feat < HEAD_DIM, qt, jnp.zeros_like(qt)),
          jnp.where(feat >= HEAD_DIM, qt, jnp.zeros_like(qt))]

    slot_prev, slot_own = FAR_UNROLL, FAR_UNROLL + 1
    far_bias = [rb_ref[REL_BUCKETS - 1, 2 * hp + h] * LOG2E for h in range(2)]
    prev = jnp.maximum(c - 1, 0)
    n_far = prev
    last = jnp.maximum(n_far - 1, 0)

    def block_rows(j):
        return pl.ds(pl.multiple_of(j * MOBA_BLOCK, MOBA_BLOCK), MOBA_BLOCK)

    def scores(slot, j, bias_index=None):
        for h in range(2):
            s = jnp.dot(k_ref[block_rows(j), :], qm[h], preferred_element_type=jnp.float32)
            if bias_index is not None:
                s = s + bias_ref[h, bias_index]
            s_ref[slot, h] = s
            mj_ref[slot, h] = jnp.max(s, axis=0, keepdims=True)

    def probs(slot):
        return [jnp.exp2((s_ref[slot, h] - mj_ref[slot, h]).astype(jnp.bfloat16))
                for h in range(2)]

    def weighted(j, p):
        return [jnp.dot(vt_ref[h, :, block_rows(j)], p[h], preferred_element_type=jnp.float32)
                for h in range(2)]

    def merge(states, mj, oj, j, valid, always=False):
        out = []
        for h in range(2):
            m, acc = states[h]
            if always:
                m_new = jnp.maximum(m, mj[h])
                beta = jnp.exp2(mj[h] - m_new)
            else:
                chosen = (sel_ref[h, pl.ds(j, 1), :] > 0.5) & valid
                m_new = jnp.where(chosen, jnp.maximum(m, mj[h]), m)
                beta = jnp.where(chosen, jnp.exp2(mj[h] - m_new), 0.0)
            out.append((m_new, acc * jnp.exp2(m - m_new) + oj[h] * beta))
        return tuple(out)

    scores(slot_own, c, 0)
    scores(slot_prev, prev, 1)
    for i in range(FAR_AHEAD):
        scores(i, jnp.minimum(i, last))

    blk = lax.broadcasted_iota(jnp.int32, (n_blocks, MOBA_BLOCK), 0)
    for h in range(2):
        g = jnp.where(blk < c, gate_ref[h * n_blocks:(h + 1) * n_blocks, :], -jnp.inf)
        sel = jnp.zeros(g.shape, jnp.float32)
        for _ in range(MOBA_TOPK):
            top = jnp.max(g, axis=0, keepdims=True)
            idx = jnp.min(jnp.where(g == top, blk, n_blocks), axis=0, keepdims=True)
            pick = (blk == idx) & (blk < c)
            sel = jnp.where(pick, 1.0, sel)
            g = jnp.where(pick, -jnp.inf, g)
        sel_ref[h] = sel

    states = tuple((jnp.full((1, MOBA_BLOCK), NEG, jnp.float32),
                    jnp.zeros((V_ROWS, MOBA_BLOCK), jnp.float32)) for _ in range(2))

    p_ref[...] = jnp.zeros_like(p_ref)
    pend_m = tuple(jnp.zeros((1, MOBA_BLOCK), jnp.float32) for _ in range(2))

    def trip(unroll, first):
        def body(t, carry):
            states, pend_m, pend_j, pend_valid = carry
            base = first + unroll * t
            js = [jnp.minimum(base + i, last) for i in range(unroll + FAR_AHEAD)]
            pending = (pend_m, weighted(pend_j, [p_ref[h] for h in range(2)]), pend_j,
                       pend_valid > 0)
            for i in range(unroll):
                scores((i + FAR_AHEAD) % unroll, js[i + FAR_AHEAD])
                p = probs(i)
                m_i = tuple(mj_ref[i, h] + far_bias[h] for h in range(2))
                if i + 1 < unroll:
                    o_i = weighted(js[i], p)
                else:
                    for h in range(2):
                        p_ref[h] = p[h]
                states = merge(states, *pending)
                if i + 1 < unroll:
                    pending = (m_i, o_i, js[i], base + i < n_far)
            return (states, m_i, js[unroll - 1], (base + unroll - 1 < n_far).astype(jnp.int32))
        return body

    n_long = n_far // FAR_UNROLL
    done = n_long * FAR_UNROLL
    carry = lax.fori_loop(0, n_long, trip(FAR_UNROLL, 0),
                          (states, pend_m, jnp.int32(0), jnp.int32(0)))
    states, pend_m, pend_j, pend_valid = lax.fori_loop(
        0, (n_far - done + FAR_UNROLL_TAIL - 1) // FAR_UNROLL_TAIL,
        trip(FAR_UNROLL_TAIL, done), carry)
    o_pend = weighted(pend_j, [p_ref[h] for h in range(2)])
    o_prev = weighted(prev, probs(slot_prev))
    o_own = weighted(c, probs(slot_own))
    states = merge(states, pend_m, o_pend, pend_j, pend_valid > 0)
    states = merge(states, [mj_ref[slot_prev, h] for h in range(2)], o_prev, prev, True)
    states = merge(states, [mj_ref[slot_own, h] for h in range(2)], o_own, c, True, always=True)

    outs = []
    for h in range(2):
        _, acc = states[h]
        outs.append((acc[:HEAD_DIM] / acc[HEAD_DIM:HEAD_DIM + 1]).T)
    o_ref[...] = jnp.concatenate(outs, axis=1).astype(o_ref.dtype)


def _moba(rel_bias, qat, ka, vt, gate, bias, batch, seq):
    d_a = ka.shape[1]
    n_blocks = seq // MOBA_BLOCK
    ka3 = ka.reshape(batch, seq, d_a)
    out = pl.pallas_call(
        functools.partial(_moba_kernel, n_blocks=n_blocks),
        grid=(batch, d_a // PAIR, n_blocks),
        in_specs=[pl.BlockSpec(memory_space=pltpu.SMEM),
                  pl.BlockSpec((None, PAIR, MOBA_BLOCK), lambda b, p, c: (b, p, c)),
                  pl.BlockSpec((None, seq, PAIR), lambda b, p, c: (b, 0, p)),
                  pl.BlockSpec((None, 2, V_ROWS, seq), lambda b, p, c: (b, p, 0, 0)),
                  pl.BlockSpec((None, 2 * n_blocks, MOBA_BLOCK), lambda b, p, c: (b, p, c)),
                  pl.BlockSpec((2, 2, MOBA_BLOCK, MOBA_BLOCK), lambda b, p, c: (p, 0, 0, 0))],
        out_specs=pl.BlockSpec((None, MOBA_BLOCK, PAIR), lambda b, p, c: (b, c, p)),
        out_shape=jax.ShapeDtypeStruct((batch, seq, d_a), jnp.bfloat16),
        scratch_shapes=[pltpu.VMEM((2, n_blocks, MOBA_BLOCK), jnp.float32),
                        pltpu.VMEM((FAR_UNROLL + 2, 2, MOBA_BLOCK, MOBA_BLOCK), jnp.float32),
                        pltpu.VMEM((FAR_UNROLL + 2, 2, 1, MOBA_BLOCK), jnp.float32),
                        pltpu.VMEM((2, MOBA_BLOCK, MOBA_BLOCK), jnp.bfloat16)],
        compiler_params=pltpu.CompilerParams(
            dimension_semantics=("arbitrary", "arbitrary", "arbitrary"),
            vmem_limit_bytes=VMEM_LIMIT),
        name="moba",
    )(rel_bias, qat, ka3, vt, gate, bias)
    return out.reshape(batch * seq, d_a)


def _ret_kernel(q_ref, k_ref, v_ref, g_ref, dmask_ref, dstart_ref, dend_ref, cdec_ref,
                o_ref, state_ref):
    @pl.when(pl.program_id(1) == 0)
    def _():
        state_ref[...] = jnp.zeros_like(state_ref)

    n_pairs = state_ref.shape[0]
    bf = jnp.bfloat16
    f32 = jnp.float32
    cols = [slice(p * PAIR, (p + 1) * PAIR) for p in range(n_pairs)]
    lane = lax.broadcasted_iota(jnp.int32, (RET_CHUNK, PAIR), 1)
    in_head = [lane < HEAD_DIM, lane >= HEAD_DIM]
    r = lax.broadcasted_iota(jnp.int32, (PAIR, PAIR), 0) // HEAD_DIM
    cc = lax.broadcasted_iota(jnp.int32, (PAIR, PAIR), 1) // HEAD_DIM
    same_head = r == cc
    avg = jnp.where(same_head, 1.0 / HEAD_DIM, 0.0).astype(bf)

    q = [q_ref[:, c] for c in cols]
    k = [k_ref[:, c] for c in cols]
    v = [v_ref[:, c] for c in cols]

    sc = [[lax.dot_general(jnp.where(in_head[h], q[p], jnp.zeros_like(q[p])), k[p], NT,
                           preferred_element_type=f32) for h in range(2)]
          for p in range(n_pairs)]
    state = [state_ref[p] for p in range(n_pairs)]
    cross = [jnp.dot(q[p], state[p].astype(bf), preferred_element_type=f32)
             for p in range(n_pairs)]
    kv = [lax.dot_general((k[p].astype(f32) * dend_ref[:, cols[p]]).astype(bf), v[p], TN,
                          preferred_element_type=f32) for p in range(n_pairs)]
    for p in range(n_pairs):
        state_ref[p] = jnp.where(same_head, state[p] * cdec_ref[:, cols[p]] + kv[p], 0.0)

    y = []
    for p in range(n_pairs):
        yh = [jnp.dot((sc[p][h] * dmask_ref[2 * p + h]).astype(bf), v[p],
                      preferred_element_type=f32) for h in range(2)]
        y.append(jnp.where(in_head[0], yh[0], yh[1]) + cross[p] * dstart_ref[:, cols[p]])

    def head_mean(t):
        return jnp.dot(t.astype(bf), avg, preferred_element_type=f32)

    y_hi = [t.astype(bf) for t in y]
    mu = [jnp.dot(y_hi[p], avg, preferred_element_type=f32)
          + head_mean(y[p] - y_hi[p].astype(f32)) for p in range(n_pairs)]
    d = [y[p] - mu[p] for p in range(n_pairs)]
    var = [head_mean(d[p] * d[p]) for p in range(n_pairs)]
    for p in range(n_pairs):
        g = g_ref[:, cols[p]].astype(f32)
        o_ref[:, cols[p]] = (g * jax.nn.sigmoid(g) * d[p] * lax.rsqrt(var[p] + EPS)).astype(bf)


def _retention(qr, kr, vr, gr, dmask, dstart, dend, cdec, batch, seq):
    d_r = qr.shape[1]
    n_chunks = seq // RET_CHUNK
    blk = pl.BlockSpec((None, RET_CHUNK, d_r), lambda b, c: (b, c, 0))
    r3 = lambda a: a.reshape(batch, seq, d_r)
    out = pl.pallas_call(
        _ret_kernel,
        grid=(batch, n_chunks),
        in_specs=[blk, blk, blk, blk, _const_spec(dmask.shape), _const_spec(dstart.shape),
                  _const_spec(dend.shape), _const_spec(cdec.shape)],
        out_specs=blk,
        out_shape=jax.ShapeDtypeStruct((batch, seq, d_r), jnp.bfloat16),
        scratch_shapes=[pltpu.VMEM((d_r // PAIR, PAIR, PAIR), jnp.float32)],
        compiler_params=pltpu.CompilerParams(
            dimension_semantics=("arbitrary", "arbitrary"), vmem_limit_bytes=VMEM_LIMIT),
        name="retention",
    )(r3(qr), r3(kr), r3(vr), r3(gr), dmask, dstart, dend, cdec)
    return out.reshape(batch * seq, d_r)


def _ffn_kernel(ya_ref, yr_ref, x_ref, woa_ref, wor_ref, g1_ref, g2_ref, wup_ref, cw_ref, cb_ref,
                wdn_ref, g3_ref, o_ref, u_ref, *, tiles_per_batch, d_ff):
    tm = x_ref.shape[0]

    @pl.when(pl.program_id(0) % tiles_per_batch == 0)
    def _():
        u_ref[0:HALO, :] = jnp.zeros((HALO, u_ref.shape[1]), jnp.float32)

    y = (jnp.dot(ya_ref[...], woa_ref[...], preferred_element_type=jnp.float32)
         + jnp.dot(yr_ref[...], wor_ref[...], preferred_element_type=jnp.float32))
    x1 = x_ref[...] + _rms(y, g1_ref[...])
    h2 = _rms(x1, g2_ref[...]).astype(jnp.bfloat16)

    def chunk_cols(ci):
        lo = ci * FF_CHUNK
        return slice(lo, lo + FF_CHUNK), slice(d_ff + lo, d_ff + lo + FF_CHUNK)

    def up(ci):
        for cols in chunk_cols(ci):
            u_ref[HALO:HALO + tm, cols] = jnp.dot(h2, wup_ref[:, cols],
                                                  preferred_element_type=jnp.float32)

    def conv(cols):
        w = cw_ref[:, cols]
        return (cb_ref[:, cols]
                + w[2:3] * u_ref[HALO:HALO + tm, cols]
                + w[1:2] * u_ref[HALO - 1:HALO - 1 + tm, cols]
                + w[0:1] * u_ref[HALO - 2:HALO - 2 + tm, cols])

    n_chunks = d_ff // FF_CHUNK
    acc = jnp.zeros((tm, o_ref.shape[1]), jnp.float32)
    up(0)
    for ci in range(n_chunks):
        if ci + 1 < n_chunks:
            up(ci + 1)
        cols_a, cols_b = chunk_cols(ci)
        f = (jax.nn.gelu(conv(cols_a), approximate=True) * conv(cols_b)).astype(jnp.bfloat16)
        acc = acc + jnp.dot(f, wdn_ref[ci * FF_CHUNK:(ci + 1) * FF_CHUNK, :],
                            preferred_element_type=jnp.float32)
    u_ref[0:HALO, :] = u_ref[tm:tm + HALO, :]
    o_ref[...] = x1 + _rms(acc, g3_ref[...])


def _out_ffn(ya, yr, x2, woa, wor, g1, g2, wup, cw, cb, wdn, g3, seq):
    t, d = x2.shape
    d_ff = wdn.shape[0]
    tm = ROW_TILE
    row = lambda w: pl.BlockSpec((tm, w), lambda i: (i, 0))
    return pl.pallas_call(
        functools.partial(_ffn_kernel, tiles_per_batch=seq // tm, d_ff=d_ff),
        grid=(t // tm,),
        in_specs=[row(ya.shape[1]), row(yr.shape[1]), row(d),
                  _const_spec(woa.shape), _const_spec(wor.shape),
                  _const_spec((1, d)), _const_spec((1, d)), _const_spec(wup.shape),
                  _const_spec(cw.shape), _const_spec(cb.shape), _const_spec(wdn.shape),
                  _const_spec((1, d))],
        out_specs=row(d),
        out_shape=jax.ShapeDtypeStruct((t, d), jnp.float32),
        scratch_shapes=[pltpu.VMEM((HALO + tm, 2 * d_ff), jnp.float32)],
        compiler_params=pltpu.CompilerParams(
            dimension_semantics=("arbitrary",), vmem_limit_bytes=VMEM_LIMIT),
        name="out_ffn",
    )(ya, yr, x2, woa, wor, g1, g2, wup, cw, cb, wdn, g3)


def _rotary_tables(seq):
    inv = ROPE_BASE ** (-jnp.arange(0, HEAD_DIM, 2, dtype=jnp.float32) / HEAD_DIM)
    ang = jnp.arange(seq, dtype=jnp.float32)[:, None] * inv[None, :]
    cos, sin = jnp.cos(ang), jnp.sin(ang)
    cos_t = jnp.tile(jnp.concatenate([cos, cos], axis=1), (1, PAIR // HEAD_DIM))
    sin_t = jnp.tile(jnp.concatenate([-sin, sin], axis=1), (1, PAIR // HEAD_DIM))
    return cos_t, sin_t


def _decay_tables(n_heads):
    c = RET_CHUNK
    log_gamma = jnp.log(1.0 - 2.0 ** (-5.0 - jnp.arange(n_heads, dtype=jnp.float32)))
    n = jnp.arange(c, dtype=jnp.float32)
    rel = n[:, None] - n[None, :]
    dmask = jnp.where(rel >= 0, jnp.exp(jnp.maximum(rel, 0.0)[None] * log_gamma[:, None, None]), 0.0)
    dend = jnp.exp((c - 1.0 - n)[:, None] * log_gamma[None, :])
    dstart = jnp.exp((n + 1.0)[:, None] * log_gamma[None, :])
    cdec = jnp.exp(c * log_gamma)[None, :]
    wide = lambda a: jnp.repeat(a, HEAD_DIM, axis=1)
    return dmask, wide(dstart), wide(dend), wide(cdec)


def kernel(x, norm_mix_pre, w_in, rel_bias, w_out, norm_mix_post, norm_ffn_pre, w_up, conv_w,
           conv_b, w_down, norm_ffn_post):
    batch, seq, d = x.shape
    depth = w_in.shape[0]
    d_a = w_out.shape[1] // 2
    d_r = w_out.shape[1] - d_a
    assert seq % ROW_TILE == 0 and ROW_TILE % MOBA_BLOCK == 0 and MOBA_BLOCK == RET_CHUNK
    assert w_in.shape[2] == 3 * d_a + 4 * d_r and d_a % PAIR == 0 and d_r % PAIR == 0
    assert w_down.shape[1] % FF_CHUNK == 0

    bf = jnp.bfloat16
    cos_t, sin_t = _rotary_tables(seq)
    dmask, dstart, dend, cdec = _decay_tables(d_r // HEAD_DIM)
    bias = _bias_tables(rel_bias)

    x2 = x.reshape(batch * seq, d)
    for l in range(depth):
        w = w_in[l].astype(bf)
        wqvt = jnp.concatenate([w[:, :d_a], w[:, 2 * d_a:3 * d_a]], axis=1).T
        wk = w[:, d_a:2 * d_a]
        wqkr = w[:, 3 * d_a:3 * d_a + 2 * d_r]
        wvgr = w[:, 3 * d_a + 2 * d_r:]
        qat, ka, vt, gate, qr, kr, vr, gr = _projection(
            x2, norm_mix_pre[l][None], wqvt, wk, wqkr, wvgr, cos_t, sin_t, batch, seq)
        ya = _moba(rel_bias, qat, ka, vt, gate, bias, batch, seq)
        yr = _retention(qr, kr, vr, gr, dmask, dstart, dend, cdec, batch, seq)
        wo = w_out[l].astype(bf)
        x2 = _out_ffn(ya, yr, x2, wo[:d_a], wo[d_a:], norm_mix_post[l][None],
                      norm_ffn_pre[l][None], w_up[l].astype(bf), conv_w[l], conv_b[l][None],
                      w_down[l].astype(bf), norm_ffn_post[l][None], seq)
    return x2.reshape(batch, seq, d)
```

```python
import functools
import math

import jax
import jax.numpy as jnp
from jax import lax
from jax.experimental import pallas as pl
from jax.experimental.pallas import tpu as pltpu

HEAD_DIM = 64
PAIR = 2 * HEAD_DIM
MOBA_BLOCK = 256
MOBA_TOPK = 3
RET_CHUNK = 256
REL_BUCKETS = 32
REL_MAX_DIST = 128
ROPE_BASE = 10000.0
CONV_WIDTH = 3
EPS = 1e-6
NEG = -1e30
LOG2E = math.log2(math.e)
Q_SCALE = HEAD_DIM ** -0.5 * LOG2E
V_ROWS = HEAD_DIM + 16
ROW_TILE = 512
FF_CHUNK = 256
MOBA_HEADS = 2
FAR_UNROLL = 8
FAR_UNROLL_TAIL = 4
FAR_AHEAD = 2
FF_GROUPS = 2
HALO = 8
VMEM_LIMIT = 56 * 1024 * 1024

NT = (((1,), (1,)), ((), ()))
TN = (((0,), (0,)), ((), ()))


def _rms(x, g):
    return x * lax.rsqrt(jnp.mean(x * x, axis=-1, keepdims=True) + EPS) * g


def _bias_kernel(rb_ref, o_ref):
    h = pl.program_id(0)
    key = lax.broadcasted_iota(jnp.int32, (MOBA_BLOCK, MOBA_BLOCK), 0)
    qry = lax.broadcasted_iota(jnp.int32, (MOBA_BLOCK, MOBA_BLOCK), 1)
    max_exact = REL_BUCKETS // 2
    for which in range(2):
        rel = qry - key + which * MOBA_BLOCK
        n = jnp.maximum(rel, 0)
        n_f = jnp.maximum(n, 1).astype(jnp.float32)
        large = max_exact + (jnp.log(n_f / max_exact) / math.log(REL_MAX_DIST / max_exact)
                             * (REL_BUCKETS - max_exact)).astype(jnp.int32)
        large = jnp.minimum(large, REL_BUCKETS - 1)
        bucket = jnp.where(n < max_exact, n, large)
        val = jnp.zeros((MOBA_BLOCK, MOBA_BLOCK), jnp.float32)
        for b in range(REL_BUCKETS):
            val = jnp.where(bucket == b, rb_ref[b, h], val)
        o_ref[0, which] = jnp.where(rel >= 0, val * LOG2E, NEG)


def _bias_tables(rel_bias):
    n_heads = rel_bias.shape[1]
    return pl.pallas_call(
        _bias_kernel,
        grid=(n_heads,),
        in_specs=[pl.BlockSpec(memory_space=pltpu.SMEM)],
        out_specs=pl.BlockSpec((1, 2, MOBA_BLOCK, MOBA_BLOCK), lambda h: (h, 0, 0, 0)),
        out_shape=jax.ShapeDtypeStruct((n_heads, 2, MOBA_BLOCK, MOBA_BLOCK), jnp.float32),
        name="bias_tables",
    )(rel_bias)


def _proj_kernel(x_ref, g_ref, wqvt_ref, wk_ref, wqkr_ref, wvgr_ref, cos_ref, sin_ref,
                 qat_ref, ka_ref, vt_ref, gate_ref, qr_ref, kr_ref, vr_ref, gr_ref,
                 kmt_ref, *, tiles_per_batch, d_a, d_r):
    t_in_b = pl.program_id(0) % tiles_per_batch
    n_heads = d_a // HEAD_DIM
    tm = x_ref.shape[0]
    n_blocks = tiles_per_batch * (tm // MOBA_BLOCK)

    @pl.when(t_in_b == 0)
    def _():
        kmt_ref[...] = jnp.zeros_like(kmt_ref)

    hb = _rms(x_ref[...], g_ref[...]).astype(jnp.bfloat16)

    qvt = lax.dot_general(wqvt_ref[...], hb, NT, preferred_element_type=jnp.float32)
    qt = qvt[:d_a]
    qat_ref[...] = (qt * Q_SCALE).astype(jnp.bfloat16)
    k = jnp.dot(hb, wk_ref[...], preferred_element_type=jnp.float32)
    ka_ref[...] = k.astype(jnp.bfloat16)

    lane_head = lax.broadcasted_iota(jnp.int32, (1, d_a), 1) // HEAD_DIM
    for half in range(tm // MOBA_BLOCK):
        rows = slice(half * MOBA_BLOCK, (half + 1) * MOBA_BLOCK)
        gate_ref[:, rows] = jnp.dot(
            kmt_ref[...].astype(jnp.bfloat16), qt[:, rows].astype(jnp.bfloat16),
            preferred_element_type=jnp.float32)
        k_mean = jnp.mean(k[rows], axis=0, keepdims=True)
        j = t_in_b * (tm // MOBA_BLOCK) + half
        for hh in range(n_heads):
            kmt_ref[pl.ds(hh * n_blocks + j, 1), :] = jnp.where(lane_head == hh, k_mean, 0.0)

    vt = qvt[d_a:].astype(jnp.bfloat16)
    for hh in range(n_heads):
        vt_ref[hh, 0:HEAD_DIM, :] = vt[hh * HEAD_DIM:(hh + 1) * HEAD_DIM, :]
        vt_ref[hh, HEAD_DIM:V_ROWS, :] = jnp.ones((V_ROWS - HEAD_DIM, tm), jnp.bfloat16)

    qkr = jnp.dot(hb, wqkr_ref[...], preferred_element_type=jnp.float32)
    cos = cos_ref[...]
    sin = sin_ref[...]
    first_half = (lax.broadcasted_iota(jnp.int32, (tm, PAIR), 1) % HEAD_DIM) < HEAD_DIM // 2
    for cidx in range(2 * d_r // PAIR):
        xc = qkr[:, cidx * PAIR:(cidx + 1) * PAIR]
        sw = jnp.where(first_half,
                       pltpu.roll(xc, PAIR - HEAD_DIM // 2, axis=1),
                       pltpu.roll(xc, HEAD_DIM // 2, axis=1))
        rc = xc * cos + sw * sin
        if cidx < d_r // PAIR:
            qr_ref[:, cidx * PAIR:(cidx + 1) * PAIR] = rc.astype(jnp.bfloat16)
        else:
            c2 = cidx - d_r // PAIR
            kr_ref[:, c2 * PAIR:(c2 + 1) * PAIR] = (rc * HEAD_DIM ** -0.5).astype(jnp.bfloat16)

    vg = jnp.dot(hb, wvgr_ref[...], preferred_element_type=jnp.float32)
    vr_ref[...] = vg[:, :d_r].astype(jnp.bfloat16)
    gr_ref[...] = vg[:, d_r:].astype(jnp.bfloat16)


def _const_spec(shape):
    zeros = (0,) * len(shape)
    return pl.BlockSpec(shape, lambda *_: zeros, pipeline_mode=pl.Buffered(1))


def _projection(x2, g, wqvt, wk, wqkr, wvgr, cos_t, sin_t, batch, seq):
    t, d = x2.shape
    d_a = wk.shape[1]
    d_r = wqkr.shape[1] // 2
    n_heads = d_a // HEAD_DIM
    tm = ROW_TILE
    tpb = seq // tm
    n_blocks = seq // MOBA_BLOCK
    row = lambda w: pl.BlockSpec((tm, w), lambda i: (i, 0))
    bf = jnp.bfloat16
    return pl.pallas_call(
        functools.partial(_proj_kernel, tiles_per_batch=tpb, d_a=d_a, d_r=d_r),
        grid=(t // tm,),
        in_specs=[row(d), _const_spec((1, d)), _const_spec(wqvt.shape), _const_spec(wk.shape),
                  _const_spec(wqkr.shape), _const_spec(wvgr.shape),
                  pl.BlockSpec((tm, PAIR), lambda i: (i % tpb, 0)),
                  pl.BlockSpec((tm, PAIR), lambda i: (i % tpb, 0))],
        out_specs=[pl.BlockSpec((None, d_a, tm), lambda i: (i // tpb, 0, i % tpb)), row(d_a),
                   pl.BlockSpec((None, n_heads, V_ROWS, tm), lambda i: (i // tpb, 0, 0, i % tpb)),
                   pl.BlockSpec((None, n_heads * n_blocks, tm), lambda i: (i // tpb, 0, i % tpb)),
                   row(d_r), row(d_r), row(d_r), row(d_r)],
        out_shape=[jax.ShapeDtypeStruct((batch, d_a, seq), bf), jax.ShapeDtypeStruct((t, d_a), bf),
                   jax.ShapeDtypeStruct((batch, n_heads, V_ROWS, seq), bf),
                   jax.ShapeDtypeStruct((batch, n_heads * n_blocks, seq), jnp.float32),
                   jax.ShapeDtypeStruct((t, d_r), bf), jax.ShapeDtypeStruct((t, d_r), bf),
                   jax.ShapeDtypeStruct((t, d_r), bf), jax.ShapeDtypeStruct((t, d_r), bf)],
        scratch_shapes=[pltpu.VMEM((n_heads * n_blocks, d_a), jnp.float32)],
        compiler_params=pltpu.CompilerParams(
            dimension_semantics=("arbitrary",), vmem_limit_bytes=VMEM_LIMIT),
        name="projection",
    )(x2, g, wqvt, wk, wqkr, wvgr, cos_t, sin_t)


def _moba_kernel(rb_ref, qt_ref, k_ref, vt_ref, gate_ref, bias_ref, o_ref, sel_ref, s_ref, mj_ref,
                 p_ref, *, n_blocks):
    nh = vt_ref.shape[0]
    heads = range(nh)
    group = pl.program_id(1)
    c = pl.program_id(2)

    def pair_cols(h):
        return slice((h // 2) * PAIR, (h // 2 + 1) * PAIR)

    feat = lax.broadcasted_iota(jnp.int32, (PAIR, MOBA_BLOCK), 0)
    qm = []
    for h in heads:
        qt = qt_ref[pair_cols(h), :]
        own_feat = (feat >= HEAD_DIM) if h % 2 else (feat < HEAD_DIM)
        qm.append(jnp.where(own_feat, qt, jnp.zeros_like(qt)))

    slot_prev, slot_own = FAR_UNROLL, FAR_UNROLL + 1
    far_bias = [rb_ref[REL_BUCKETS - 1, nh * group + h] * LOG2E for h in heads]
    prev = jnp.maximum(c - 1, 0)
    n_far = prev
    last = jnp.maximum(n_far - 1, 0)

    def block_rows(j):
        return pl.ds(pl.multiple_of(j * MOBA_BLOCK, MOBA_BLOCK), MOBA_BLOCK)

    def scores(slot, j, bias_index=None):
        for h in heads:
            s = jnp.dot(k_ref[block_rows(j), pair_cols(h)], qm[h],
                        preferred_element_type=jnp.float32)
            if bias_index is not None:
                s = s + bias_ref[h, bias_index]
            s_ref[slot, h] = s
            mj_ref[slot, h] = jnp.max(s, axis=0, keepdims=True)

    def probs(slot):
        return [jnp.exp2((s_ref[slot, h] - mj_ref[slot, h]).astype(jnp.bfloat16))
                for h in heads]

    def weighted(j, p):
        return [jnp.dot(vt_ref[h, :, block_rows(j)], p[h], preferred_element_type=jnp.float32)
                for h in heads]

    def merge(states, mj, oj, j, valid, always=False):
        out = []
        for h in heads:
            m, acc = states[h]
            if always:
                m_new = jnp.maximum(m, mj[h])
                beta = jnp.exp2(mj[h] - m_new)
            else:
                chosen = (sel_ref[h, pl.ds(j, 1), :] > 0.5) & valid
                m_new = jnp.where(chosen, jnp.maximum(m, mj[h]), m)
                beta = jnp.where(chosen, jnp.exp2(mj[h] - m_new), 0.0)
            out.append((m_new, acc * jnp.exp2(m - m_new) + oj[h] * beta))
        return tuple(out)

    scores(slot_own, c, 0)
    scores(slot_prev, prev, 1)
    for i in range(FAR_AHEAD):
        scores(i, jnp.minimum(i, last))

    blk = lax.broadcasted_iota(jnp.int32, (n_blocks, MOBA_BLOCK), 0)
    for h in heads:
        g = jnp.where(blk < c, gate_ref[h * n_blocks:(h + 1) * n_blocks, :], -jnp.inf)
        sel = jnp.zeros(g.shape, jnp.float32)
        for _ in range(MOBA_TOPK):
            top = jnp.max(g, axis=0, keepdims=True)
            idx = jnp.min(jnp.where(g == top, blk, n_blocks), axis=0, keepdims=True)
            pick = (blk == idx) & (blk < c)
            sel = jnp.where(pick, 1.0, sel)
            g = jnp.where(pick, -jnp.inf, g)
        sel_ref[h] = sel

    states = tuple((jnp.full((1, MOBA_BLOCK), NEG, jnp.float32),
                    jnp.zeros((V_ROWS, MOBA_BLOCK), jnp.float32)) for _ in heads)

    p_ref[...] = jnp.zeros_like(p_ref)
    pend_m = tuple(jnp.zeros((1, MOBA_BLOCK), jnp.float32) for _ in heads)

    def trip(unroll, first):
        def body(t, carry):
            states, pend_m, pend_j, pend_valid = carry
            base = first + unroll * t
            js = [jnp.minimum(base + i, last) for i in range(unroll + FAR_AHEAD)]
            pending = (pend_m, weighted(pend_j, [p_ref[h] for h in heads]), pend_j,
                       pend_valid > 0)
            for i in range(unroll):
                scores((i + FAR_AHEAD) % unroll, js[i + FAR_AHEAD])
                p = probs(i)
                m_i = tuple(mj_ref[i, h] + far_bias[h] for h in heads)
                if i + 1 < unroll:
                    o_i = weighted(js[i], p)
                else:
                    for h in heads:
                        p_ref[h] = p[h]
                states = merge(states, *pending)
                if i + 1 < unroll:
                    pending = (m_i, o_i, js[i], base + i < n_far)
            return (states, m_i, js[unroll - 1], (base + unroll - 1 < n_far).astype(jnp.int32))
        return body

    n_long = n_far // FAR_UNROLL
    done = n_long * FAR_UNROLL
    carry = lax.fori_loop(0, n_long, trip(FAR_UNROLL, 0),
                          (states, pend_m, jnp.int32(0), jnp.int32(0)))
    states, pend_m, pend_j, pend_valid = lax.fori_loop(
        0, (n_far - done + FAR_UNROLL_TAIL - 1) // FAR_UNROLL_TAIL,
        trip(FAR_UNROLL_TAIL, done), carry)
    o_pend = weighted(pend_j, [p_ref[h] for h in heads])
    o_prev = weighted(prev, probs(slot_prev))
    o_own = weighted(c, probs(slot_own))
    states = merge(states, pend_m, o_pend, pend_j, pend_valid > 0)
    states = merge(states, [mj_ref[slot_prev, h] for h in heads], o_prev, prev, True)
    states = merge(states, [mj_ref[slot_own, h] for h in heads], o_own, c, True, always=True)

    outs = [(acc[:HEAD_DIM] / acc[HEAD_DIM:HEAD_DIM + 1]).T for _, acc in states]
    o_ref[...] = jnp.concatenate(outs, axis=1).astype(o_ref.dtype)


def _moba(rel_bias, qat, ka, vt, gate, bias, batch, seq):
    d_a = ka.shape[1]
    n_blocks = seq // MOBA_BLOCK
    ka3 = ka.reshape(batch, seq, d_a)
    nh = MOBA_HEADS
    width = nh * HEAD_DIM
    out = pl.pallas_call(
        functools.partial(_moba_kernel, n_blocks=n_blocks),
        grid=(batch, d_a // width, n_blocks),
        in_specs=[pl.BlockSpec(memory_space=pltpu.SMEM),
                  pl.BlockSpec((None, width, MOBA_BLOCK), lambda b, g, c: (b, g, c)),
                  pl.BlockSpec((None, seq, width), lambda b, g, c: (b, 0, g)),
                  pl.BlockSpec((None, nh, V_ROWS, seq), lambda b, g, c: (b, g, 0, 0)),
                  pl.BlockSpec((None, nh * n_blocks, MOBA_BLOCK), lambda b, g, c: (b, g, c)),
                  pl.BlockSpec((nh, 2, MOBA_BLOCK, MOBA_BLOCK), lambda b, g, c: (g, 0, 0, 0))],
        out_specs=pl.BlockSpec((None, MOBA_BLOCK, width), lambda b, g, c: (b, c, g)),
        out_shape=jax.ShapeDtypeStruct((batch, seq, d_a), jnp.bfloat16),
        scratch_shapes=[pltpu.VMEM((nh, n_blocks, MOBA_BLOCK), jnp.float32),
                        pltpu.VMEM((FAR_UNROLL + 2, nh, MOBA_BLOCK, MOBA_BLOCK), jnp.float32),
                        pltpu.VMEM((FAR_UNROLL + 2, nh, 1, MOBA_BLOCK), jnp.float32),
                        pltpu.VMEM((nh, MOBA_BLOCK, MOBA_BLOCK), jnp.bfloat16)],
        compiler_params=pltpu.CompilerParams(
            dimension_semantics=("arbitrary", "arbitrary", "arbitrary"),
            vmem_limit_bytes=VMEM_LIMIT),
        name="moba",
    )(rel_bias, qat, ka3, vt, gate, bias)
    return out.reshape(batch * seq, d_a)


def _ret_kernel(q_ref, k_ref, v_ref, g_ref, dmask_ref, dstart_ref, dend_ref, cdec_ref,
                o_ref, state_ref):
    @pl.when(pl.program_id(1) == 0)
    def _():
        state_ref[...] = jnp.zeros_like(state_ref)

    n_pairs = state_ref.shape[0]
    bf = jnp.bfloat16
    f32 = jnp.float32
    cols = [slice(p * PAIR, (p + 1) * PAIR) for p in range(n_pairs)]
    lane = lax.broadcasted_iota(jnp.int32, (RET_CHUNK, PAIR), 1)
    in_head = [lane < HEAD_DIM, lane >= HEAD_DIM]
    r = lax.broadcasted_iota(jnp.int32, (PAIR, PAIR), 0) // HEAD_DIM
    cc = lax.broadcasted_iota(jnp.int32, (PAIR, PAIR), 1) // HEAD_DIM
    same_head = r == cc
    avg = jnp.where(same_head, 1.0 / HEAD_DIM, 0.0).astype(bf)

    q = [q_ref[:, c] for c in cols]
    k = [k_ref[:, c] for c in cols]
    v = [v_ref[:, c] for c in cols]

    sc = [[lax.dot_general(jnp.where(in_head[h], q[p], jnp.zeros_like(q[p])), k[p], NT,
                           preferred_element_type=f32) for h in range(2)]
          for p in range(n_pairs)]
    state = [state_ref[p] for p in range(n_pairs)]
    cross = [jnp.dot(q[p], state[p].astype(bf), preferred_element_type=f32)
             for p in range(n_pairs)]
    kv = [lax.dot_general((k[p].astype(f32) * dend_ref[:, cols[p]]).astype(bf), v[p], TN,
                          preferred_element_type=f32) for p in range(n_pairs)]
    for p in range(n_pairs):
        state_ref[p] = jnp.where(same_head, state[p] * cdec_ref[:, cols[p]] + kv[p], 0.0)

    y = []
    for p in range(n_pairs):
        yh = [jnp.dot((sc[p][h] * dmask_ref[2 * p + h]).astype(bf), v[p],
                      preferred_element_type=f32) for h in range(2)]
        y.append(jnp.where(in_head[0], yh[0], yh[1]) + cross[p] * dstart_ref[:, cols[p]])

    def head_mean(t):
        return jnp.dot(t.astype(bf), avg, preferred_element_type=f32)

    y_hi = [t.astype(bf) for t in y]
    mu = [jnp.dot(y_hi[p], avg, preferred_element_type=f32)
          + head_mean(y[p] - y_hi[p].astype(f32)) for p in range(n_pairs)]
    d = [y[p] - mu[p] for p in range(n_pairs)]
    var = [head_mean(d[p] * d[p]) for p in range(n_pairs)]
    for p in range(n_pairs):
        g = g_ref[:, cols[p]].astype(f32)
        o_ref[:, cols[p]] = (g * jax.nn.sigmoid(g) * d[p] * lax.rsqrt(var[p] + EPS)).astype(bf)


def _retention(qr, kr, vr, gr, dmask, dstart, dend, cdec, batch, seq):
    d_r = qr.shape[1]
    n_chunks = seq // RET_CHUNK
    blk = pl.BlockSpec((None, RET_CHUNK, d_r), lambda b, c: (b, c, 0))
    r3 = lambda a: a.reshape(batch, seq, d_r)
    out = pl.pallas_call(
        _ret_kernel,
        grid=(batch, n_chunks),
        in_specs=[blk, blk, blk, blk, _const_spec(dmask.shape), _const_spec(dstart.shape),
                  _const_spec(dend.shape), _const_spec(cdec.shape)],
        out_specs=blk,
        out_shape=jax.ShapeDtypeStruct((batch, seq, d_r), jnp.bfloat16),
        scratch_shapes=[pltpu.VMEM((d_r // PAIR, PAIR, PAIR), jnp.float32)],
        compiler_params=pltpu.CompilerParams(
            dimension_semantics=("arbitrary", "arbitrary"), vmem_limit_bytes=VMEM_LIMIT),
        name="retention",
    )(r3(qr), r3(kr), r3(vr), r3(gr), dmask, dstart, dend, cdec)
    return out.reshape(batch * seq, d_r)


def _ffn_kernel(ya_ref, yr_ref, x_ref, woa_ref, wor_ref, g1_ref, g2_ref, wup_ref, cw_ref, cb_ref,
                wdn_ref, g3_ref, o_ref, u_ref, f_ref, *, tiles_per_batch, d_ff):
    tm = x_ref.shape[0]

    @pl.when(pl.program_id(0) % tiles_per_batch == 0)
    def _():
        u_ref[0:HALO, :] = jnp.zeros((HALO, u_ref.shape[1]), jnp.float32)

    y = (jnp.dot(ya_ref[...], woa_ref[...], preferred_element_type=jnp.float32)
         + jnp.dot(yr_ref[...], wor_ref[...], preferred_element_type=jnp.float32))
    x1 = x_ref[...] + _rms(y, g1_ref[...])
    h2 = _rms(x1, g2_ref[...]).astype(jnp.bfloat16)

    def chunk_cols(ci):
        lo = ci * FF_CHUNK
        return slice(lo, lo + FF_CHUNK), slice(d_ff + lo, d_ff + lo + FF_CHUNK)

    def up(ci):
        for cols in chunk_cols(ci):
            u_ref[HALO:HALO + tm, cols] = jnp.dot(h2, wup_ref[:, cols],
                                                  preferred_element_type=jnp.float32)

    def conv(cols):
        w = cw_ref[:, cols]
        return (cb_ref[:, cols]
                + w[2:3] * u_ref[HALO:HALO + tm, cols]
                + w[1:2] * u_ref[HALO - 1:HALO - 1 + tm, cols]
                + w[0:1] * u_ref[HALO - 2:HALO - 2 + tm, cols])

    n_chunks = d_ff // FF_CHUNK
    group_ends = [(g + 1) * n_chunks // FF_GROUPS for g in range(FF_GROUPS)]
    acc = None
    start = 0
    up(0)
    for ci in range(n_chunks):
        if ci + 1 < n_chunks:
            up(ci + 1)
        cols_a, cols_b = chunk_cols(ci)
        f_ref[:, cols_a] = (jax.nn.gelu(conv(cols_a), approximate=True)
                            * conv(cols_b)).astype(jnp.bfloat16)
        if ci + 1 in group_ends:
            rows = slice(start * FF_CHUNK, (ci + 1) * FF_CHUNK)
            part = jnp.dot(f_ref[:, rows], wdn_ref[rows, :], preferred_element_type=jnp.float32)
            acc = part if acc is None else acc + part
            start = ci + 1
    u_ref[0:HALO, :] = u_ref[tm:tm + HALO, :]
    o_ref[...] = x1 + _rms(acc, g3_ref[...])


def _out_ffn(ya, yr, x2, woa, wor, g1, g2, wup, cw, cb, wdn, g3, seq):
    t, d = x2.shape
    d_ff = wdn.shape[0]
    tm = ROW_TILE
    row = lambda w: pl.BlockSpec((tm, w), lambda i: (i, 0))
    return pl.pallas_call(
        functools.partial(_ffn_kernel, tiles_per_batch=seq // tm, d_ff=d_ff),
        grid=(t // tm,),
        in_specs=[row(ya.shape[1]), row(yr.shape[1]), row(d),
                  _const_spec(woa.shape), _const_spec(wor.shape),
                  _const_spec((1, d)), _const_spec((1, d)), _const_spec(wup.shape),
                  _const_spec(cw.shape), _const_spec(cb.shape), _const_spec(wdn.shape),
                  _const_spec((1, d))],
        out_specs=row(d),
        out_shape=jax.ShapeDtypeStruct((t, d), jnp.float32),
        scratch_shapes=[pltpu.VMEM((HALO + tm, 2 * d_ff), jnp.float32),
                        pltpu.VMEM((tm, d_ff), jnp.bfloat16)],
        compiler_params=pltpu.CompilerParams(
            dimension_semantics=("arbitrary",), vmem_limit_bytes=VMEM_LIMIT),
        name="out_ffn",
    )(ya, yr, x2, woa, wor, g1, g2, wup, cw, cb, wdn, g3)


def _rotary_tables(seq):
    inv = ROPE_BASE ** (-jnp.arange(0, HEAD_DIM, 2, dtype=jnp.float32) / HEAD_DIM)
    ang = jnp.arange(seq, dtype=jnp.float32)[:, None] * inv[None, :]
    cos, sin = jnp.cos(ang), jnp.sin(ang)
    cos_t = jnp.tile(jnp.concatenate([cos, cos], axis=1), (1, PAIR // HEAD_DIM))
    sin_t = jnp.tile(jnp.concatenate([-sin, sin], axis=1), (1, PAIR // HEAD_DIM))
    return cos_t, sin_t


def _decay_tables(n_heads):
    c = RET_CHUNK
    log_gamma = jnp.log(1.0 - 2.0 ** (-5.0 - jnp.arange(n_heads, dtype=jnp.float32)))
    n = jnp.arange(c, dtype=jnp.float32)
    rel = n[:, None] - n[None, :]
    dmask = jnp.where(rel >= 0, jnp.exp(jnp.maximum(rel, 0.0)[None] * log_gamma[:, None, None]), 0.0)
    dend = jnp.exp((c - 1.0 - n)[:, None] * log_gamma[None, :])
    dstart = jnp.exp((n + 1.0)[:, None] * log_gamma[None, :])
    cdec = jnp.exp(c * log_gamma)[None, :]
    wide = lambda a: jnp.repeat(a, HEAD_DIM, axis=1)
    return dmask, wide(dstart), wide(dend), wide(cdec)


def kernel(x, norm_mix_pre, w_in, rel_bias, w_out, norm_mix_post, norm_ffn_pre, w_up, conv_w,
           conv_b, w_down, norm_ffn_post):
    batch, seq, d = x.shape
    depth = w_in.shape[0]
    d_a = w_out.shape[1] // 2
    d_r = w_out.shape[1] - d_a
    assert seq % ROW_TILE == 0 and ROW_TILE % MOBA_BLOCK == 0 and MOBA_BLOCK == RET_CHUNK
    assert w_in.shape[2] == 3 * d_a + 4 * d_r and d_a % PAIR == 0 and d_r % PAIR == 0
    assert w_down.shape[1] % FF_CHUNK == 0

    bf = jnp.bfloat16
    cos_t, sin_t = _rotary_tables(seq)
    dmask, dstart, dend, cdec = _decay_tables(d_r // HEAD_DIM)
    bias = _bias_tables(rel_bias)

    x2 = x.reshape(batch * seq, d)
    for l in range(depth):
        w = w_in[l].astype(bf)
        wqvt = jnp.concatenate([w[:, :d_a], w[:, 2 * d_a:3 * d_a]], axis=1).T
        wk = w[:, d_a:2 * d_a]
        wqkr = w[:, 3 * d_a:3 * d_a + 2 * d_r]
        wvgr = w[:, 3 * d_a + 2 * d_r:]
        qat, ka, vt, gate, qr, kr, vr, gr = _projection(
            x2, norm_mix_pre[l][None], wqvt, wk, wqkr, wvgr, cos_t, sin_t, batch, seq)
        ya = _moba(rel_bias, qat, ka, vt, gate, bias, batch, seq)
        yr = _retention(qr, kr, vr, gr, dmask, dstart, dend, cdec, batch, seq)
        wo = w_out[l].astype(bf)
        x2 = _out_ffn(ya, yr, x2, wo[:d_a], wo[d_a:], norm_mix_post[l][None],
                      norm_ffn_pre[l][None], w_up[l].astype(bf), conv_w[l], conv_b[l][None],
                      w_down[l].astype(bf), norm_ffn_post[l][None], seq)
    return x2.reshape(batch, seq, d)
```

```python
import functools
import math

import jax
import jax.numpy as jnp
from jax import lax
from jax.experimental import pallas as pl
from jax.experimental.pallas import tpu as pltpu

HEAD_DIM = 64
PAIR = 2 * HEAD_DIM
MOBA_BLOCK = 256
MOBA_TOPK = 3
RET_CHUNK = 256
REL_BUCKETS = 32
REL_MAX_DIST = 128
ROPE_BASE = 10000.0
CONV_WIDTH = 3
EPS = 1e-6
NEG = -1e30
LOG2E = math.log2(math.e)
Q_SCALE = HEAD_DIM ** -0.5 * LOG2E
V_ROWS = HEAD_DIM + 16
ROW_TILE = 512
FF_CHUNK = 256
MOBA_HEADS = 4
FAR_TRIPS = (16, 8, 4)
FAR_AHEAD = 2
FAR_RING = 4
FF_GROUPS = 2
HALO = 8
VMEM_LIMIT = 56 * 1024 * 1024

NT = (((1,), (1,)), ((), ()))
TN = (((0,), (0,)), ((), ()))


def _rms(x, g):
    return x * lax.rsqrt(jnp.mean(x * x, axis=-1, keepdims=True) + EPS) * g


def _bias_kernel(rb_ref, o_ref):
    h = pl.program_id(0)
    key = lax.broadcasted_iota(jnp.int32, (MOBA_BLOCK, MOBA_BLOCK), 0)
    qry = lax.broadcasted_iota(jnp.int32, (MOBA_BLOCK, MOBA_BLOCK), 1)
    max_exact = REL_BUCKETS // 2
    for which in range(2):
        rel = qry - key + which * MOBA_BLOCK
        n = jnp.maximum(rel, 0)
        n_f = jnp.maximum(n, 1).astype(jnp.float32)
        large = max_exact + (jnp.log(n_f / max_exact) / math.log(REL_MAX_DIST / max_exact)
                             * (REL_BUCKETS - max_exact)).astype(jnp.int32)
        large = jnp.minimum(large, REL_BUCKETS - 1)
        bucket = jnp.where(n < max_exact, n, large)
        val = jnp.zeros((MOBA_BLOCK, MOBA_BLOCK), jnp.float32)
        for b in range(REL_BUCKETS):
            val = jnp.where(bucket == b, rb_ref[b, h], val)
        o_ref[0, which] = jnp.where(rel >= 0, val * LOG2E, NEG)


def _bias_tables(rel_bias):
    n_heads = rel_bias.shape[1]
    return pl.pallas_call(
        _bias_kernel,
        grid=(n_heads,),
        in_specs=[pl.BlockSpec(memory_space=pltpu.SMEM)],
        out_specs=pl.BlockSpec((1, 2, MOBA_BLOCK, MOBA_BLOCK), lambda h: (h, 0, 0, 0)),
        out_shape=jax.ShapeDtypeStruct((n_heads, 2, MOBA_BLOCK, MOBA_BLOCK), jnp.float32),
        name="bias_tables",
    )(rel_bias)


def _proj_kernel(x_ref, g_ref, wqvt_ref, wk_ref, wqkr_ref, wvgr_ref, cos_ref, sin_ref,
                 qat_ref, ka_ref, vt_ref, gate_ref, qr_ref, kr_ref, vr_ref, gr_ref,
                 kmt_ref, *, tiles_per_batch, d_a, d_r):
    t_in_b = pl.program_id(0) % tiles_per_batch
    n_heads = d_a // HEAD_DIM
    tm = x_ref.shape[0]
    n_blocks = tiles_per_batch * (tm // MOBA_BLOCK)

    @pl.when(t_in_b == 0)
    def _():
        kmt_ref[...] = jnp.zeros_like(kmt_ref)

    hb = _rms(x_ref[...], g_ref[...]).astype(jnp.bfloat16)

    qvt = lax.dot_general(wqvt_ref[...], hb, NT, preferred_element_type=jnp.float32)
    qt = qvt[:d_a]
    qat_ref[...] = (qt * Q_SCALE).astype(jnp.bfloat16)
    k = jnp.dot(hb, wk_ref[...], preferred_element_type=jnp.float32)
    ka_ref[...] = k.astype(jnp.bfloat16)

    lane_head = lax.broadcasted_iota(jnp.int32, (1, d_a), 1) // HEAD_DIM
    for half in range(tm // MOBA_BLOCK):
        rows = slice(half * MOBA_BLOCK, (half + 1) * MOBA_BLOCK)
        gate_ref[:, rows] = jnp.dot(
            kmt_ref[...].astype(jnp.bfloat16), qt[:, rows].astype(jnp.bfloat16),
            preferred_element_type=jnp.float32)
        k_mean = jnp.mean(k[rows], axis=0, keepdims=True)
        j = t_in_b * (tm // MOBA_BLOCK) + half
        for hh in range(n_heads):
            kmt_ref[pl.ds(hh * n_blocks + j, 1), :] = jnp.where(lane_head == hh, k_mean, 0.0)

    vt = qvt[d_a:].astype(jnp.bfloat16)
    for hh in range(n_heads):
        vt_ref[hh, 0:HEAD_DIM, :] = vt[hh * HEAD_DIM:(hh + 1) * HEAD_DIM, :]
        vt_ref[hh, HEAD_DIM:V_ROWS, :] = jnp.ones((V_ROWS - HEAD_DIM, tm), jnp.bfloat16)

    qkr = jnp.dot(hb, wqkr_ref[...], preferred_element_type=jnp.float32)
    cos = cos_ref[...]
    sin = sin_ref[...]
    first_half = (lax.broadcasted_iota(jnp.int32, (tm, PAIR), 1) % HEAD_DIM) < HEAD_DIM // 2
    for cidx in range(2 * d_r // PAIR):
        xc = qkr[:, cidx * PAIR:(cidx + 1) * PAIR]
        sw = jnp.where(first_half,
                       pltpu.roll(xc, PAIR - HEAD_DIM // 2, axis=1),
                       pltpu.roll(xc, HEAD_DIM // 2, axis=1))
        rc = xc * cos + sw * sin
        if cidx < d_r // PAIR:
            qr_ref[:, cidx * PAIR:(cidx + 1) * PAIR] = rc.astype(jnp.bfloat16)
        else:
            c2 = cidx - d_r // PAIR
            kr_ref[:, c2 * PAIR:(c2 + 1) * PAIR] = (rc * HEAD_DIM ** -0.5).astype(jnp.bfloat16)

    vg = jnp.dot(hb, wvgr_ref[...], preferred_element_type=jnp.float32)
    vr_ref[...] = vg[:, :d_r].astype(jnp.bfloat16)
    gr_ref[...] = vg[:, d_r:].astype(jnp.bfloat16)


def _const_spec(shape):
    zeros = (0,) * len(shape)
    return pl.BlockSpec(shape, lambda *_: zeros, pipeline_mode=pl.Buffered(1))


def _projection(x2, g, wqvt, wk, wqkr, wvgr, cos_t, sin_t, batch, seq):
    t, d = x2.shape
    d_a = wk.shape[1]
    d_r = wqkr.shape[1] // 2
    n_heads = d_a // HEAD_DIM
    tm = ROW_TILE
    tpb = seq // tm
    n_blocks = seq // MOBA_BLOCK
    row = lambda w: pl.BlockSpec((tm, w), lambda i: (i, 0))
    bf = jnp.bfloat16
    return pl.pallas_call(
        functools.partial(_proj_kernel, tiles_per_batch=tpb, d_a=d_a, d_r=d_r),
        grid=(t // tm,),
        in_specs=[row(d), _const_spec((1, d)), _const_spec(wqvt.shape), _const_spec(wk.shape),
                  _const_spec(wqkr.shape), _const_spec(wvgr.shape),
                  pl.BlockSpec((tm, PAIR), lambda i: (i % tpb, 0)),
                  pl.BlockSpec((tm, PAIR), lambda i: (i % tpb, 0))],
        out_specs=[pl.BlockSpec((None, d_a, tm), lambda i: (i // tpb, 0, i % tpb)), row(d_a),
                   pl.BlockSpec((None, n_heads, V_ROWS, tm), lambda i: (i // tpb, 0, 0, i % tpb)),
                   pl.BlockSpec((None, n_heads * n_blocks, tm), lambda i: (i // tpb, 0, i % tpb)),
                   row(d_r), row(d_r), row(d_r), row(d_r)],
        out_shape=[jax.ShapeDtypeStruct((batch, d_a, seq), bf), jax.ShapeDtypeStruct((t, d_a), bf),
                   jax.ShapeDtypeStruct((batch, n_heads, V_ROWS, seq), bf),
                   jax.ShapeDtypeStruct((batch, n_heads * n_blocks, seq), jnp.float32),
                   jax.ShapeDtypeStruct((t, d_r), bf), jax.ShapeDtypeStruct((t, d_r), bf),
                   jax.ShapeDtypeStruct((t, d_r), bf), jax.ShapeDtypeStruct((t, d_r), bf)],
        scratch_shapes=[pltpu.VMEM((n_heads * n_blocks, d_a), jnp.float32)],
        compiler_params=pltpu.CompilerParams(
            dimension_semantics=("arbitrary",), vmem_limit_bytes=VMEM_LIMIT),
        name="projection",
    )(x2, g, wqvt, wk, wqkr, wvgr, cos_t, sin_t)


def _moba_kernel(rb_ref, qt_ref, k_ref, vt_ref, gate_ref, bias_ref, o_ref, sel_ref, s_ref, mj_ref,
                 p_ref, *, n_blocks):
    nh = vt_ref.shape[0]
    heads = range(nh)
    group = pl.program_id(1)
    c = pl.program_id(2)

    def pair_cols(h):
        return slice((h // 2) * PAIR, (h // 2 + 1) * PAIR)

    feat = lax.broadcasted_iota(jnp.int32, (PAIR, MOBA_BLOCK), 0)
    qm = []
    for h in heads:
        qt = qt_ref[pair_cols(h), :]
        own_feat = (feat >= HEAD_DIM) if h % 2 else (feat < HEAD_DIM)
        qm.append(jnp.where(own_feat, qt, jnp.zeros_like(qt)))

    slot_prev, slot_own = FAR_RING, FAR_RING + 1
    far_bias = [rb_ref[REL_BUCKETS - 1, nh * group + h] * LOG2E for h in heads]
    prev = jnp.maximum(c - 1, 0)
    n_far = prev
    last = jnp.maximum(n_far - 1, 0)

    def block_rows(j):
        return pl.ds(pl.multiple_of(j * MOBA_BLOCK, MOBA_BLOCK), MOBA_BLOCK)

    def scores(slot, j, bias_index=None):
        for h in heads:
            s = jnp.dot(k_ref[block_rows(j), pair_cols(h)], qm[h],
                        preferred_element_type=jnp.float32)
            if bias_index is not None:
                s = s + bias_ref[h, bias_index]
            s_ref[slot, h] = s
            mj_ref[slot, h] = jnp.max(s, axis=0, keepdims=True)

    def probs(slot):
        return [jnp.exp2((s_ref[slot, h] - mj_ref[slot, h]).astype(jnp.bfloat16))
                for h in heads]

    def weighted(j, p):
        return [jnp.dot(vt_ref[h, :, block_rows(j)], p[h], preferred_element_type=jnp.float32)
                for h in heads]

    def merge(states, mj, oj, j, valid, always=False):
        out = []
        for h in heads:
            m, acc = states[h]
            if always:
                m_new = jnp.maximum(m, mj[h])
                beta = jnp.exp2(mj[h] - m_new)
            else:
                chosen = (sel_ref[h, pl.ds(j, 1), :] > 0.5) & valid
                m_new = jnp.where(chosen, jnp.maximum(m, mj[h]), m)
                beta = jnp.where(chosen, jnp.exp2(mj[h] - m_new), 0.0)
            out.append((m_new, acc * jnp.exp2(m - m_new) + oj[h] * beta))
        return tuple(out)

    scores(slot_own, c, 0)
    scores(slot_prev, prev, 1)
    for i in range(FAR_AHEAD):
        scores(i, jnp.minimum(i, last))

    blk = lax.broadcasted_iota(jnp.int32, (n_blocks, MOBA_BLOCK), 0)
    for h in heads:
        g = jnp.where(blk < c, gate_ref[h * n_blocks:(h + 1) * n_blocks, :], -jnp.inf)
        sel = jnp.zeros(g.shape, jnp.float32)
        for _ in range(MOBA_TOPK):
            top = jnp.max(g, axis=0, keepdims=True)
            idx = jnp.min(jnp.where(g == top, blk, n_blocks), axis=0, keepdims=True)
            pick = (blk == idx) & (blk < c)
            sel = jnp.where(pick, 1.0, sel)
            g = jnp.where(pick, -jnp.inf, g)
        sel_ref[h] = sel

    states = tuple((jnp.full((1, MOBA_BLOCK), NEG, jnp.float32),
                    jnp.zeros((V_ROWS, MOBA_BLOCK), jnp.float32)) for _ in heads)

    p_ref[...] = jnp.zeros_like(p_ref)
    pend_m = tuple(jnp.zeros((1, MOBA_BLOCK), jnp.float32) for _ in heads)

    def trip(unroll, first):
        def body(t, carry):
            states, pend_m, pend_j, pend_valid = carry
            base = first + unroll * t
            js = [jnp.minimum(base + i, last) for i in range(unroll + FAR_AHEAD)]
            pending = (pend_m, weighted(pend_j, [p_ref[h] for h in heads]), pend_j,
                       pend_valid > 0)
            for i in range(unroll):
                scores((i + FAR_AHEAD) % FAR_RING, js[i + FAR_AHEAD])
                p = probs(i % FAR_RING)
                m_i = tuple(mj_ref[i % FAR_RING, h] + far_bias[h] for h in heads)
                if i + 1 < unroll:
                    o_i = weighted(js[i], p)
                else:
                    for h in heads:
                        p_ref[h] = p[h]
                states = merge(states, *pending)
                if i + 1 < unroll:
                    pending = (m_i, o_i, js[i], base + i < n_far)
            return (states, m_i, js[unroll - 1], (base + unroll - 1 < n_far).astype(jnp.int32))
        return body

    carry = (states, pend_m, jnp.int32(0), jnp.int32(0))
    done = 0
    for unroll in FAR_TRIPS:
        left = n_far - done
        n_trips = left // unroll if unroll != FAR_TRIPS[-1] else (left + unroll - 1) // unroll
        carry = lax.fori_loop(0, n_trips, trip(unroll, done), carry)
        done = done + n_trips * unroll
    states, pend_m, pend_j, pend_valid = carry
    o_pend = weighted(pend_j, [p_ref[h] for h in heads])
    o_prev = weighted(prev, probs(slot_prev))
    o_own = weighted(c, probs(slot_own))
    states = merge(states, pend_m, o_pend, pend_j, pend_valid > 0)
    states = merge(states, [mj_ref[slot_prev, h] for h in heads], o_prev, prev, True)
    states = merge(states, [mj_ref[slot_own, h] for h in heads], o_own, c, True, always=True)

    outs = [(acc[:HEAD_DIM] / acc[HEAD_DIM:HEAD_DIM + 1]).T for _, acc in states]
    o_ref[...] = jnp.concatenate(outs, axis=1).astype(o_ref.dtype)


def _moba(rel_bias, qat, ka, vt, gate, bias, batch, seq):
    d_a = ka.shape[1]
    n_blocks = seq // MOBA_BLOCK
    ka3 = ka.reshape(batch, seq, d_a)
    nh = MOBA_HEADS
    width = nh * HEAD_DIM
    out = pl.pallas_call(
        functools.partial(_moba_kernel, n_blocks=n_blocks),
        grid=(batch, d_a // width, n_blocks),
        in_specs=[pl.BlockSpec(memory_space=pltpu.SMEM),
                  pl.BlockSpec((None, width, MOBA_BLOCK), lambda b, g, c: (b, g, c)),
                  pl.BlockSpec((None, seq, width), lambda b, g, c: (b, 0, g)),
                  pl.BlockSpec((None, nh, V_ROWS, seq), lambda b, g, c: (b, g, 0, 0)),
                  pl.BlockSpec((None, nh * n_blocks, MOBA_BLOCK), lambda b, g, c: (b, g, c)),
                  pl.BlockSpec((nh, 2, MOBA_BLOCK, MOBA_BLOCK), lambda b, g, c: (g, 0, 0, 0),
                               pipeline_mode=pl.Buffered(1))],
        out_specs=pl.BlockSpec((None, MOBA_BLOCK, width), lambda b, g, c: (b, c, g)),
        out_shape=jax.ShapeDtypeStruct((batch, seq, d_a), jnp.bfloat16),
        scratch_shapes=[pltpu.VMEM((nh, n_blocks, MOBA_BLOCK), jnp.float32),
                        pltpu.VMEM((FAR_RING + 2, nh, MOBA_BLOCK, MOBA_BLOCK), jnp.float32),
                        pltpu.VMEM((FAR_RING + 2, nh, 1, MOBA_BLOCK), jnp.float32),
                        pltpu.VMEM((nh, MOBA_BLOCK, MOBA_BLOCK), jnp.bfloat16)],
        compiler_params=pltpu.CompilerParams(
            dimension_semantics=("arbitrary", "arbitrary", "arbitrary"),
            vmem_limit_bytes=VMEM_LIMIT),
        name="moba",
    )(rel_bias, qat, ka3, vt, gate, bias)
    return out.reshape(batch * seq, d_a)


def _ret_kernel(q_ref, k_ref, v_ref, g_ref, dmask_ref, dstart_ref, dend_ref, cdec_ref,
                o_ref, state_ref):
    @pl.when(pl.program_id(1) == 0)
    def _():
        state_ref[...] = jnp.zeros_like(state_ref)

    n_pairs = state_ref.shape[0]
    bf = jnp.bfloat16
    f32 = jnp.float32
    cols = [slice(p * PAIR, (p + 1) * PAIR) for p in range(n_pairs)]
    lane = lax.broadcasted_iota(jnp.int32, (RET_CHUNK, PAIR), 1)
    in_head = [lane < HEAD_DIM, lane >= HEAD_DIM]
    r = lax.broadcasted_iota(jnp.int32, (PAIR, PAIR), 0) // HEAD_DIM
    cc = lax.broadcasted_iota(jnp.int32, (PAIR, PAIR), 1) // HEAD_DIM
    same_head = r == cc
    avg = jnp.where(same_head, 1.0 / HEAD_DIM, 0.0).astype(bf)

    q = [q_ref[:, c] for c in cols]
    k = [k_ref[:, c] for c in cols]
    v = [v_ref[:, c] for c in cols]

    sc = [[lax.dot_general(jnp.where(in_head[h], q[p], jnp.zeros_like(q[p])), k[p], NT,
                           preferred_element_type=f32) for h in range(2)]
          for p in range(n_pairs)]
    state = [state_ref[p] for p in range(n_pairs)]
    cross = [jnp.dot(q[p], state[p].astype(bf), preferred_element_type=f32)
             for p in range(n_pairs)]
    kv = [lax.dot_general((k[p].astype(f32) * dend_ref[:, cols[p]]).astype(bf), v[p], TN,
                          preferred_element_type=f32) for p in range(n_pairs)]
    for p in range(n_pairs):
        state_ref[p] = jnp.where(same_head, state[p] * cdec_ref[:, cols[p]] + kv[p], 0.0)

    y = []
    for p in range(n_pairs):
        yh = [jnp.dot((sc[p][h] * dmask_ref[2 * p + h]).astype(bf), v[p],
                      preferred_element_type=f32) for h in range(2)]
        y.append(jnp.where(in_head[0], yh[0], yh[1]) + cross[p] * dstart_ref[:, cols[p]])

    def head_mean(t):
        return jnp.dot(t.astype(bf), avg, preferred_element_type=f32)

    y_hi = [t.astype(bf) for t in y]
    mu = [jnp.dot(y_hi[p], avg, preferred_element_type=f32)
          + head_mean(y[p] - y_hi[p].astype(f32)) for p in range(n_pairs)]
    d = [y[p] - mu[p] for p in range(n_pairs)]
    var = [head_mean(d[p] * d[p]) for p in range(n_pairs)]
    for p in range(n_pairs):
        g = g_ref[:, cols[p]].astype(f32)
        o_ref[:, cols[p]] = (g * jax.nn.sigmoid(g) * d[p] * lax.rsqrt(var[p] + EPS)).astype(bf)


def _retention(qr, kr, vr, gr, dmask, dstart, dend, cdec, batch, seq):
    d_r = qr.shape[1]
    n_chunks = seq // RET_CHUNK
    blk = pl.BlockSpec((None, RET_CHUNK, d_r), lambda b, c: (b, c, 0))
    r3 = lambda a: a.reshape(batch, seq, d_r)
    out = pl.pallas_call(
        _ret_kernel,
        grid=(batch, n_chunks),
        in_specs=[blk, blk, blk, blk, _const_spec(dmask.shape), _const_spec(dstart.shape),
                  _const_spec(dend.shape), _const_spec(cdec.shape)],
        out_specs=blk,
        out_shape=jax.ShapeDtypeStruct((batch, seq, d_r), jnp.bfloat16),
        scratch_shapes=[pltpu.VMEM((d_r // PAIR, PAIR, PAIR), jnp.float32)],
        compiler_params=pltpu.CompilerParams(
            dimension_semantics=("arbitrary", "arbitrary"), vmem_limit_bytes=VMEM_LIMIT),
        name="retention",
    )(r3(qr), r3(kr), r3(vr), r3(gr), dmask, dstart, dend, cdec)
    return out.reshape(batch * seq, d_r)


def _ffn_kernel(ya_ref, yr_ref, x_ref, woa_ref, wor_ref, g1_ref, g2_ref, wup_ref, cw_ref, cb_ref,
                wdn_ref, g3_ref, o_ref, u_ref, f_ref, *, tiles_per_batch, d_ff):
    tm = x_ref.shape[0]

    @pl.when(pl.program_id(0) % tiles_per_batch == 0)
    def _():
        u_ref[0:HALO, :] = jnp.zeros((HALO, u_ref.shape[1]), jnp.float32)

    y = (jnp.dot(ya_ref[...], woa_ref[...], preferred_element_type=jnp.float32)
         + jnp.dot(yr_ref[...], wor_ref[...], preferred_element_type=jnp.float32))
    x1 = x_ref[...] + _rms(y, g1_ref[...])
    h2 = _rms(x1, g2_ref[...]).astype(jnp.bfloat16)

    def chunk_cols(ci):
        lo = ci * FF_CHUNK
        return slice(lo, lo + FF_CHUNK), slice(d_ff + lo, d_ff + lo + FF_CHUNK)

    def up(ci):
        for cols in chunk_cols(ci):
            u_ref[HALO:HALO + tm, cols] = jnp.dot(h2, wup_ref[:, cols],
                                                  preferred_element_type=jnp.float32)

    def conv(cols):
        w = cw_ref[:, cols]
        return (cb_ref[:, cols]
                + w[2:3] * u_ref[HALO:HALO + tm, cols]
                + w[1:2] * u_ref[HALO - 1:HALO - 1 + tm, cols]
                + w[0:1] * u_ref[HALO - 2:HALO - 2 + tm, cols])

    n_chunks = d_ff // FF_CHUNK
    group_ends = [(g + 1) * n_chunks // FF_GROUPS for g in range(FF_GROUPS)]
    acc = None
    start = 0
    up(0)
    for ci in range(n_chunks):
        if ci + 1 < n_chunks:
            up(ci + 1)
        cols_a, cols_b = chunk_cols(ci)
        f_ref[:, cols_a] = (jax.nn.gelu(conv(cols_a), approximate=True)
                            * conv(cols_b)).astype(jnp.bfloat16)
        if ci + 1 in group_ends:
            rows = slice(start * FF_CHUNK, (ci + 1) * FF_CHUNK)
            part = jnp.dot(f_ref[:, rows], wdn_ref[rows, :], preferred_element_type=jnp.float32)
            acc = part if acc is None else acc + part
            start = ci + 1
    u_ref[0:HALO, :] = u_ref[tm:tm + HALO, :]
    o_ref[...] = x1 + _rms(acc, g3_ref[...])


def _out_ffn(ya, yr, x2, woa, wor, g1, g2, wup, cw, cb, wdn, g3, seq):
    t, d = x2.shape
    d_ff = wdn.shape[0]
    tm = ROW_TILE
    row = lambda w: pl.BlockSpec((tm, w), lambda i: (i, 0))
    return pl.pallas_call(
        functools.partial(_ffn_kernel, tiles_per_batch=seq // tm, d_ff=d_ff),
        grid=(t // tm,),
        in_specs=[row(ya.shape[1]), row(yr.shape[1]), row(d),
                  _const_spec(woa.shape), _const_spec(wor.shape),
                  _const_spec((1, d)), _const_spec((1, d)), _const_spec(wup.shape),
                  _const_spec(cw.shape), _const_spec(cb.shape), _const_spec(wdn.shape),
                  _const_spec((1, d))],
        out_specs=row(d),
        out_shape=jax.ShapeDtypeStruct((t, d), jnp.float32),
        scratch_shapes=[pltpu.VMEM((HALO + tm, 2 * d_ff), jnp.float32),
                        pltpu.VMEM((tm, d_ff), jnp.bfloat16)],
        compiler_params=pltpu.CompilerParams(
            dimension_semantics=("arbitrary",), vmem_limit_bytes=VMEM_LIMIT),
        name="out_ffn",
    )(ya, yr, x2, woa, wor, g1, g2, wup, cw, cb, wdn, g3)


def _rotary_tables(seq):
    inv = ROPE_BASE ** (-jnp.arange(0, HEAD_DIM, 2, dtype=jnp.float32) / HEAD_DIM)
    ang = jnp.arange(seq, dtype=jnp.float32)[:, None] * inv[None, :]
    cos, sin = jnp.cos(ang), jnp.sin(ang)
    cos_t = jnp.tile(jnp.concatenate([cos, cos], axis=1), (1, PAIR // HEAD_DIM))
    sin_t = jnp.tile(jnp.concatenate([-sin, sin], axis=1), (1, PAIR // HEAD_DIM))
    return cos_t, sin_t


def _decay_tables(n_heads):
    c = RET_CHUNK
    log_gamma = jnp.log(1.0 - 2.0 ** (-5.0 - jnp.arange(n_heads, dtype=jnp.float32)))
    n = jnp.arange(c, dtype=jnp.float32)
    rel = n[:, None] - n[None, :]
    dmask = jnp.where(rel >= 0, jnp.exp(jnp.maximum(rel, 0.0)[None] * log_gamma[:, None, None]), 0.0)
    dend = jnp.exp((c - 1.0 - n)[:, None] * log_gamma[None, :])
    dstart = jnp.exp((n + 1.0)[:, None] * log_gamma[None, :])
    cdec = jnp.exp(c * log_gamma)[None, :]
    wide = lambda a: jnp.repeat(a, HEAD_DIM, axis=1)
    return dmask, wide(dstart), wide(dend), wide(cdec)


def kernel(x, norm_mix_pre, w_in, rel_bias, w_out, norm_mix_post, norm_ffn_pre, w_up, conv_w,
           conv_b, w_down, norm_ffn_post):
    batch, seq, d = x.shape
    depth = w_in.shape[0]
    d_a = w_out.shape[1] // 2
    d_r = w_out.shape[1] - d_a
    assert seq % ROW_TILE == 0 and ROW_TILE % MOBA_BLOCK == 0 and MOBA_BLOCK == RET_CHUNK
    assert w_in.shape[2] == 3 * d_a + 4 * d_r and d_a % PAIR == 0 and d_r % PAIR == 0
    assert w_down.shape[1] % FF_CHUNK == 0

    bf = jnp.bfloat16
    cos_t, sin_t = _rotary_tables(seq)
    dmask, dstart, dend, cdec = _decay_tables(d_r // HEAD_DIM)
    bias = _bias_tables(rel_bias)

    x2 = x.reshape(batch * seq, d)
    for l in range(depth):
        w = w_in[l].astype(bf)
        wqvt = jnp.concatenate([w[:, :d_a], w[:, 2 * d_a:3 * d_a]], axis=1).T
        wk = w[:, d_a:2 * d_a]
        wqkr = w[:, 3 * d_a:3 * d_a + 2 * d_r]
        wvgr = w[:, 3 * d_a + 2 * d_r:]
        qat, ka, vt, gate, qr, kr, vr, gr = _projection(
            x2, norm_mix_pre[l][None], wqvt, wk, wqkr, wvgr, cos_t, sin_t, batch, seq)
        ya = _moba(rel_bias, qat, ka, vt, gate, bias, batch, seq)
        yr = _retention(qr, kr, vr, gr, dmask, dstart, dend, cdec, batch, seq)
        wo = w_out[l].astype(bf)
        x2 = _out_ffn(ya, yr, x2, wo[:d_a], wo[d_a:], norm_mix_post[l][None],
                      norm_ffn_pre[l][None], w_up[l].astype(bf), conv_w[l], conv_b[l][None],
                      w_down[l].astype(bf), norm_ffn_post[l][None], seq)
    return x2.reshape(batch, seq, d)
```

```python
import functools
import math

import jax
import jax.numpy as jnp
from jax import lax
from jax.experimental import pallas as pl
from jax.experimental.pallas import tpu as pltpu

HEAD_DIM = 64
PAIR = 2 * HEAD_DIM
MOBA_BLOCK = 256
MOBA_TOPK = 3
RET_CHUNK = 256
REL_BUCKETS = 32
REL_MAX_DIST = 128
ROPE_BASE = 10000.0
CONV_WIDTH = 3
EPS = 1e-6
NEG = -1e30
LOG2E = math.log2(math.e)
Q_SCALE = HEAD_DIM ** -0.5 * LOG2E
V_ROWS = HEAD_DIM + 16
ROW_TILE = 512
FF_CHUNK = 256
MOBA_HEADS = 4
FAR_TRIPS = (16, 8, 4)
FAR_AHEAD = 2
FAR_RING = 4
FF_GROUPS = 2
HALO = 8
VMEM_LIMIT = 56 * 1024 * 1024

NT = (((1,), (1,)), ((), ()))
TN = (((0,), (0,)), ((), ()))


def _rms(x, g):
    return x * lax.rsqrt(jnp.mean(x * x, axis=-1, keepdims=True) + EPS) * g


def _bias_kernel(rb_ref, o_ref):
    h = pl.program_id(0)
    key = lax.broadcasted_iota(jnp.int32, (MOBA_BLOCK, MOBA_BLOCK), 0)
    qry = lax.broadcasted_iota(jnp.int32, (MOBA_BLOCK, MOBA_BLOCK), 1)
    max_exact = REL_BUCKETS // 2
    for which in range(2):
        rel = qry - key + which * MOBA_BLOCK
        n = jnp.maximum(rel, 0)
        n_f = jnp.maximum(n, 1).astype(jnp.float32)
        large = max_exact + (jnp.log(n_f / max_exact) / math.log(REL_MAX_DIST / max_exact)
                             * (REL_BUCKETS - max_exact)).astype(jnp.int32)
        large = jnp.minimum(large, REL_BUCKETS - 1)
        bucket = jnp.where(n < max_exact, n, large)
        val = jnp.zeros((MOBA_BLOCK, MOBA_BLOCK), jnp.float32)
        for b in range(REL_BUCKETS):
            val = jnp.where(bucket == b, rb_ref[b, h], val)
        o_ref[0, which] = jnp.where(rel >= 0, val * LOG2E, NEG)


def _bias_tables(rel_bias):
    n_heads = rel_bias.shape[1]
    return pl.pallas_call(
        _bias_kernel,
        grid=(n_heads,),
        in_specs=[pl.BlockSpec(memory_space=pltpu.SMEM)],
        out_specs=pl.BlockSpec((1, 2, MOBA_BLOCK, MOBA_BLOCK), lambda h: (h, 0, 0, 0)),
        out_shape=jax.ShapeDtypeStruct((n_heads, 2, MOBA_BLOCK, MOBA_BLOCK), jnp.float32),
        name="bias_tables",
    )(rel_bias)


def _proj_kernel(x_ref, g_ref, wqvt_ref, wk_ref, wqkr_ref, wvgr_ref, cos_ref, sin_ref,
                 qat_ref, ka_ref, vt_ref, gate_ref, qr_ref, kr_ref, vr_ref, gr_ref,
                 kmt_ref, *, tiles_per_batch, d_a, d_r):
    t_in_b = pl.program_id(0) % tiles_per_batch
    n_heads = d_a // HEAD_DIM
    tm = x_ref.shape[0]
    n_blocks = tiles_per_batch * (tm // MOBA_BLOCK)

    @pl.when(t_in_b == 0)
    def _():
        kmt_ref[...] = jnp.zeros_like(kmt_ref)

    hb = _rms(x_ref[...], g_ref[...]).astype(jnp.bfloat16)

    qvt = lax.dot_general(wqvt_ref[...], hb, NT, preferred_element_type=jnp.float32)
    qt = qvt[:d_a]
    qat_ref[...] = (qt * Q_SCALE).astype(jnp.bfloat16)
    k = jnp.dot(hb, wk_ref[...], preferred_element_type=jnp.float32)
    ka_ref[...] = k.astype(jnp.bfloat16)

    lane_head = lax.broadcasted_iota(jnp.int32, (1, d_a), 1) // HEAD_DIM
    for half in range(tm // MOBA_BLOCK):
        rows = slice(half * MOBA_BLOCK, (half + 1) * MOBA_BLOCK)
        gate_ref[:, rows] = jnp.dot(
            kmt_ref[...].astype(jnp.bfloat16), qt[:, rows].astype(jnp.bfloat16),
            preferred_element_type=jnp.float32)
        k_mean = jnp.mean(k[rows], axis=0, keepdims=True)
        j = t_in_b * (tm // MOBA_BLOCK) + half
        for hh in range(n_heads):
            kmt_ref[pl.ds(hh * n_blocks + j, 1), :] = jnp.where(lane_head == hh, k_mean, 0.0)

    vt = qvt[d_a:].astype(jnp.bfloat16)
    for hh in range(n_heads):
        vt_ref[hh, 0:HEAD_DIM, :] = vt[hh * HEAD_DIM:(hh + 1) * HEAD_DIM, :]
        vt_ref[hh, HEAD_DIM:V_ROWS, :] = jnp.ones((V_ROWS - HEAD_DIM, tm), jnp.bfloat16)

    qkr = jnp.dot(hb, wqkr_ref[...], preferred_element_type=jnp.float32)
    cos = cos_ref[...]
    sin = sin_ref[...]
    first_half = (lax.broadcasted_iota(jnp.int32, (tm, PAIR), 1) % HEAD_DIM) < HEAD_DIM // 2
    for cidx in range(2 * d_r // PAIR):
        xc = qkr[:, cidx * PAIR:(cidx + 1) * PAIR]
        sw = jnp.where(first_half,
                       pltpu.roll(xc, PAIR - HEAD_DIM // 2, axis=1),
                       pltpu.roll(xc, HEAD_DIM // 2, axis=1))
        rc = xc * cos + sw * sin
        if cidx < d_r // PAIR:
            qr_ref[:, cidx * PAIR:(cidx + 1) * PAIR] = rc.astype(jnp.bfloat16)
        else:
            c2 = cidx - d_r // PAIR
            kr_ref[:, c2 * PAIR:(c2 + 1) * PAIR] = (rc * HEAD_DIM ** -0.5).astype(jnp.bfloat16)

    vg = jnp.dot(hb, wvgr_ref[...], preferred_element_type=jnp.float32)
    vr_ref[...] = vg[:, :d_r].astype(jnp.bfloat16)
    gr_ref[...] = vg[:, d_r:].astype(jnp.bfloat16)


def _const_spec(shape):
    zeros = (0,) * len(shape)
    return pl.BlockSpec(shape, lambda *_: zeros, pipeline_mode=pl.Buffered(1))


def _projection(x2, g, wqvt, wk, wqkr, wvgr, cos_t, sin_t, batch, seq):
    t, d = x2.shape
    d_a = wk.shape[1]
    d_r = wqkr.shape[1] // 2
    n_heads = d_a // HEAD_DIM
    tm = ROW_TILE
    tpb = seq // tm
    n_blocks = seq // MOBA_BLOCK
    row = lambda w: pl.BlockSpec((tm, w), lambda i: (i, 0))
    bf = jnp.bfloat16
    return pl.pallas_call(
        functools.partial(_proj_kernel, tiles_per_batch=tpb, d_a=d_a, d_r=d_r),
        grid=(t // tm,),
        in_specs=[row(d), _const_spec((1, d)), _const_spec(wqvt.shape), _const_spec(wk.shape),
                  _const_spec(wqkr.shape), _const_spec(wvgr.shape),
                  pl.BlockSpec((tm, PAIR), lambda i: (i % tpb, 0)),
                  pl.BlockSpec((tm, PAIR), lambda i: (i % tpb, 0))],
        out_specs=[pl.BlockSpec((None, d_a, tm), lambda i: (i // tpb, 0, i % tpb)), row(d_a),
                   pl.BlockSpec((None, n_heads, V_ROWS, tm), lambda i: (i // tpb, 0, 0, i % tpb)),
                   pl.BlockSpec((None, n_heads * n_blocks, tm), lambda i: (i // tpb, 0, i % tpb)),
                   row(d_r), row(d_r), row(d_r), row(d_r)],
        out_shape=[jax.ShapeDtypeStruct((batch, d_a, seq), bf), jax.ShapeDtypeStruct((t, d_a), bf),
                   jax.ShapeDtypeStruct((batch, n_heads, V_ROWS, seq), bf),
                   jax.ShapeDtypeStruct((batch, n_heads * n_blocks, seq), jnp.float32),
                   jax.ShapeDtypeStruct((t, d_r), bf), jax.ShapeDtypeStruct((t, d_r), bf),
                   jax.ShapeDtypeStruct((t, d_r), bf), jax.ShapeDtypeStruct((t, d_r), bf)],
        scratch_shapes=[pltpu.VMEM((n_heads * n_blocks, d_a), jnp.float32)],
        compiler_params=pltpu.CompilerParams(
            dimension_semantics=("arbitrary",), vmem_limit_bytes=VMEM_LIMIT),
        name="projection",
    )(x2, g, wqvt, wk, wqkr, wvgr, cos_t, sin_t)


def _moba_kernel(rb_ref, qt_ref, k_ref, vt_ref, gate_ref, bias_ref, o_ref, sel_ref, s_ref, mj_ref,
                 p_ref, acc_ref, *, n_blocks):
    nh = vt_ref.shape[0]
    heads = range(nh)
    group = pl.program_id(1)
    c = pl.program_id(2)

    def pair_cols(h):
        return slice((h // 2) * PAIR, (h // 2 + 1) * PAIR)

    feat = lax.broadcasted_iota(jnp.int32, (PAIR, MOBA_BLOCK), 0)
    qm = []
    for h in heads:
        qt = qt_ref[pair_cols(h), :]
        own_feat = (feat >= HEAD_DIM) if h % 2 else (feat < HEAD_DIM)
        qm.append(jnp.where(own_feat, qt, jnp.zeros_like(qt)))

    slot_prev, slot_own = FAR_RING, FAR_RING + 1
    far_bias = [rb_ref[REL_BUCKETS - 1, nh * group + h] * LOG2E for h in heads]
    prev = jnp.maximum(c - 1, 0)
    n_far = prev
    last = jnp.maximum(n_far - 1, 0)

    def block_rows(j):
        return pl.ds(pl.multiple_of(j * MOBA_BLOCK, MOBA_BLOCK), MOBA_BLOCK)

    def scores(slot, j, bias_index=None):
        for h in heads:
            s = jnp.dot(k_ref[block_rows(j), pair_cols(h)], qm[h],
                        preferred_element_type=jnp.float32)
            if bias_index is not None:
                s = s + bias_ref[h, bias_index]
            s_ref[slot, h] = s
            mj_ref[slot, h] = jnp.max(s, axis=0, keepdims=True)

    def prob(slot, h):
        return jnp.exp2((s_ref[slot, h] - mj_ref[slot, h]).astype(jnp.bfloat16))

    def weigh(j, h, p):
        return jnp.dot(vt_ref[h, :, block_rows(j)], p, preferred_element_type=jnp.float32)

    def weighted(j, slot):
        return [weigh(j, h, prob(slot, h)) for h in heads]

    def merge(states, mj, oj, j, valid, always=False):
        out = []
        for h in heads:
            m = states[h]
            if always:
                m_new = jnp.maximum(m, mj[h])
                beta = jnp.exp2(mj[h] - m_new)
            else:
                chosen = (sel_ref[h, pl.ds(j, 1), :] > 0.5) & valid
                m_new = jnp.where(chosen, jnp.maximum(m, mj[h]), m)
                beta = jnp.where(chosen, jnp.exp2(mj[h] - m_new), 0.0)
            acc_ref[h] = acc_ref[h] * jnp.exp2(m - m_new) + oj[h] * beta
            out.append(m_new)
        return tuple(out)

    scores(slot_own, c, 0)
    scores(slot_prev, prev, 1)
    for i in range(FAR_AHEAD):
        scores(i, jnp.minimum(i, last))

    blk = lax.broadcasted_iota(jnp.int32, (n_blocks, MOBA_BLOCK), 0)
    for h in heads:
        g = jnp.where(blk < c, gate_ref[h * n_blocks:(h + 1) * n_blocks, :], -jnp.inf)
        sel = jnp.zeros(g.shape, jnp.float32)
        for _ in range(MOBA_TOPK):
            top = jnp.max(g, axis=0, keepdims=True)
            idx = jnp.min(jnp.where(g == top, blk, n_blocks), axis=0, keepdims=True)
            pick = (blk == idx) & (blk < c)
            sel = jnp.where(pick, 1.0, sel)
            g = jnp.where(pick, -jnp.inf, g)
        sel_ref[h] = sel

    states = tuple(jnp.full((1, MOBA_BLOCK), NEG, jnp.float32) for _ in heads)
    acc_ref[...] = jnp.zeros_like(acc_ref)

    p_ref[...] = jnp.zeros_like(p_ref)
    pend_m = tuple(jnp.zeros((1, MOBA_BLOCK), jnp.float32) for _ in heads)

    def trip(unroll, first):
        def body(t, carry):
            states, pend_m, pend_j, pend_valid = carry
            base = first + unroll * t
            js = [jnp.minimum(base + i, last) for i in range(unroll + FAR_AHEAD)]
            pending = (pend_m, [weigh(pend_j, h, p_ref[h]) for h in heads], pend_j,
                       pend_valid > 0)
            for i in range(unroll):
                scores((i + FAR_AHEAD) % FAR_RING, js[i + FAR_AHEAD])
                m_i = tuple(mj_ref[i % FAR_RING, h] + far_bias[h] for h in heads)
                if i + 1 < unroll:
                    o_i = weighted(js[i], i % FAR_RING)
                else:
                    for h in heads:
                        p_ref[h] = prob(i % FAR_RING, h)
                states = merge(states, *pending)
                if i + 1 < unroll:
                    pending = (m_i, o_i, js[i], base + i < n_far)
            return (states, m_i, js[unroll - 1], (base + unroll - 1 < n_far).astype(jnp.int32))
        return body

    carry = (states, pend_m, jnp.int32(0), jnp.int32(0))
    done = 0
    for unroll in FAR_TRIPS:
        left = n_far - done
        n_trips = left // unroll if unroll != FAR_TRIPS[-1] else (left + unroll - 1) // unroll
        carry = lax.fori_loop(0, n_trips, trip(unroll, done), carry)
        done = done + n_trips * unroll
    states, pend_m, pend_j, pend_valid = carry
    o_pend = [weigh(pend_j, h, p_ref[h]) for h in heads]
    o_prev = weighted(prev, slot_prev)
    o_own = weighted(c, slot_own)
    states = merge(states, pend_m, o_pend, pend_j, pend_valid > 0)
    states = merge(states, [mj_ref[slot_prev, h] for h in heads], o_prev, prev, True)
    states = merge(states, [mj_ref[slot_own, h] for h in heads], o_own, c, True, always=True)

    outs = [(acc_ref[h, :HEAD_DIM, :] / acc_ref[h, HEAD_DIM:HEAD_DIM + 1, :]).T for h in heads]
    o_ref[...] = jnp.concatenate(outs, axis=1).astype(o_ref.dtype)


def _moba(rel_bias, qat, ka, vt, gate, bias, batch, seq):
    d_a = ka.shape[1]
    n_blocks = seq // MOBA_BLOCK
    ka3 = ka.reshape(batch, seq, d_a)
    nh = MOBA_HEADS
    width = nh * HEAD_DIM
    out = pl.pallas_call(
        functools.partial(_moba_kernel, n_blocks=n_blocks),
        grid=(batch, d_a // width, n_blocks),
        in_specs=[pl.BlockSpec(memory_space=pltpu.SMEM),
                  pl.BlockSpec((None, width, MOBA_BLOCK), lambda b, g, c: (b, g, c)),
                  pl.BlockSpec((None, seq, width), lambda b, g, c: (b, 0, g)),
                  pl.BlockSpec((None, nh, V_ROWS, seq), lambda b, g, c: (b, g, 0, 0)),
                  pl.BlockSpec((None, nh * n_blocks, MOBA_BLOCK), lambda b, g, c: (b, g, c)),
                  pl.BlockSpec((nh, 2, MOBA_BLOCK, MOBA_BLOCK), lambda b, g, c: (g, 0, 0, 0),
                               pipeline_mode=pl.Buffered(1))],
        out_specs=pl.BlockSpec((None, MOBA_BLOCK, width), lambda b, g, c: (b, c, g)),
        out_shape=jax.ShapeDtypeStruct((batch, seq, d_a), jnp.bfloat16),
        scratch_shapes=[pltpu.VMEM((nh, n_blocks, MOBA_BLOCK), jnp.float32),
                        pltpu.VMEM((FAR_RING + 2, nh, MOBA_BLOCK, MOBA_BLOCK), jnp.float32),
                        pltpu.VMEM((FAR_RING + 2, nh, 1, MOBA_BLOCK), jnp.float32),
                        pltpu.VMEM((nh, MOBA_BLOCK, MOBA_BLOCK), jnp.bfloat16),
                        pltpu.VMEM((nh, V_ROWS, MOBA_BLOCK), jnp.float32)],
        compiler_params=pltpu.CompilerParams(
            dimension_semantics=("arbitrary", "arbitrary", "arbitrary"),
            vmem_limit_bytes=VMEM_LIMIT),
        name="moba",
    )(rel_bias, qat, ka3, vt, gate, bias)
    return out.reshape(batch * seq, d_a)


def _ret_kernel(q_ref, k_ref, v_ref, g_ref, dmask_ref, dstart_ref, dend_ref, cdec_ref,
                o_ref, state_ref):
    @pl.when(pl.program_id(1) == 0)
    def _():
        state_ref[...] = jnp.zeros_like(state_ref)

    n_pairs = state_ref.shape[0]
    bf = jnp.bfloat16
    f32 = jnp.float32
    cols = [slice(p * PAIR, (p + 1) * PAIR) for p in range(n_pairs)]
    lane = lax.broadcasted_iota(jnp.int32, (RET_CHUNK, PAIR), 1)
    in_head = [lane < HEAD_DIM, lane >= HEAD_DIM]
    r = lax.broadcasted_iota(jnp.int32, (PAIR, PAIR), 0) // HEAD_DIM
    cc = lax.broadcasted_iota(jnp.int32, (PAIR, PAIR), 1) // HEAD_DIM
    same_head = r == cc
    avg = jnp.where(same_head, 1.0 / HEAD_DIM, 0.0).astype(bf)

    q = [q_ref[:, c] for c in cols]
    k = [k_ref[:, c] for c in cols]
    v = [v_ref[:, c] for c in cols]

    sc = [[lax.dot_general(jnp.where(in_head[h], q[p], jnp.zeros_like(q[p])), k[p], NT,
                           preferred_element_type=f32) for h in range(2)]
          for p in range(n_pairs)]
    state = [state_ref[p] for p in range(n_pairs)]
    cross = [jnp.dot(q[p], state[p].astype(bf), preferred_element_type=f32)
             for p in range(n_pairs)]
    kv = [lax.dot_general((k[p].astype(f32) * dend_ref[:, cols[p]]).astype(bf), v[p], TN,
                          preferred_element_type=f32) for p in range(n_pairs)]
    for p in range(n_pairs):
        state_ref[p] = jnp.where(same_head, state[p] * cdec_ref[:, cols[p]] + kv[p], 0.0)

    y = []
    for p in range(n_pairs):
        yh = [jnp.dot((sc[p][h] * dmask_ref[2 * p + h]).astype(bf), v[p],
                      preferred_element_type=f32) for h in range(2)]
        y.append(jnp.where(in_head[0], yh[0], yh[1]) + cross[p] * dstart_ref[:, cols[p]])

    def head_mean(t):
        return jnp.dot(t.astype(bf), avg, preferred_element_type=f32)

    y_hi = [t.astype(bf) for t in y]
    mu = [jnp.dot(y_hi[p], avg, preferred_element_type=f32)
          + head_mean(y[p] - y_hi[p].astype(f32)) for p in range(n_pairs)]
    d = [y[p] - mu[p] for p in range(n_pairs)]
    var = [head_mean(d[p] * d[p]) for p in range(n_pairs)]
    for p in range(n_pairs):
        g = g_ref[:, cols[p]].astype(f32)
        o_ref[:, cols[p]] = (g * jax.nn.sigmoid(g) * d[p] * lax.rsqrt(var[p] + EPS)).astype(bf)


def _retention(qr, kr, vr, gr, dmask, dstart, dend, cdec, batch, seq):
    d_r = qr.shape[1]
    n_chunks = seq // RET_CHUNK
    blk = pl.BlockSpec((None, RET_CHUNK, d_r), lambda b, c: (b, c, 0))
    r3 = lambda a: a.reshape(batch, seq, d_r)
    out = pl.pallas_call(
        _ret_kernel,
        grid=(batch, n_chunks),
        in_specs=[blk, blk, blk, blk, _const_spec(dmask.shape), _const_spec(dstart.shape),
                  _const_spec(dend.shape), _const_spec(cdec.shape)],
        out_specs=blk,
        out_shape=jax.ShapeDtypeStruct((batch, seq, d_r), jnp.bfloat16),
        scratch_shapes=[pltpu.VMEM((d_r // PAIR, PAIR, PAIR), jnp.float32)],
        compiler_params=pltpu.CompilerParams(
            dimension_semantics=("arbitrary", "arbitrary"), vmem_limit_bytes=VMEM_LIMIT),
        name="retention",
    )(r3(qr), r3(kr), r3(vr), r3(gr), dmask, dstart, dend, cdec)
    return out.reshape(batch * seq, d_r)


def _ffn_kernel(ya_ref, yr_ref, x_ref, woa_ref, wor_ref, g1_ref, g2_ref, wup_ref, cw_ref, cb_ref,
                wdn_ref, g3_ref, o_ref, u_ref, f_ref, *, tiles_per_batch, d_ff):
    tm = x_ref.shape[0]

    @pl.when(pl.program_id(0) % tiles_per_batch == 0)
    def _():
        u_ref[0:HALO, :] = jnp.zeros((HALO, u_ref.shape[1]), jnp.float32)

    y = (jnp.dot(ya_ref[...], woa_ref[...], preferred_element_type=jnp.float32)
         + jnp.dot(yr_ref[...], wor_ref[...], preferred_element_type=jnp.float32))
    x1 = x_ref[...] + _rms(y, g1_ref[...])
    h2 = _rms(x1, g2_ref[...]).astype(jnp.bfloat16)

    def chunk_cols(ci):
        lo = ci * FF_CHUNK
        return slice(lo, lo + FF_CHUNK), slice(d_ff + lo, d_ff + lo + FF_CHUNK)

    def up(ci):
        for cols in chunk_cols(ci):
            u_ref[HALO:HALO + tm, cols] = jnp.dot(h2, wup_ref[:, cols],
                                                  preferred_element_type=jnp.float32)

    def conv(cols):
        w = cw_ref[:, cols]
        return (cb_ref[:, cols]
                + w[2:3] * u_ref[HALO:HALO + tm, cols]
                + w[1:2] * u_ref[HALO - 1:HALO - 1 + tm, cols]
                + w[0:1] * u_ref[HALO - 2:HALO - 2 + tm, cols])

    n_chunks = d_ff // FF_CHUNK
    group_ends = [(g + 1) * n_chunks // FF_GROUPS for g in range(FF_GROUPS)]
    acc = None
    start = 0
    up(0)
    for ci in range(n_chunks):
        if ci + 1 < n_chunks:
            up(ci + 1)
        cols_a, cols_b = chunk_cols(ci)
        f_ref[:, cols_a] = (jax.nn.gelu(conv(cols_a), approximate=True)
                            * conv(cols_b)).astype(jnp.bfloat16)
        if ci + 1 in group_ends:
            rows = slice(start * FF_CHUNK, (ci + 1) * FF_CHUNK)
            part = jnp.dot(f_ref[:, rows], wdn_ref[rows, :], preferred_element_type=jnp.float32)
            acc = part if acc is None else acc + part
            start = ci + 1
    u_ref[0:HALO, :] = u_ref[tm:tm + HALO, :]
    o_ref[...] = x1 + _rms(acc, g3_ref[...])


def _out_ffn(ya, yr, x2, woa, wor, g1, g2, wup, cw, cb, wdn, g3, seq):
    t, d = x2.shape
    d_ff = wdn.shape[0]
    tm = ROW_TILE
    row = lambda w: pl.BlockSpec((tm, w), lambda i: (i, 0))
    return pl.pallas_call(
        functools.partial(_ffn_kernel, tiles_per_batch=seq // tm, d_ff=d_ff),
        grid=(t // tm,),
        in_specs=[row(ya.shape[1]), row(yr.shape[1]), row(d),
                  _const_spec(woa.shape), _const_spec(wor.shape),
                  _const_spec((1, d)), _const_spec((1, d)), _const_spec(wup.shape),
                  _const_spec(cw.shape), _const_spec(cb.shape), _const_spec(wdn.shape),
                  _const_spec((1, d))],
        out_specs=row(d),
        out_shape=jax.ShapeDtypeStruct((t, d), jnp.float32),
        scratch_shapes=[pltpu.VMEM((HALO + tm, 2 * d_ff), jnp.float32),
                        pltpu.VMEM((tm, d_ff), jnp.bfloat16)],
        compiler_params=pltpu.CompilerParams(
            dimension_semantics=("arbitrary",), vmem_limit_bytes=VMEM_LIMIT),
        name="out_ffn",
    )(ya, yr, x2, woa, wor, g1, g2, wup, cw, cb, wdn, g3)


def _rotary_tables(seq):
    inv = ROPE_BASE ** (-jnp.arange(0, HEAD_DIM, 2, dtype=jnp.float32) / HEAD_DIM)
    ang = jnp.arange(seq, dtype=jnp.float32)[:, None] * inv[None, :]
    cos, sin = jnp.cos(ang), jnp.sin(ang)
    cos_t = jnp.tile(jnp.concatenate([cos, cos], axis=1), (1, PAIR // HEAD_DIM))
    sin_t = jnp.tile(jnp.concatenate([-sin, sin], axis=1), (1, PAIR // HEAD_DIM))
    return cos_t, sin_t


def _decay_tables(n_heads):
    c = RET_CHUNK
    log_gamma = jnp.log(1.0 - 2.0 ** (-5.0 - jnp.arange(n_heads, dtype=jnp.float32)))
    n = jnp.arange(c, dtype=jnp.float32)
    rel = n[:, None] - n[None, :]
    dmask = jnp.where(rel >= 0, jnp.exp(jnp.maximum(rel, 0.0)[None] * log_gamma[:, None, None]), 0.0)
    dend = jnp.exp((c - 1.0 - n)[:, None] * log_gamma[None, :])
    dstart = jnp.exp((n + 1.0)[:, None] * log_gamma[None, :])
    cdec = jnp.exp(c * log_gamma)[None, :]
    wide = lambda a: jnp.repeat(a, HEAD_DIM, axis=1)
    return dmask, wide(dstart), wide(dend), wide(cdec)


def kernel(x, norm_mix_pre, w_in, rel_bias, w_out, norm_mix_post, norm_ffn_pre, w_up, conv_w,
           conv_b, w_down, norm_ffn_post):
    batch, seq, d = x.shape
    depth = w_in.shape[0]
    d_a = w_out.shape[1] // 2
    d_r = w_out.shape[1] - d_a
    assert seq % ROW_TILE == 0 and ROW_TILE % MOBA_BLOCK == 0 and MOBA_BLOCK == RET_CHUNK
    assert w_in.shape[2] == 3 * d_a + 4 * d_r and d_a % PAIR == 0 and d_r % PAIR == 0
    assert w_down.shape[1] % FF_CHUNK == 0

    bf = jnp.bfloat16
    cos_t, sin_t = _rotary_tables(seq)
    dmask, dstart, dend, cdec = _decay_tables(d_r // HEAD_DIM)
    bias = _bias_tables(rel_bias)

    x2 = x.reshape(batch * seq, d)
    for l in range(depth):
        w = w_in[l].astype(bf)
        wqvt = jnp.concatenate([w[:, :d_a], w[:, 2 * d_a:3 * d_a]], axis=1).T
        wk = w[:, d_a:2 * d_a]
        wqkr = w[:, 3 * d_a:3 * d_a + 2 * d_r]
        wvgr = w[:, 3 * d_a + 2 * d_r:]
        qat, ka, vt, gate, qr, kr, vr, gr = _projection(
            x2, norm_mix_pre[l][None], wqvt, wk, wqkr, wvgr, cos_t, sin_t, batch, seq)
        ya = _moba(rel_bias, qat, ka, vt, gate, bias, batch, seq)
        yr = _retention(qr, kr, vr, gr, dmask, dstart, dend, cdec, batch, seq)
        wo = w_out[l].astype(bf)
        x2 = _out_ffn(ya, yr, x2, wo[:d_a], wo[d_a:], norm_mix_post[l][None],
                      norm_ffn_pre[l][None], w_up[l].astype(bf), conv_w[l], conv_b[l][None],
                      w_down[l].astype(bf), norm_ffn_post[l][None], seq)
    return x2.reshape(batch, seq, d)
```

```python
import functools
import math

import jax
import jax.numpy as jnp
from jax import lax
from jax.experimental import pallas as pl
from jax.experimental.pallas import tpu as pltpu

HEAD_DIM = 64
PAIR = 2 * HEAD_DIM
MOBA_BLOCK = 256
MOBA_TOPK = 3
RET_CHUNK = 256
REL_BUCKETS = 32
REL_MAX_DIST = 128
ROPE_BASE = 10000.0
CONV_WIDTH = 3
EPS = 1e-6
NEG = -1e30
LOG2E = math.log2(math.e)
Q_SCALE = HEAD_DIM ** -0.5 * LOG2E
V_ROWS = HEAD_DIM + 16
ROW_TILE = 512
MOBA_HEADS = 4
FAR_TRIPS = (16, 8, 4)
FAR_AHEAD = 2
FAR_RING = 4
FF_CHUNK = 256
FF_GROUPS = 2
HALO = 8
VMEM_LIMIT = 56 * 1024 * 1024

NT = (((1,), (1,)), ((), ()))
TN = (((0,), (0,)), ((), ()))


def _rms(x, g):
    return x * lax.rsqrt(jnp.mean(x * x, axis=-1, keepdims=True) + EPS) * g


def _bias_kernel(rb_ref, o_ref):
    h = pl.program_id(0)
    key = lax.broadcasted_iota(jnp.int32, (MOBA_BLOCK, MOBA_BLOCK), 0)
    qry = lax.broadcasted_iota(jnp.int32, (MOBA_BLOCK, MOBA_BLOCK), 1)
    max_exact = REL_BUCKETS // 2
    for which in range(2):
        rel = qry - key + which * MOBA_BLOCK
        n = jnp.maximum(rel, 0)
        n_f = jnp.maximum(n, 1).astype(jnp.float32)
        large = max_exact + (jnp.log(n_f / max_exact) / math.log(REL_MAX_DIST / max_exact)
                             * (REL_BUCKETS - max_exact)).astype(jnp.int32)
        large = jnp.minimum(large, REL_BUCKETS - 1)
        bucket = jnp.where(n < max_exact, n, large)
        val = jnp.zeros((MOBA_BLOCK, MOBA_BLOCK), jnp.float32)
        for b in range(REL_BUCKETS):
            val = jnp.where(bucket == b, rb_ref[b, h], val)
        o_ref[0, which] = jnp.where(rel >= 0, val * LOG2E, NEG)


def _bias_tables(rel_bias):
    n_heads = rel_bias.shape[1]
    return pl.pallas_call(
        _bias_kernel,
        grid=(n_heads,),
        in_specs=[pl.BlockSpec(memory_space=pltpu.SMEM)],
        out_specs=pl.BlockSpec((1, 2, MOBA_BLOCK, MOBA_BLOCK), lambda h: (h, 0, 0, 0)),
        out_shape=jax.ShapeDtypeStruct((n_heads, 2, MOBA_BLOCK, MOBA_BLOCK), jnp.float32),
        name="bias_tables",
    )(rel_bias)


def _proj_kernel(x_ref, g_ref, wqvt_ref, wk_ref, wqkr_ref, wvgr_ref, cos_ref, sin_ref,
                 qat_ref, ka_ref, vt_ref, gate_ref, qr_ref, kr_ref, vr_ref, gr_ref,
                 kmt_ref, *, tiles_per_batch, d_a, d_r):
    t_in_b = pl.program_id(0) % tiles_per_batch
    n_heads = d_a // HEAD_DIM
    tm = x_ref.shape[0]
    n_blocks = tiles_per_batch * (tm // MOBA_BLOCK)

    @pl.when(t_in_b == 0)
    def _():
        kmt_ref[...] = jnp.zeros_like(kmt_ref)

    hb = _rms(x_ref[...], g_ref[...]).astype(jnp.bfloat16)

    qvt = lax.dot_general(wqvt_ref[...], hb, NT, preferred_element_type=jnp.float32)
    qt = qvt[:d_a]
    qat_ref[...] = (qt * Q_SCALE).astype(jnp.bfloat16)
    k = jnp.dot(hb, wk_ref[...], preferred_element_type=jnp.float32)
    ka_ref[...] = k.astype(jnp.bfloat16)

    lane_head = lax.broadcasted_iota(jnp.int32, (1, d_a), 1) // HEAD_DIM
    for half in range(tm // MOBA_BLOCK):
        rows = slice(half * MOBA_BLOCK, (half + 1) * MOBA_BLOCK)
        gate_ref[:, rows] = jnp.dot(
            kmt_ref[...].astype(jnp.bfloat16), qt[:, rows].astype(jnp.bfloat16),
            preferred_element_type=jnp.float32)
        k_mean = jnp.mean(k[rows], axis=0, keepdims=True)
        j = t_in_b * (tm // MOBA_BLOCK) + half
        for hh in range(n_heads):
            kmt_ref[pl.ds(hh * n_blocks + j, 1), :] = jnp.where(lane_head == hh, k_mean, 0.0)

    vt = qvt[d_a:].astype(jnp.bfloat16)
    for hh in range(n_heads):
        vt_ref[hh, 0:HEAD_DIM, :] = vt[hh * HEAD_DIM:(hh + 1) * HEAD_DIM, :]
        vt_ref[hh, HEAD_DIM:V_ROWS, :] = jnp.ones((V_ROWS - HEAD_DIM, tm), jnp.bfloat16)

    qkr = jnp.dot(hb, wqkr_ref[...], preferred_element_type=jnp.float32)
    cos = cos_ref[...]
    sin = sin_ref[...]
    first_half = (lax.broadcasted_iota(jnp.int32, (tm, PAIR), 1) % HEAD_DIM) < HEAD_DIM // 2
    for cidx in range(2 * d_r // PAIR):
        xc = qkr[:, cidx * PAIR:(cidx + 1) * PAIR]
        sw = jnp.where(first_half,
                       pltpu.roll(xc, PAIR - HEAD_DIM // 2, axis=1),
                       pltpu.roll(xc, HEAD_DIM // 2, axis=1))
        rc = xc * cos + sw * sin
        if cidx < d_r // PAIR:
            qr_ref[:, cidx * PAIR:(cidx + 1) * PAIR] = rc.astype(jnp.bfloat16)
        else:
            c2 = cidx - d_r // PAIR
            kr_ref[:, c2 * PAIR:(c2 + 1) * PAIR] = (rc * HEAD_DIM ** -0.5).astype(jnp.bfloat16)

    vg = jnp.dot(hb, wvgr_ref[...], preferred_element_type=jnp.float32)
    vr_ref[...] = vg[:, :d_r].astype(jnp.bfloat16)
    gr_ref[...] = vg[:, d_r:].astype(jnp.bfloat16)


def _const_spec(shape):
    zeros = (0,) * len(shape)
    return pl.BlockSpec(shape, lambda *_: zeros, pipeline_mode=pl.Buffered(1))


def _projection(x2, g, wqvt, wk, wqkr, wvgr, cos_t, sin_t, batch, seq):
    t, d = x2.shape
    d_a = wk.shape[1]
    d_r = wqkr.shape[1] // 2
    n_heads = d_a // HEAD_DIM
    tm = ROW_TILE
    tpb = seq // tm
    n_blocks = seq // MOBA_BLOCK
    row = lambda w: pl.BlockSpec((tm, w), lambda i: (i, 0))
    bf = jnp.bfloat16
    return pl.pallas_call(
        functools.partial(_proj_kernel, tiles_per_batch=tpb, d_a=d_a, d_r=d_r),
        grid=(t // tm,),
        in_specs=[row(d), _const_spec((1, d)), _const_spec(wqvt.shape), _const_spec(wk.shape),
                  _const_spec(wqkr.shape), _const_spec(wvgr.shape),
                  pl.BlockSpec((tm, PAIR), lambda i: (i % tpb, 0)),
                  pl.BlockSpec((tm, PAIR), lambda i: (i % tpb, 0))],
        out_specs=[pl.BlockSpec((None, d_a, tm), lambda i: (i // tpb, 0, i % tpb)), row(d_a),
                   pl.BlockSpec((None, n_heads, V_ROWS, tm), lambda i: (i // tpb, 0, 0, i % tpb)),
                   pl.BlockSpec((None, n_heads * n_blocks, tm), lambda i: (i // tpb, 0, i % tpb)),
                   row(d_r), row(d_r), row(d_r), row(d_r)],
        out_shape=[jax.ShapeDtypeStruct((batch, d_a, seq), bf), jax.ShapeDtypeStruct((t, d_a), bf),
                   jax.ShapeDtypeStruct((batch, n_heads, V_ROWS, seq), bf),
                   jax.ShapeDtypeStruct((batch, n_heads * n_blocks, seq), jnp.float32),
                   jax.ShapeDtypeStruct((t, d_r), bf), jax.ShapeDtypeStruct((t, d_r), bf),
                   jax.ShapeDtypeStruct((t, d_r), bf), jax.ShapeDtypeStruct((t, d_r), bf)],
        scratch_shapes=[pltpu.VMEM((n_heads * n_blocks, d_a), jnp.float32)],
        compiler_params=pltpu.CompilerParams(
            dimension_semantics=("arbitrary",), vmem_limit_bytes=VMEM_LIMIT),
        name="projection",
    )(x2, g, wqvt, wk, wqkr, wvgr, cos_t, sin_t)


def _moba_kernel(rb_ref, qt_ref, k_ref, vt_ref, gate_ref, bias_ref, o_ref, sel_ref, s_ref, mj_ref,
                 p_ref, acc_ref, *, n_blocks):
    nh = vt_ref.shape[0]
    heads = range(nh)
    group = pl.program_id(1)
    c = pl.program_id(2)

    def pair_cols(h):
        return slice((h // 2) * PAIR, (h // 2 + 1) * PAIR)

    feat = lax.broadcasted_iota(jnp.int32, (PAIR, MOBA_BLOCK), 0)
    qm = []
    for h in heads:
        qt = qt_ref[pair_cols(h), :]
        own_feat = (feat >= HEAD_DIM) if h % 2 else (feat < HEAD_DIM)
        qm.append(jnp.where(own_feat, qt, jnp.zeros_like(qt)))

    slot_prev, slot_own = FAR_RING, FAR_RING + 1
    far_bias = [rb_ref[REL_BUCKETS - 1, nh * group + h] * LOG2E for h in heads]
    prev = jnp.maximum(c - 1, 0)
    n_far = prev
    last = jnp.maximum(n_far - 1, 0)

    def block_rows(j):
        return pl.ds(pl.multiple_of(j * MOBA_BLOCK, MOBA_BLOCK), MOBA_BLOCK)

    def scores(slot, j, bias_index=None):
        for h in heads:
            s = jnp.dot(k_ref[block_rows(j), pair_cols(h)], qm[h],
                        preferred_element_type=jnp.float32)
            if bias_index is not None:
                s = s + bias_ref[h, bias_index]
            s_ref[slot, h] = s
            mj_ref[slot, h] = jnp.max(s, axis=0, keepdims=True)

    def prob(slot, h):
        return jnp.exp2((s_ref[slot, h] - mj_ref[slot, h]).astype(jnp.bfloat16))

    def weigh(j, h, p):
        return jnp.dot(vt_ref[h, :, block_rows(j)], p, preferred_element_type=jnp.float32)

    def weighted(j, slot):
        return [weigh(j, h, prob(slot, h)) for h in heads]

    def merge(states, mj, oj, j, valid, always=False):
        out = []
        for h in heads:
            m = states[h]
            if always:
                m_new = jnp.maximum(m, mj[h])
                beta = jnp.exp2(mj[h] - m_new)
            else:
                chosen = (sel_ref[h, pl.ds(j, 1), :] > 0.5) & valid
                m_new = jnp.where(chosen, jnp.maximum(m, mj[h]), m)
                beta = jnp.where(chosen, jnp.exp2(mj[h] - m_new), 0.0)
            acc_ref[h] = acc_ref[h] * jnp.exp2(m - m_new) + oj[h] * beta
            out.append(m_new)
        return tuple(out)

    scores(slot_own, c, 0)
    scores(slot_prev, prev, 1)
    for i in range(FAR_AHEAD):
        scores(i, jnp.minimum(i, last))

    blk = lax.broadcasted_iota(jnp.int32, (n_blocks, MOBA_BLOCK), 0)
    for h in heads:
        g = jnp.where(blk < c, gate_ref[h * n_blocks:(h + 1) * n_blocks, :], -jnp.inf)
        sel = jnp.zeros(g.shape, jnp.float32)
        for _ in range(MOBA_TOPK):
            top = jnp.max(g, axis=0, keepdims=True)
            idx = jnp.min(jnp.where(g == top, blk, n_blocks), axis=0, keepdims=True)
            pick = (blk == idx) & (blk < c)
            sel = jnp.where(pick, 1.0, sel)
            g = jnp.where(pick, -jnp.inf, g)
        sel_ref[h] = sel

    states = tuple(jnp.full((1, MOBA_BLOCK), NEG, jnp.float32) for _ in heads)
    acc_ref[...] = jnp.zeros_like(acc_ref)

    p_ref[...] = jnp.zeros_like(p_ref)
    pend_m = tuple(jnp.zeros((1, MOBA_BLOCK), jnp.float32) for _ in heads)

    def trip(unroll, first):
        def body(t, carry):
            states, pend_m, pend_j, pend_valid = carry
            base = first + unroll * t
            js = [jnp.minimum(base + i, last) for i in range(unroll + FAR_AHEAD)]
            pending = (pend_m, [weigh(pend_j, h, p_ref[h]) for h in heads], pend_j,
                       pend_valid > 0)
            for i in range(unroll):
                scores((i + FAR_AHEAD) % FAR_RING, js[i + FAR_AHEAD])
                m_i = tuple(mj_ref[i % FAR_RING, h] + far_bias[h] for h in heads)
                if i + 1 < unroll:
                    o_i = weighted(js[i], i % FAR_RING)
                else:
                    for h in heads:
                        p_ref[h] = prob(i % FAR_RING, h)
                states = merge(states, *pending)
                if i + 1 < unroll:
                    pending = (m_i, o_i, js[i], base + i < n_far)
            return (states, m_i, js[unroll - 1], (base + unroll - 1 < n_far).astype(jnp.int32))
        return body

    carry = (states, pend_m, jnp.int32(0), jnp.int32(0))
    done = 0
    for unroll in FAR_TRIPS:
        left = n_far - done
        n_trips = left // unroll if unroll != FAR_TRIPS[-1] else (left + unroll - 1) // unroll
        carry = lax.fori_loop(0, n_trips, trip(unroll, done), carry)
        done = done + n_trips * unroll
    states, pend_m, pend_j, pend_valid = carry
    o_pend = [weigh(pend_j, h, p_ref[h]) for h in heads]
    o_prev = weighted(prev, slot_prev)
    o_own = weighted(c, slot_own)
    states = merge(states, pend_m, o_pend, pend_j, pend_valid > 0)
    states = merge(states, [mj_ref[slot_prev, h] for h in heads], o_prev, prev, True)
    states = merge(states, [mj_ref[slot_own, h] for h in heads], o_own, c, True, always=True)

    outs = [(acc_ref[h, :HEAD_DIM, :] / acc_ref[h, HEAD_DIM:HEAD_DIM + 1, :]).T for h in heads]
    o_ref[...] = jnp.concatenate(outs, axis=1).astype(o_ref.dtype)


def _moba(rel_bias, qat, ka, vt, gate, bias, batch, seq):
    d_a = ka.shape[1]
    n_blocks = seq // MOBA_BLOCK
    ka3 = ka.reshape(batch, seq, d_a)
    nh = MOBA_HEADS
    width = nh * HEAD_DIM
    out = pl.pallas_call(
        functools.partial(_moba_kernel, n_blocks=n_blocks),
        grid=(batch, d_a // width, n_blocks),
        in_specs=[pl.BlockSpec(memory_space=pltpu.SMEM),
                  pl.BlockSpec((None, width, MOBA_BLOCK), lambda b, g, c: (b, g, c)),
                  pl.BlockSpec((None, seq, width), lambda b, g, c: (b, 0, g)),
                  pl.BlockSpec((None, nh, V_ROWS, seq), lambda b, g, c: (b, g, 0, 0)),
                  pl.BlockSpec((None, nh * n_blocks, MOBA_BLOCK), lambda b, g, c: (b, g, c)),
                  pl.BlockSpec((nh, 2, MOBA_BLOCK, MOBA_BLOCK), lambda b, g, c: (g, 0, 0, 0),
                               pipeline_mode=pl.Buffered(1))],
        out_specs=pl.BlockSpec((None, MOBA_BLOCK, width), lambda b, g, c: (b, c, g)),
        out_shape=jax.ShapeDtypeStruct((batch, seq, d_a), jnp.bfloat16),
        scratch_shapes=[pltpu.VMEM((nh, n_blocks, MOBA_BLOCK), jnp.float32),
                        pltpu.VMEM((FAR_RING + 2, nh, MOBA_BLOCK, MOBA_BLOCK), jnp.float32),
                        pltpu.VMEM((FAR_RING + 2, nh, 1, MOBA_BLOCK), jnp.float32),
                        pltpu.VMEM((nh, MOBA_BLOCK, MOBA_BLOCK), jnp.bfloat16),
                        pltpu.VMEM((nh, V_ROWS, MOBA_BLOCK), jnp.float32)],
        compiler_params=pltpu.CompilerParams(
            dimension_semantics=("arbitrary", "arbitrary", "arbitrary"),
            vmem_limit_bytes=VMEM_LIMIT),
        name="moba",
    )(rel_bias, qat, ka3, vt, gate, bias)
    return out.reshape(batch * seq, d_a)


def _ret_kernel(q_ref, k_ref, v_ref, g_ref, dmask_ref, dstart_ref, dend_ref, cdec_ref,
                o_ref, state_ref):
    @pl.when(pl.program_id(1) == 0)
    def _():
        state_ref[...] = jnp.zeros_like(state_ref)

    n_pairs = state_ref.shape[0]
    bf = jnp.bfloat16
    f32 = jnp.float32
    cols = [slice(p * PAIR, (p + 1) * PAIR) for p in range(n_pairs)]
    lane = lax.broadcasted_iota(jnp.int32, (RET_CHUNK, PAIR), 1)
    in_head = [lane < HEAD_DIM, lane >= HEAD_DIM]
    r = lax.broadcasted_iota(jnp.int32, (PAIR, PAIR), 0) // HEAD_DIM
    cc = lax.broadcasted_iota(jnp.int32, (PAIR, PAIR), 1) // HEAD_DIM
    same_head = r == cc
    avg = jnp.where(same_head, 1.0 / HEAD_DIM, 0.0).astype(bf)

    q = [q_ref[:, c] for c in cols]
    k = [k_ref[:, c] for c in cols]
    v = [v_ref[:, c] for c in cols]

    sc = [[lax.dot_general(jnp.where(in_head[h], q[p], jnp.zeros_like(q[p])), k[p], NT,
                           preferred_element_type=f32) for h in range(2)]
          for p in range(n_pairs)]
    state = [state_ref[p] for p in range(n_pairs)]
    cross = [jnp.dot(q[p], state[p].astype(bf), preferred_element_type=f32)
             for p in range(n_pairs)]
    kv = [lax.dot_general((k[p].astype(f32) * dend_ref[:, cols[p]]).astype(bf), v[p], TN,
                          preferred_element_type=f32) for p in range(n_pairs)]
    for p in range(n_pairs):
        state_ref[p] = jnp.where(same_head, state[p] * cdec_ref[:, cols[p]] + kv[p], 0.0)

    y = []
    for p in range(n_pairs):
        yh = [jnp.dot((sc[p][h] * dmask_ref[2 * p + h]).astype(bf), v[p],
                      preferred_element_type=f32) for h in range(2)]
        y.append(jnp.where(in_head[0], yh[0], yh[1]) + cross[p] * dstart_ref[:, cols[p]])

    def head_mean(t):
        return jnp.dot(t.astype(bf), avg, preferred_element_type=f32)

    y_hi = [t.astype(bf) for t in y]
    mu = [jnp.dot(y_hi[p], avg, preferred_element_type=f32)
          + head_mean(y[p] - y_hi[p].astype(f32)) for p in range(n_pairs)]
    d = [y[p] - mu[p] for p in range(n_pairs)]
    var = [head_mean(d[p] * d[p]) for p in range(n_pairs)]
    for p in range(n_pairs):
        g = g_ref[:, cols[p]].astype(f32)
        o_ref[:, cols[p]] = (g * jax.nn.sigmoid(g) * d[p] * lax.rsqrt(var[p] + EPS)).astype(bf)


def _retention(qr, kr, vr, gr, dmask, dstart, dend, cdec, batch, seq):
    d_r = qr.shape[1]
    n_chunks = seq // RET_CHUNK
    blk = pl.BlockSpec((None, RET_CHUNK, d_r), lambda b, c: (b, c, 0))
    r3 = lambda a: a.reshape(batch, seq, d_r)
    out = pl.pallas_call(
        _ret_kernel,
        grid=(batch, n_chunks),
        in_specs=[blk, blk, blk, blk, _const_spec(dmask.shape), _const_spec(dstart.shape),
                  _const_spec(dend.shape), _const_spec(cdec.shape)],
        out_specs=blk,
        out_shape=jax.ShapeDtypeStruct((batch, seq, d_r), jnp.bfloat16),
        scratch_shapes=[pltpu.VMEM((d_r // PAIR, PAIR, PAIR), jnp.float32)],
        compiler_params=pltpu.CompilerParams(
            dimension_semantics=("arbitrary", "arbitrary"), vmem_limit_bytes=VMEM_LIMIT),
        name="retention",
    )(r3(qr), r3(kr), r3(vr), r3(gr), dmask, dstart, dend, cdec)
    return out.reshape(batch * seq, d_r)


def _ffn_kernel(ya_ref, yr_ref, x_ref, woa_ref, wor_ref, g1_ref, g2_ref, wup_ref, cw_ref, cb_ref,
                wdn_ref, g3_ref, o_ref, u_ref, f_ref, *, tiles_per_batch, d_ff):
    tm = x_ref.shape[0]

    @pl.when(pl.program_id(0) % tiles_per_batch == 0)
    def _():
        u_ref[0:HALO, :] = jnp.zeros((HALO, u_ref.shape[1]), jnp.float32)

    y = (jnp.dot(ya_ref[...], woa_ref[...], preferred_element_type=jnp.float32)
         + jnp.dot(yr_ref[...], wor_ref[...], preferred_element_type=jnp.float32))
    x1 = x_ref[...] + _rms(y, g1_ref[...])
    h2 = _rms(x1, g2_ref[...]).astype(jnp.bfloat16)

    def chunk_cols(ci):
        lo = ci * FF_CHUNK
        return slice(lo, lo + FF_CHUNK), slice(d_ff + lo, d_ff + lo + FF_CHUNK)

    def up(ci):
        for cols in chunk_cols(ci):
            u_ref[HALO:HALO + tm, cols] = jnp.dot(h2, wup_ref[:, cols],
                                                  preferred_element_type=jnp.float32)

    def conv(cols):
        w = cw_ref[:, cols]
        out = cb_ref[:, cols]
        for j in range(CONV_WIDTH):
            lo = HALO - (CONV_WIDTH - 1 - j)
            out = out + w[j:j + 1] * u_ref[lo:lo + tm, cols]
        return out

    n_chunks = d_ff // FF_CHUNK
    group_ends = [(g + 1) * n_chunks // FF_GROUPS for g in range(FF_GROUPS)]
    acc = None
    start = 0
    up(0)
    for ci in range(n_chunks):
        if ci + 1 < n_chunks:
            up(ci + 1)
        cols_a, cols_b = chunk_cols(ci)
        f_ref[:, cols_a] = (jax.nn.gelu(conv(cols_a), approximate=True)
                            * conv(cols_b)).astype(jnp.bfloat16)
        if ci + 1 in group_ends:
            rows = slice(start * FF_CHUNK, (ci + 1) * FF_CHUNK)
            part = jnp.dot(f_ref[:, rows], wdn_ref[rows, :], preferred_element_type=jnp.float32)
            acc = part if acc is None else acc + part
            start = ci + 1
    u_ref[0:HALO, :] = u_ref[tm:tm + HALO, :]
    o_ref[...] = x1 + _rms(acc, g3_ref[...])


def _out_ffn(ya, yr, x2, woa, wor, g1, g2, wup, cw, cb, wdn, g3, seq):
    t, d = x2.shape
    d_ff = wdn.shape[0]
    tm = ROW_TILE
    row = lambda w: pl.BlockSpec((tm, w), lambda i: (i, 0))
    return pl.pallas_call(
        functools.partial(_ffn_kernel, tiles_per_batch=seq // tm, d_ff=d_ff),
        grid=(t // tm,),
        in_specs=[row(ya.shape[1]), row(yr.shape[1]), row(d),
                  _const_spec(woa.shape), _const_spec(wor.shape),
                  _const_spec((1, d)), _const_spec((1, d)), _const_spec(wup.shape),
                  _const_spec(cw.shape), _const_spec(cb.shape), _const_spec(wdn.shape),
                  _const_spec((1, d))],
        out_specs=row(d),
        out_shape=jax.ShapeDtypeStruct((t, d), jnp.float32),
        scratch_shapes=[pltpu.VMEM((HALO + tm, 2 * d_ff), jnp.float32),
                        pltpu.VMEM((tm, d_ff), jnp.bfloat16)],
        compiler_params=pltpu.CompilerParams(
            dimension_semantics=("arbitrary",), vmem_limit_bytes=VMEM_LIMIT),
        name="out_ffn",
    )(ya, yr, x2, woa, wor, g1, g2, wup, cw, cb, wdn, g3)


def _rotary_tables(seq):
    inv = ROPE_BASE ** (-jnp.arange(0, HEAD_DIM, 2, dtype=jnp.float32) / HEAD_DIM)
    ang = jnp.arange(seq, dtype=jnp.float32)[:, None] * inv[None, :]
    cos, sin = jnp.cos(ang), jnp.sin(ang)
    cos_t = jnp.tile(jnp.concatenate([cos, cos], axis=1), (1, PAIR // HEAD_DIM))
    sin_t = jnp.tile(jnp.concatenate([-sin, sin], axis=1), (1, PAIR // HEAD_DIM))
    return cos_t, sin_t


def _decay_tables(n_heads):
    c = RET_CHUNK
    log_gamma = jnp.log(1.0 - 2.0 ** (-5.0 - jnp.arange(n_heads, dtype=jnp.float32)))
    n = jnp.arange(c, dtype=jnp.float32)
    rel = n[:, None] - n[None, :]
    dmask = jnp.where(rel >= 0, jnp.exp(jnp.maximum(rel, 0.0)[None] * log_gamma[:, None, None]), 0.0)
    dend = jnp.exp((c - 1.0 - n)[:, None] * log_gamma[None, :])
    dstart = jnp.exp((n + 1.0)[:, None] * log_gamma[None, :])
    cdec = jnp.exp(c * log_gamma)[None, :]
    wide = lambda a: jnp.repeat(a, HEAD_DIM, axis=1)
    return dmask, wide(dstart), wide(dend), wide(cdec)


def kernel(x, norm_mix_pre, w_in, rel_bias, w_out, norm_mix_post, norm_ffn_pre, w_up, conv_w,
           conv_b, w_down, norm_ffn_post):
    batch, seq, d = x.shape
    depth = w_in.shape[0]
    d_a = w_out.shape[1] // 2
    d_r = w_out.shape[1] - d_a
    assert seq % ROW_TILE == 0 and ROW_TILE % MOBA_BLOCK == 0 and MOBA_BLOCK == RET_CHUNK
    assert w_in.shape[2] == 3 * d_a + 4 * d_r and d_a % PAIR == 0 and d_r % PAIR == 0
    d_ff = w_down.shape[1]
    assert d_ff % FF_CHUNK == 0 and d_ff // FF_CHUNK >= FF_GROUPS and w_up.shape[2] == 2 * d_ff
    assert conv_w.shape[1] == CONV_WIDTH <= HALO
    assert d_a % (MOBA_HEADS * HEAD_DIM) == 0 and MOBA_HEADS % 2 == 0
    assert all(t % FAR_RING == 0 for t in FAR_TRIPS) and FAR_AHEAD < FAR_RING

    bf = jnp.bfloat16
    cos_t, sin_t = _rotary_tables(seq)
    dmask, dstart, dend, cdec = _decay_tables(d_r // HEAD_DIM)
    bias = _bias_tables(rel_bias)

    x2 = x.reshape(batch * seq, d)
    for l in range(depth):
        w = w_in[l].astype(bf)
        wqvt = jnp.concatenate([w[:, :d_a], w[:, 2 * d_a:3 * d_a]], axis=1).T
        wk = w[:, d_a:2 * d_a]
        wqkr = w[:, 3 * d_a:3 * d_a + 2 * d_r]
        wvgr = w[:, 3 * d_a + 2 * d_r:]
        qat, ka, vt, gate, qr, kr, vr, gr = _projection(
            x2, norm_mix_pre[l][None], wqvt, wk, wqkr, wvgr, cos_t, sin_t, batch, seq)
        ya = _moba(rel_bias, qat, ka, vt, gate, bias, batch, seq)
        yr = _retention(qr, kr, vr, gr, dmask, dstart, dend, cdec, batch, seq)
        wo = w_out[l].astype(bf)
        x2 = _out_ffn(ya, yr, x2, wo[:d_a], wo[d_a:], norm_mix_post[l][None],
                      norm_ffn_pre[l][None], w_up[l].astype(bf), conv_w[l], conv_b[l][None],
                      w_down[l].astype(bf), norm_ffn_post[l][None], seq)
    return x2.reshape(batch, seq, d)
```

```python
import functools
import math

import jax
import jax.numpy as jnp
from jax import lax
from jax.experimental import pallas as pl
from jax.experimental.pallas import tpu as pltpu

HEAD_DIM = 64
PAIR = 2 * HEAD_DIM
MOBA_BLOCK = 256
MOBA_TOPK = 3
RET_CHUNK = 256
REL_BUCKETS = 32
REL_MAX_DIST = 128
ROPE_BASE = 10000.0
CONV_WIDTH = 3
EPS = 1e-6
NEG = -1e30
LOG2E = math.log2(math.e)
Q_SCALE = HEAD_DIM ** -0.5 * LOG2E
V_ROWS = HEAD_DIM + 16
ROW_TILE = 512
RET_SUB = 2
MOBA_HEADS = 4
FAR_TRIPS = (16, 8, 4)
FAR_AHEAD = 2
FAR_RING = 4
FF_CHUNK = 256
FF_GROUPS = 2
HALO = 8
VMEM_LIMIT = 56 * 1024 * 1024

NT = (((1,), (1,)), ((), ()))
TN = (((0,), (0,)), ((), ()))


def _rms(x, g):
    return x * lax.rsqrt(jnp.mean(x * x, axis=-1, keepdims=True) + EPS) * g


def _bias_kernel(rb_ref, o_ref):
    h = pl.program_id(0)
    key = lax.broadcasted_iota(jnp.int32, (MOBA_BLOCK, MOBA_BLOCK), 0)
    qry = lax.broadcasted_iota(jnp.int32, (MOBA_BLOCK, MOBA_BLOCK), 1)
    max_exact = REL_BUCKETS // 2
    for which in range(2):
        rel = qry - key + which * MOBA_BLOCK
        n = jnp.maximum(rel, 0)
        n_f = jnp.maximum(n, 1).astype(jnp.float32)
        large = max_exact + (jnp.log(n_f / max_exact) / math.log(REL_MAX_DIST / max_exact)
                             * (REL_BUCKETS - max_exact)).astype(jnp.int32)
        large = jnp.minimum(large, REL_BUCKETS - 1)
        bucket = jnp.where(n < max_exact, n, large)
        val = jnp.zeros((MOBA_BLOCK, MOBA_BLOCK), jnp.float32)
        for b in range(REL_BUCKETS):
            val = jnp.where(bucket == b, rb_ref[b, h], val)
        o_ref[0, which] = jnp.where(rel >= 0, val * LOG2E, NEG)


def _bias_tables(rel_bias):
    n_heads = rel_bias.shape[1]
    return pl.pallas_call(
        _bias_kernel,
        grid=(n_heads,),
        in_specs=[pl.BlockSpec(memory_space=pltpu.SMEM)],
        out_specs=pl.BlockSpec((1, 2, MOBA_BLOCK, MOBA_BLOCK), lambda h: (h, 0, 0, 0)),
        out_shape=jax.ShapeDtypeStruct((n_heads, 2, MOBA_BLOCK, MOBA_BLOCK), jnp.float32),
        name="bias_tables",
    )(rel_bias)


def _proj_kernel(x_ref, g_ref, wqvt_ref, wk_ref, wqkr_ref, wvgr_ref, cos_ref, sin_ref,
                 qat_ref, ka_ref, vt_ref, gate_ref, qr_ref, kr_ref, vr_ref, gr_ref,
                 kmt_ref, *, tiles_per_batch, d_a, d_r):
    t_in_b = pl.program_id(0) % tiles_per_batch
    n_heads = d_a // HEAD_DIM
    tm = x_ref.shape[0]
    n_blocks = tiles_per_batch * (tm // MOBA_BLOCK)

    @pl.when(t_in_b == 0)
    def _():
        kmt_ref[...] = jnp.zeros_like(kmt_ref)

    hb = _rms(x_ref[...], g_ref[...]).astype(jnp.bfloat16)

    qvt = lax.dot_general(wqvt_ref[...], hb, NT, preferred_element_type=jnp.float32)
    qt = qvt[:d_a]
    qat_ref[...] = (qt * Q_SCALE).astype(jnp.bfloat16)
    k = jnp.dot(hb, wk_ref[...], preferred_element_type=jnp.float32)
    ka_ref[...] = k.astype(jnp.bfloat16)

    lane_head = lax.broadcasted_iota(jnp.int32, (1, d_a), 1) // HEAD_DIM
    for half in range(tm // MOBA_BLOCK):
        rows = slice(half * MOBA_BLOCK, (half + 1) * MOBA_BLOCK)
        gate_ref[:, rows] = jnp.dot(
            kmt_ref[...].astype(jnp.bfloat16), qt[:, rows].astype(jnp.bfloat16),
            preferred_element_type=jnp.float32)
        k_mean = jnp.mean(k[rows], axis=0, keepdims=True)
        j = t_in_b * (tm // MOBA_BLOCK) + half
        for hh in range(n_heads):
            kmt_ref[pl.ds(hh * n_blocks + j, 1), :] = jnp.where(lane_head == hh, k_mean, 0.0)

    vt = qvt[d_a:].astype(jnp.bfloat16)
    for hh in range(n_heads):
        vt_ref[hh, 0:HEAD_DIM, :] = vt[hh * HEAD_DIM:(hh + 1) * HEAD_DIM, :]
        vt_ref[hh, HEAD_DIM:V_ROWS, :] = jnp.ones((V_ROWS - HEAD_DIM, tm), jnp.bfloat16)

    qkr = jnp.dot(hb, wqkr_ref[...], preferred_element_type=jnp.float32)
    cos = cos_ref[...]
    sin = sin_ref[...]
    first_half = (lax.broadcasted_iota(jnp.int32, (tm, PAIR), 1) % HEAD_DIM) < HEAD_DIM // 2
    for cidx in range(2 * d_r // PAIR):
        xc = qkr[:, cidx * PAIR:(cidx + 1) * PAIR]
        sw = jnp.where(first_half,
                       pltpu.roll(xc, PAIR - HEAD_DIM // 2, axis=1),
                       pltpu.roll(xc, HEAD_DIM // 2, axis=1))
        rc = xc * cos + sw * sin
        if cidx < d_r // PAIR:
            qr_ref[:, cidx * PAIR:(cidx + 1) * PAIR] = rc.astype(jnp.bfloat16)
        else:
            c2 = cidx - d_r // PAIR
            kr_ref[:, c2 * PAIR:(c2 + 1) * PAIR] = (rc * HEAD_DIM ** -0.5).astype(jnp.bfloat16)

    vg = jnp.dot(hb, wvgr_ref[...], preferred_element_type=jnp.float32)
    vr_ref[...] = vg[:, :d_r].astype(jnp.bfloat16)
    gr_ref[...] = vg[:, d_r:].astype(jnp.bfloat16)


def _const_spec(shape):
    zeros = (0,) * len(shape)
    return pl.BlockSpec(shape, lambda *_: zeros, pipeline_mode=pl.Buffered(1))


def _projection(x2, g, wqvt, wk, wqkr, wvgr, cos_t, sin_t, batch, seq):
    t, d = x2.shape
    d_a = wk.shape[1]
    d_r = wqkr.shape[1] // 2
    n_heads = d_a // HEAD_DIM
    tm = ROW_TILE
    tpb = seq // tm
    n_blocks = seq // MOBA_BLOCK
    row = lambda w: pl.BlockSpec((tm, w), lambda i: (i, 0))
    bf = jnp.bfloat16
    return pl.pallas_call(
        functools.partial(_proj_kernel, tiles_per_batch=tpb, d_a=d_a, d_r=d_r),
        grid=(t // tm,),
        in_specs=[row(d), _const_spec((1, d)), _const_spec(wqvt.shape), _const_spec(wk.shape),
                  _const_spec(wqkr.shape), _const_spec(wvgr.shape),
                  pl.BlockSpec((tm, PAIR), lambda i: (i % tpb, 0)),
                  pl.BlockSpec((tm, PAIR), lambda i: (i % tpb, 0))],
        out_specs=[pl.BlockSpec((None, d_a, tm), lambda i: (i // tpb, 0, i % tpb)), row(d_a),
                   pl.BlockSpec((None, n_heads, V_ROWS, tm), lambda i: (i // tpb, 0, 0, i % tpb)),
                   pl.BlockSpec((None, n_heads * n_blocks, tm), lambda i: (i // tpb, 0, i % tpb)),
                   row(d_r), row(d_r), row(d_r), row(d_r)],
        out_shape=[jax.ShapeDtypeStruct((batch, d_a, seq), bf), jax.ShapeDtypeStruct((t, d_a), bf),
                   jax.ShapeDtypeStruct((batch, n_heads, V_ROWS, seq), bf),
                   jax.ShapeDtypeStruct((batch, n_heads * n_blocks, seq), jnp.float32),
                   jax.ShapeDtypeStruct((t, d_r), bf), jax.ShapeDtypeStruct((t, d_r), bf),
                   jax.ShapeDtypeStruct((t, d_r), bf), jax.ShapeDtypeStruct((t, d_r), bf)],
        scratch_shapes=[pltpu.VMEM((n_heads * n_blocks, d_a), jnp.float32)],
        compiler_params=pltpu.CompilerParams(
            dimension_semantics=("arbitrary",), vmem_limit_bytes=VMEM_LIMIT),
        name="projection",
    )(x2, g, wqvt, wk, wqkr, wvgr, cos_t, sin_t)


def _moba_kernel(rb_ref, qt_ref, k_ref, vt_ref, gate_ref, bias_ref, o_ref, sel_ref, s_ref, mj_ref,
                 p_ref, acc_ref, *, n_blocks):
    nh = vt_ref.shape[0]
    heads = range(nh)
    group = pl.program_id(1)
    c = pl.program_id(2)

    def pair_cols(h):
        return slice((h // 2) * PAIR, (h // 2 + 1) * PAIR)

    feat = lax.broadcasted_iota(jnp.int32, (PAIR, MOBA_BLOCK), 0)
    qm = []
    for h in heads:
        qt = qt_ref[pair_cols(h), :]
        own_feat = (feat >= HEAD_DIM) if h % 2 else (feat < HEAD_DIM)
        qm.append(jnp.where(own_feat, qt, jnp.zeros_like(qt)))

    slot_prev, slot_own = FAR_RING, FAR_RING + 1
    far_bias = [rb_ref[REL_BUCKETS - 1, nh * group + h] * LOG2E for h in heads]
    prev = jnp.maximum(c - 1, 0)
    n_far = prev
    last = jnp.maximum(n_far - 1, 0)

    def block_rows(j):
        return pl.ds(pl.multiple_of(j * MOBA_BLOCK, MOBA_BLOCK), MOBA_BLOCK)

    def scores(slot, j, bias_index=None):
        for h in heads:
            s = jnp.dot(k_ref[block_rows(j), pair_cols(h)], qm[h],
                        preferred_element_type=jnp.float32)
            if bias_index is not None:
                s = s + bias_ref[h, bias_index]
            s_ref[slot, h] = s
            mj_ref[slot, h] = jnp.max(s, axis=0, keepdims=True)

    def prob(slot, h):
        return jnp.exp2((s_ref[slot, h] - mj_ref[slot, h]).astype(jnp.bfloat16))

    def weigh(j, h, p):
        return jnp.dot(vt_ref[h, :, block_rows(j)], p, preferred_element_type=jnp.float32)

    def weighted(j, slot):
        return [weigh(j, h, prob(slot, h)) for h in heads]

    def merge(states, mj, oj, j, valid, always=False):
        out = []
        for h in heads:
            m = states[h]
            if always:
                m_new = jnp.maximum(m, mj[h])
                beta = jnp.exp2(mj[h] - m_new)
            else:
                chosen = (sel_ref[h, pl.ds(j, 1), :] > 0.5) & valid
                m_new = jnp.where(chosen, jnp.maximum(m, mj[h]), m)
                beta = jnp.where(chosen, jnp.exp2(mj[h] - m_new), 0.0)
            acc_ref[h] = acc_ref[h] * jnp.exp2(m - m_new) + oj[h] * beta
            out.append(m_new)
        return tuple(out)

    scores(slot_own, c, 0)
    scores(slot_prev, prev, 1)
    for i in range(FAR_AHEAD):
        scores(i, jnp.minimum(i, last))

    blk = lax.broadcasted_iota(jnp.int32, (n_blocks, MOBA_BLOCK), 0)
    for h in heads:
        g = jnp.where(blk < c, gate_ref[h * n_blocks:(h + 1) * n_blocks, :], -jnp.inf)
        sel = jnp.zeros(g.shape, jnp.float32)
        for _ in range(MOBA_TOPK):
            top = jnp.max(g, axis=0, keepdims=True)
            idx = jnp.min(jnp.where(g == top, blk, n_blocks), axis=0, keepdims=True)
            pick = (blk == idx) & (blk < c)
            sel = jnp.where(pick, 1.0, sel)
            g = jnp.where(pick, -jnp.inf, g)
        sel_ref[h] = sel

    states = tuple(jnp.full((1, MOBA_BLOCK), NEG, jnp.float32) for _ in heads)
    acc_ref[...] = jnp.zeros_like(acc_ref)

    p_ref[...] = jnp.zeros_like(p_ref)
    pend_m = tuple(jnp.zeros((1, MOBA_BLOCK), jnp.float32) for _ in heads)

    def trip(unroll, first):
        def body(t, carry):
            states, pend_m, pend_j, pend_valid = carry
            base = first + unroll * t
            js = [jnp.minimum(base + i, last) for i in range(unroll + FAR_AHEAD)]
            pending = (pend_m, [weigh(pend_j, h, p_ref[h]) for h in heads], pend_j,
                       pend_valid > 0)
            for i in range(unroll):
                scores((i + FAR_AHEAD) % FAR_RING, js[i + FAR_AHEAD])
                m_i = tuple(mj_ref[i % FAR_RING, h] + far_bias[h] for h in heads)
                if i + 1 < unroll:
                    o_i = weighted(js[i], i % FAR_RING)
                else:
                    for h in heads:
                        p_ref[h] = prob(i % FAR_RING, h)
                states = merge(states, *pending)
                if i + 1 < unroll:
                    pending = (m_i, o_i, js[i], base + i < n_far)
            return (states, m_i, js[unroll - 1], (base + unroll - 1 < n_far).astype(jnp.int32))
        return body

    carry = (states, pend_m, jnp.int32(0), jnp.int32(0))
    done = 0
    for unroll in FAR_TRIPS:
        left = n_far - done
        n_trips = left // unroll if unroll != FAR_TRIPS[-1] else (left + unroll - 1) // unroll
        carry = lax.fori_loop(0, n_trips, trip(unroll, done), carry)
        done = done + n_trips * unroll
    states, pend_m, pend_j, pend_valid = carry
    o_pend = [weigh(pend_j, h, p_ref[h]) for h in heads]
    o_prev = weighted(prev, slot_prev)
    o_own = weighted(c, slot_own)
    states = merge(states, pend_m, o_pend, pend_j, pend_valid > 0)
    states = merge(states, [mj_ref[slot_prev, h] for h in heads], o_prev, prev, True)
    states = merge(states, [mj_ref[slot_own, h] for h in heads], o_own, c, True, always=True)

    outs = [(acc_ref[h, :HEAD_DIM, :] / acc_ref[h, HEAD_DIM:HEAD_DIM + 1, :]).T for h in heads]
    o_ref[...] = jnp.concatenate(outs, axis=1).astype(o_ref.dtype)


def _moba(rel_bias, qat, ka, vt, gate, bias, batch, seq):
    d_a = ka.shape[1]
    n_blocks = seq // MOBA_BLOCK
    ka3 = ka.reshape(batch, seq, d_a)
    nh = MOBA_HEADS
    width = nh * HEAD_DIM
    out = pl.pallas_call(
        functools.partial(_moba_kernel, n_blocks=n_blocks),
        grid=(batch, d_a // width, n_blocks),
        in_specs=[pl.BlockSpec(memory_space=pltpu.SMEM),
                  pl.BlockSpec((None, width, MOBA_BLOCK), lambda b, g, c: (b, g, c)),
                  pl.BlockSpec((None, seq, width), lambda b, g, c: (b, 0, g)),
                  pl.BlockSpec((None, nh, V_ROWS, seq), lambda b, g, c: (b, g, 0, 0)),
                  pl.BlockSpec((None, nh * n_blocks, MOBA_BLOCK), lambda b, g, c: (b, g, c)),
                  pl.BlockSpec((nh, 2, MOBA_BLOCK, MOBA_BLOCK), lambda b, g, c: (g, 0, 0, 0),
                               pipeline_mode=pl.Buffered(1))],
        out_specs=pl.BlockSpec((None, MOBA_BLOCK, width), lambda b, g, c: (b, c, g)),
        out_shape=jax.ShapeDtypeStruct((batch, seq, d_a), jnp.bfloat16),
        scratch_shapes=[pltpu.VMEM((nh, n_blocks, MOBA_BLOCK), jnp.float32),
                        pltpu.VMEM((FAR_RING + 2, nh, MOBA_BLOCK, MOBA_BLOCK), jnp.float32),
                        pltpu.VMEM((FAR_RING + 2, nh, 1, MOBA_BLOCK), jnp.float32),
                        pltpu.VMEM((nh, MOBA_BLOCK, MOBA_BLOCK), jnp.bfloat16),
                        pltpu.VMEM((nh, V_ROWS, MOBA_BLOCK), jnp.float32)],
        compiler_params=pltpu.CompilerParams(
            dimension_semantics=("arbitrary", "arbitrary", "arbitrary"),
            vmem_limit_bytes=VMEM_LIMIT),
        name="moba",
    )(rel_bias, qat, ka3, vt, gate, bias)
    return out.reshape(batch * seq, d_a)


def _ret_kernel(q_ref, k_ref, v_ref, g_ref, dmask_ref, dstart_ref, dend_ref, cdec_ref,
                o_ref, state_ref):
    @pl.when(pl.program_id(1) == 0)
    def _():
        state_ref[...] = jnp.zeros_like(state_ref)

    n_pairs = state_ref.shape[0]
    bf = jnp.bfloat16
    f32 = jnp.float32
    items = [(p, s) for p in range(n_pairs) for s in range(RET_SUB)]
    cols = [slice(p * PAIR, (p + 1) * PAIR) for p in range(n_pairs)]
    rows = [slice(s * RET_CHUNK, (s + 1) * RET_CHUNK) for s in range(RET_SUB)]
    lane = lax.broadcasted_iota(jnp.int32, (RET_CHUNK, PAIR), 1)
    in_head = [lane < HEAD_DIM, lane >= HEAD_DIM]
    r = lax.broadcasted_iota(jnp.int32, (PAIR, PAIR), 0) // HEAD_DIM
    cc = lax.broadcasted_iota(jnp.int32, (PAIR, PAIR), 1) // HEAD_DIM
    same_head = r == cc
    avg = jnp.where(same_head, 1.0 / HEAD_DIM, 0.0).astype(bf)

    q = {(p, s): q_ref[rows[s], cols[p]] for p, s in items}
    k = {(p, s): k_ref[rows[s], cols[p]] for p, s in items}
    v = {(p, s): v_ref[rows[s], cols[p]] for p, s in items}

    sc = {i: [lax.dot_general(jnp.where(in_head[h], q[i], jnp.zeros_like(q[i])), k[i], NT,
                              preferred_element_type=f32) for h in range(2)] for i in items}
    kv = {i: lax.dot_general((k[i].astype(f32) * dend_ref[:, cols[i[0]]]).astype(bf), v[i], TN,
                             preferred_element_type=f32) for i in items}
    cross = {}
    for p in range(n_pairs):
        state = state_ref[p]
        for s in range(RET_SUB):
            cross[p, s] = jnp.dot(q[p, s], state.astype(bf), preferred_element_type=f32)
            state = jnp.where(same_head, state * cdec_ref[:, cols[p]] + kv[p, s], 0.0)
        state_ref[p] = state

    y = {}
    for p, s in items:
        yh = [jnp.dot((sc[p, s][h] * dmask_ref[2 * p + h]).astype(bf), v[p, s],
                      preferred_element_type=f32) for h in range(2)]
        y[p, s] = jnp.where(in_head[0], yh[0], yh[1]) + cross[p, s] * dstart_ref[:, cols[p]]

    def head_mean(t):
        return jnp.dot(t.astype(bf), avg, preferred_element_type=f32)

    y_hi = {i: y[i].astype(bf) for i in items}
    mu = {i: jnp.dot(y_hi[i], avg, preferred_element_type=f32)
          + head_mean(y[i] - y_hi[i].astype(f32)) for i in items}
    d = {i: y[i] - mu[i] for i in items}
    var = {i: head_mean(d[i] * d[i]) for i in items}
    for p, s in items:
        g = g_ref[rows[s], cols[p]].astype(f32)
        o_ref[rows[s], cols[p]] = (g * jax.nn.sigmoid(g) * d[p, s]
                                   * lax.rsqrt(var[p, s] + EPS)).astype(bf)


def _retention(qr, kr, vr, gr, dmask, dstart, dend, cdec, batch, seq):
    d_r = qr.shape[1]
    n_steps = seq // (RET_SUB * RET_CHUNK)
    blk = pl.BlockSpec((None, RET_SUB * RET_CHUNK, d_r), lambda b, c: (b, c, 0))
    r3 = lambda a: a.reshape(batch, seq, d_r)
    out = pl.pallas_call(
        _ret_kernel,
        grid=(batch, n_steps),
        in_specs=[blk, blk, blk, blk, _const_spec(dmask.shape), _const_spec(dstart.shape),
                  _const_spec(dend.shape), _const_spec(cdec.shape)],
        out_specs=blk,
        out_shape=jax.ShapeDtypeStruct((batch, seq, d_r), jnp.bfloat16),
        scratch_shapes=[pltpu.VMEM((d_r // PAIR, PAIR, PAIR), jnp.float32)],
        compiler_params=pltpu.CompilerParams(
            dimension_semantics=("arbitrary", "arbitrary"), vmem_limit_bytes=VMEM_LIMIT),
        name="retention",
    )(r3(qr), r3(kr), r3(vr), r3(gr), dmask, dstart, dend, cdec)
    return out.reshape(batch * seq, d_r)


def _ffn_kernel(ya_ref, yr_ref, x_ref, woa_ref, wor_ref, g1_ref, g2_ref, wup_ref, cw_ref, cb_ref,
                wdn_ref, g3_ref, o_ref, u_ref, f_ref, *, tiles_per_batch, d_ff):
    tm = x_ref.shape[0]

    @pl.when(pl.program_id(0) % tiles_per_batch == 0)
    def _():
        u_ref[0:HALO, :] = jnp.zeros((HALO, u_ref.shape[1]), jnp.float32)

    y = (jnp.dot(ya_ref[...], woa_ref[...], preferred_element_type=jnp.float32)
         + jnp.dot(yr_ref[...], wor_ref[...], preferred_element_type=jnp.float32))
    x1 = x_ref[...] + _rms(y, g1_ref[...])
    h2 = _rms(x1, g2_ref[...]).astype(jnp.bfloat16)

    def chunk_cols(ci):
        lo = ci * FF_CHUNK
        return slice(lo, lo + FF_CHUNK), slice(d_ff + lo, d_ff + lo + FF_CHUNK)

    def up(ci):
        for cols in chunk_cols(ci):
            u_ref[HALO:HALO + tm, cols] = jnp.dot(h2, wup_ref[:, cols],
                                                  preferred_element_type=jnp.float32)

    def conv(cols):
        w = cw_ref[:, cols]
        out = cb_ref[:, cols]
        for j in reversed(range(CONV_WIDTH)):
            lo = HALO - (CONV_WIDTH - 1 - j)
            out = out + w[j:j + 1] * u_ref[lo:lo + tm, cols]
        return out

    n_chunks = d_ff // FF_CHUNK
    group_ends = [(g + 1) * n_chunks // FF_GROUPS for g in range(FF_GROUPS)]
    acc = None
    start = 0
    up(0)
    for ci in range(n_chunks):
        if ci + 1 < n_chunks:
            up(ci + 1)
        cols_a, cols_b = chunk_cols(ci)
        f_ref[:, cols_a] = (jax.nn.gelu(conv(cols_a), approximate=True)
                            * conv(cols_b)).astype(jnp.bfloat16)
        if ci + 1 in group_ends:
            rows = slice(start * FF_CHUNK, (ci + 1) * FF_CHUNK)
            part = jnp.dot(f_ref[:, rows], wdn_ref[rows, :], preferred_element_type=jnp.float32)
            acc = part if acc is None else acc + part
            start = ci + 1
    u_ref[0:HALO, :] = u_ref[tm:tm + HALO, :]
    o_ref[...] = x1 + _rms(acc, g3_ref[...])


def _out_ffn(ya, yr, x2, woa, wor, g1, g2, wup, cw, cb, wdn, g3, seq):
    t, d = x2.shape
    d_ff = wdn.shape[0]
    tm = ROW_TILE
    row = lambda w: pl.BlockSpec((tm, w), lambda i: (i, 0))
    return pl.pallas_call(
        functools.partial(_ffn_kernel, tiles_per_batch=seq // tm, d_ff=d_ff),
        grid=(t // tm,),
        in_specs=[row(ya.shape[1]), row(yr.shape[1]), row(d),
                  _const_spec(woa.shape), _const_spec(wor.shape),
                  _const_spec((1, d)), _const_spec((1, d)), _const_spec(wup.shape),
                  _const_spec(cw.shape), _const_spec(cb.shape), _const_spec(wdn.shape),
                  _const_spec((1, d))],
        out_specs=row(d),
        out_shape=jax.ShapeDtypeStruct((t, d), jnp.float32),
        scratch_shapes=[pltpu.VMEM((HALO + tm, 2 * d_ff), jnp.float32),
                        pltpu.VMEM((tm, d_ff), jnp.bfloat16)],
        compiler_params=pltpu.CompilerParams(
            dimension_semantics=("arbitrary",), vmem_limit_bytes=VMEM_LIMIT),
        name="out_ffn",
    )(ya, yr, x2, woa, wor, g1, g2, wup, cw, cb, wdn, g3)


def _rotary_tables(seq):
    inv = ROPE_BASE ** (-jnp.arange(0, HEAD_DIM, 2, dtype=jnp.float32) / HEAD_DIM)
    ang = jnp.arange(seq, dtype=jnp.float32)[:, None] * inv[None, :]
    cos, sin = jnp.cos(ang), jnp.sin(ang)
    cos_t = jnp.tile(jnp.concatenate([cos, cos], axis=1), (1, PAIR // HEAD_DIM))
    sin_t = jnp.tile(jnp.concatenate([-sin, sin], axis=1), (1, PAIR // HEAD_DIM))
    return cos_t, sin_t


def _decay_tables(n_heads):
    c = RET_CHUNK
    log_gamma = jnp.log(1.0 - 2.0 ** (-5.0 - jnp.arange(n_heads, dtype=jnp.float32)))
    n = jnp.arange(c, dtype=jnp.float32)
    rel = n[:, None] - n[None, :]
    dmask = jnp.where(rel >= 0, jnp.exp(jnp.maximum(rel, 0.0)[None] * log_gamma[:, None, None]), 0.0)
    dend = jnp.exp((c - 1.0 - n)[:, None] * log_gamma[None, :])
    dstart = jnp.exp((n + 1.0)[:, None] * log_gamma[None, :])
    cdec = jnp.exp(c * log_gamma)[None, :]
    wide = lambda a: jnp.repeat(a, HEAD_DIM, axis=1)
    return dmask, wide(dstart), wide(dend), wide(cdec)


def kernel(x, norm_mix_pre, w_in, rel_bias, w_out, norm_mix_post, norm_ffn_pre, w_up, conv_w,
           conv_b, w_down, norm_ffn_post):
    batch, seq, d = x.shape
    depth = w_in.shape[0]
    d_a = w_out.shape[1] // 2
    d_r = w_out.shape[1] - d_a
    assert seq % ROW_TILE == 0 and ROW_TILE % MOBA_BLOCK == 0 and seq % (RET_SUB * RET_CHUNK) == 0
    assert w_in.shape[2] == 3 * d_a + 4 * d_r and d_a % PAIR == 0 and d_r % PAIR == 0
    d_ff = w_down.shape[1]
    assert d_ff % FF_CHUNK == 0 and d_ff // FF_CHUNK >= FF_GROUPS and w_up.shape[2] == 2 * d_ff
    assert conv_w.shape[1] == CONV_WIDTH <= HALO
    assert d_a % (MOBA_HEADS * HEAD_DIM) == 0 and MOBA_HEADS % 2 == 0
    assert all(t % FAR_RING == 0 for t in FAR_TRIPS) and FAR_AHEAD < FAR_RING

    bf = jnp.bfloat16
    cos_t, sin_t = _rotary_tables(seq)
    dmask, dstart, dend, cdec = _decay_tables(d_r // HEAD_DIM)
    bias = _bias_tables(rel_bias)

    x2 = x.reshape(batch * seq, d)
    for l in range(depth):
        w = w_in[l].astype(bf)
        wqvt = jnp.concatenate([w[:, :d_a], w[:, 2 * d_a:3 * d_a]], axis=1).T
        wk = w[:, d_a:2 * d_a]
        wqkr = w[:, 3 * d_a:3 * d_a + 2 * d_r]
        wvgr = w[:, 3 * d_a + 2 * d_r:]
        qat, ka, vt, gate, qr, kr, vr, gr = _projection(
            x2, norm_mix_pre[l][None], wqvt, wk, wqkr, wvgr, cos_t, sin_t, batch, seq)
        ya = _moba(rel_bias, qat, ka, vt, gate, bias, batch, seq)
        yr = _retention(qr, kr, vr, gr, dmask, dstart, dend, cdec, batch, seq)
        wo = w_out[l].astype(bf)
        x2 = _out_ffn(ya, yr, x2, wo[:d_a], wo[d_a:], norm_mix_post[l][None],
                      norm_ffn_pre[l][None], w_up[l].astype(bf), conv_w[l], conv_b[l][None],
                      w_down[l].astype(bf), norm_ffn_post[l][None], seq)
    return x2.reshape(batch, seq, d)
```

```python
import functools
import math

import jax
import jax.numpy as jnp
from jax import lax
from jax.experimental import pallas as pl
from jax.experimental.pallas import tpu as pltpu

HEAD_DIM = 64
PAIR = 2 * HEAD_DIM
MOBA_BLOCK = 256
MOBA_TOPK = 3
RET_CHUNK = 256
REL_BUCKETS = 32
REL_MAX_DIST = 128
ROPE_BASE = 10000.0
CONV_WIDTH = 3
EPS = 1e-6
NEG = -1e30
LOG2E = math.log2(math.e)
Q_SCALE = HEAD_DIM ** -0.5 * LOG2E
V_ROWS = HEAD_DIM + 16
PROJ_TILE = 1024
ROW_TILE = 512
RET_SUB = 4
MOBA_HEADS = 4
FAR_TRIPS = (16, 8, 4)
FAR_AHEAD = 2
FAR_RING = 4
FF_CHUNK = 256
FF_GROUPS = 2
HALO = 8
VMEM_LIMIT = 56 * 1024 * 1024

NT = (((1,), (1,)), ((), ()))
TN = (((0,), (0,)), ((), ()))


def _rms(x, g):
    return x * lax.rsqrt(jnp.mean(x * x, axis=-1, keepdims=True) + EPS) * g


def _bias_kernel(rb_ref, o_ref):
    h = pl.program_id(0)
    key = lax.broadcasted_iota(jnp.int32, (MOBA_BLOCK, MOBA_BLOCK), 0)
    qry = lax.broadcasted_iota(jnp.int32, (MOBA_BLOCK, MOBA_BLOCK), 1)
    max_exact = REL_BUCKETS // 2
    for which in range(2):
        rel = qry - key + which * MOBA_BLOCK
        n = jnp.maximum(rel, 0)
        n_f = jnp.maximum(n, 1).astype(jnp.float32)
        large = max_exact + (jnp.log(n_f / max_exact) / math.log(REL_MAX_DIST / max_exact)
                             * (REL_BUCKETS - max_exact)).astype(jnp.int32)
        large = jnp.minimum(large, REL_BUCKETS - 1)
        bucket = jnp.where(n < max_exact, n, large)
        val = jnp.zeros((MOBA_BLOCK, MOBA_BLOCK), jnp.float32)
        for b in range(REL_BUCKETS):
            val = jnp.where(bucket == b, rb_ref[b, h], val)
        o_ref[0, which] = jnp.where(rel >= 0, val * LOG2E, NEG)


def _bias_tables(rel_bias):
    n_heads = rel_bias.shape[1]
    return pl.pallas_call(
        _bias_kernel,
        grid=(n_heads,),
        in_specs=[pl.BlockSpec(memory_space=pltpu.SMEM)],
        out_specs=pl.BlockSpec((1, 2, MOBA_BLOCK, MOBA_BLOCK), lambda h: (h, 0, 0, 0)),
        out_shape=jax.ShapeDtypeStruct((n_heads, 2, MOBA_BLOCK, MOBA_BLOCK), jnp.float32),
        name="bias_tables",
    )(rel_bias)


def _proj_kernel(x_ref, g_ref, wqvt_ref, wk_ref, wqkr_ref, wvgr_ref, cos_ref, sin_ref,
                 qat_ref, ka_ref, vt_ref, gate_ref, qr_ref, kr_ref, vr_ref, gr_ref,
                 kmt_ref, *, tiles_per_batch, d_a, d_r):
    t_in_b = pl.program_id(0) % tiles_per_batch
    n_heads = d_a // HEAD_DIM
    tm = x_ref.shape[0]
    n_blocks = tiles_per_batch * (tm // MOBA_BLOCK)

    @pl.when(t_in_b == 0)
    def _():
        kmt_ref[...] = jnp.zeros_like(kmt_ref)

    hb = _rms(x_ref[...], g_ref[...]).astype(jnp.bfloat16)

    qvt = lax.dot_general(wqvt_ref[...], hb, NT, preferred_element_type=jnp.float32)
    qt = qvt[:d_a]
    qat_ref[...] = (qt * Q_SCALE).astype(jnp.bfloat16)
    k = jnp.dot(hb, wk_ref[...], preferred_element_type=jnp.float32)
    ka_ref[...] = k.astype(jnp.bfloat16)

    lane_head = lax.broadcasted_iota(jnp.int32, (1, d_a), 1) // HEAD_DIM
    for half in range(tm // MOBA_BLOCK):
        rows = slice(half * MOBA_BLOCK, (half + 1) * MOBA_BLOCK)
        gate_ref[:, rows] = jnp.dot(
            kmt_ref[...].astype(jnp.bfloat16), qt[:, rows].astype(jnp.bfloat16),
            preferred_element_type=jnp.float32)
        k_mean = jnp.mean(k[rows], axis=0, keepdims=True)
        j = t_in_b * (tm // MOBA_BLOCK) + half
        for hh in range(n_heads):
            kmt_ref[pl.ds(hh * n_blocks + j, 1), :] = jnp.where(lane_head == hh, k_mean, 0.0)

    vt = qvt[d_a:].astype(jnp.bfloat16)
    for hh in range(n_heads):
        vt_ref[hh, 0:HEAD_DIM, :] = vt[hh * HEAD_DIM:(hh + 1) * HEAD_DIM, :]
        vt_ref[hh, HEAD_DIM:V_ROWS, :] = jnp.ones((V_ROWS - HEAD_DIM, tm), jnp.bfloat16)

    qkr = jnp.dot(hb, wqkr_ref[...], preferred_element_type=jnp.float32)
    cos = cos_ref[...]
    sin = sin_ref[...]
    first_half = (lax.broadcasted_iota(jnp.int32, (tm, PAIR), 1) % HEAD_DIM) < HEAD_DIM // 2
    for cidx in range(2 * d_r // PAIR):
        xc = qkr[:, cidx * PAIR:(cidx + 1) * PAIR]
        sw = jnp.where(first_half,
                       pltpu.roll(xc, PAIR - HEAD_DIM // 2, axis=1),
                       pltpu.roll(xc, HEAD_DIM // 2, axis=1))
        rc = xc * cos + sw * sin
        if cidx < d_r // PAIR:
            qr_ref[:, cidx * PAIR:(cidx + 1) * PAIR] = rc.astype(jnp.bfloat16)
        else:
            c2 = cidx - d_r // PAIR
            kr_ref[:, c2 * PAIR:(c2 + 1) * PAIR] = (rc * HEAD_DIM ** -0.5).astype(jnp.bfloat16)

    vg = jnp.dot(hb, wvgr_ref[...], preferred_element_type=jnp.float32)
    vr_ref[...] = vg[:, :d_r].astype(jnp.bfloat16)
    gr_ref[...] = vg[:, d_r:].astype(jnp.bfloat16)


def _const_spec(shape):
    zeros = (0,) * len(shape)
    return pl.BlockSpec(shape, lambda *_: zeros, pipeline_mode=pl.Buffered(1))


def _projection(x2, g, wqvt, wk, wqkr, wvgr, cos_t, sin_t, batch, seq):
    t, d = x2.shape
    d_a = wk.shape[1]
    d_r = wqkr.shape[1] // 2
    n_heads = d_a // HEAD_DIM
    tm = PROJ_TILE
    tpb = seq // tm
    n_blocks = seq // MOBA_BLOCK
    row = lambda w: pl.BlockSpec((tm, w), lambda i: (i, 0))
    bf = jnp.bfloat16
    return pl.pallas_call(
        functools.partial(_proj_kernel, tiles_per_batch=tpb, d_a=d_a, d_r=d_r),
        grid=(t // tm,),
        in_specs=[row(d), _const_spec((1, d)), _const_spec(wqvt.shape), _const_spec(wk.shape),
                  _const_spec(wqkr.shape), _const_spec(wvgr.shape),
                  pl.BlockSpec((tm, PAIR), lambda i: (i % tpb, 0)),
                  pl.BlockSpec((tm, PAIR), lambda i: (i % tpb, 0))],
        out_specs=[pl.BlockSpec((None, d_a, tm), lambda i: (i // tpb, 0, i % tpb)), row(d_a),
                   pl.BlockSpec((None, n_heads, V_ROWS, tm), lambda i: (i // tpb, 0, 0, i % tpb)),
                   pl.BlockSpec((None, n_heads * n_blocks, tm), lambda i: (i // tpb, 0, i % tpb)),
                   row(d_r), row(d_r), row(d_r), row(d_r)],
        out_shape=[jax.ShapeDtypeStruct((batch, d_a, seq), bf), jax.ShapeDtypeStruct((t, d_a), bf),
                   jax.ShapeDtypeStruct((batch, n_heads, V_ROWS, seq), bf),
                   jax.ShapeDtypeStruct((batch, n_heads * n_blocks, seq), jnp.float32),
                   jax.ShapeDtypeStruct((t, d_r), bf), jax.ShapeDtypeStruct((t, d_r), bf),
                   jax.ShapeDtypeStruct((t, d_r), bf), jax.ShapeDtypeStruct((t, d_r), bf)],
        scratch_shapes=[pltpu.VMEM((n_heads * n_blocks, d_a), jnp.float32)],
        compiler_params=pltpu.CompilerParams(
            dimension_semantics=("arbitrary",), vmem_limit_bytes=VMEM_LIMIT),
        name="projection",
    )(x2, g, wqvt, wk, wqkr, wvgr, cos_t, sin_t)


def _moba_kernel(rb_ref, qt_ref, k_ref, vt_ref, gate_ref, bias_ref, o_ref, sel_ref, s_ref, mj_ref,
                 p_ref, acc_ref, *, n_blocks):
    nh = vt_ref.shape[0]
    heads = range(nh)
    group = pl.program_id(1)
    c = pl.program_id(2)

    def pair_cols(h):
        return slice((h // 2) * PAIR, (h // 2 + 1) * PAIR)

    feat = lax.broadcasted_iota(jnp.int32, (PAIR, MOBA_BLOCK), 0)
    qm = []
    for h in heads:
        qt = qt_ref[pair_cols(h), :]
        own_feat = (feat >= HEAD_DIM) if h % 2 else (feat < HEAD_DIM)
        qm.append(jnp.where(own_feat, qt, jnp.zeros_like(qt)))

    slot_prev, slot_own = FAR_RING, FAR_RING + 1
    far_bias = [rb_ref[REL_BUCKETS - 1, nh * group + h] * LOG2E for h in heads]
    prev = jnp.maximum(c - 1, 0)
    n_far = prev
    last = jnp.maximum(n_far - 1, 0)

    def block_rows(j):
        return pl.ds(pl.multiple_of(j * MOBA_BLOCK, MOBA_BLOCK), MOBA_BLOCK)

    def scores(slot, j, bias_index=None):
        for h in heads:
            s = jnp.dot(k_ref[block_rows(j), pair_cols(h)], qm[h],
                        preferred_element_type=jnp.float32)
            if bias_index is not None:
                s = s + bias_ref[h, bias_index]
            s_ref[slot, h] = s
            mj_ref[slot, h] = jnp.max(s, axis=0, keepdims=True)

    def prob(slot, h):
        return jnp.exp2((s_ref[slot, h] - mj_ref[slot, h]).astype(jnp.bfloat16))

    def weigh(j, h, p):
        return jnp.dot(vt_ref[h, :, block_rows(j)], p, preferred_element_type=jnp.float32)

    def weighted(j, slot):
        return [weigh(j, h, prob(slot, h)) for h in heads]

    def merge(states, mj, oj, j, valid, always=False):
        out = []
        for h in heads:
            m = states[h]
            if always:
                m_new = jnp.maximum(m, mj[h])
                beta = jnp.exp2(mj[h] - m_new)
            else:
                chosen = (sel_ref[h, pl.ds(j, 1), :] > 0.5) & valid
                m_new = jnp.where(chosen, jnp.maximum(m, mj[h]), m)
                beta = jnp.where(chosen, jnp.exp2(mj[h] - m_new), 0.0)
            acc_ref[h] = acc_ref[h] * jnp.exp2(m - m_new) + oj[h] * beta
            out.append(m_new)
        return tuple(out)

    scores(slot_own, c, 0)
    scores(slot_prev, prev, 1)
    for i in range(FAR_AHEAD):
        scores(i, jnp.minimum(i, last))

    blk = lax.broadcasted_iota(jnp.int32, (n_blocks, MOBA_BLOCK), 0)
    for h in heads:
        g = jnp.where(blk < c, gate_ref[h * n_blocks:(h + 1) * n_blocks, :], -jnp.inf)
        sel = jnp.zeros(g.shape, jnp.float32)
        for _ in range(MOBA_TOPK):
            top = jnp.max(g, axis=0, keepdims=True)
            idx = jnp.min(jnp.where(g == top, blk, n_blocks), axis=0, keepdims=True)
            pick = (blk == idx) & (blk < c)
            sel = jnp.where(pick, 1.0, sel)
            g = jnp.where(pick, -jnp.inf, g)
        sel_ref[h] = sel

    states = tuple(jnp.full((1, MOBA_BLOCK), NEG, jnp.float32) for _ in heads)
    acc_ref[...] = jnp.zeros_like(acc_ref)

    p_ref[...] = jnp.zeros_like(p_ref)
    pend_m = tuple(jnp.zeros((1, MOBA_BLOCK), jnp.float32) for _ in heads)

    def trip(unroll, first):
        def body(t, carry):
            states, pend_m, pend_j, pend_valid = carry
            base = first + unroll * t
            js = [jnp.minimum(base + i, last) for i in range(unroll + FAR_AHEAD)]
            pending = (pend_m, [weigh(pend_j, h, p_ref[h]) for h in heads], pend_j,
                       pend_valid > 0)
            for i in range(unroll):
                scores((i + FAR_AHEAD) % FAR_RING, js[i + FAR_AHEAD])
                m_i = tuple(mj_ref[i % FAR_RING, h] + far_bias[h] for h in heads)
                if i + 1 < unroll:
                    o_i = weighted(js[i], i % FAR_RING)
                else:
                    for h in heads:
                        p_ref[h] = prob(i % FAR_RING, h)
                states = merge(states, *pending)
                if i + 1 < unroll:
                    pending = (m_i, o_i, js[i], base + i < n_far)
            return (states, m_i, js[unroll - 1], (base + unroll - 1 < n_far).astype(jnp.int32))
        return body

    carry = (states, pend_m, jnp.int32(0), jnp.int32(0))
    done = 0
    for unroll in FAR_TRIPS:
        left = n_far - done
        n_trips = left // unroll if unroll != FAR_TRIPS[-1] else (left + unroll - 1) // unroll
        carry = lax.fori_loop(0, n_trips, trip(unroll, done), carry)
        done = done + n_trips * unroll
    states, pend_m, pend_j, pend_valid = carry
    o_pend = [weigh(pend_j, h, p_ref[h]) for h in heads]
    o_prev = weighted(prev, slot_prev)
    o_own = weighted(c, slot_own)
    states = merge(states, pend_m, o_pend, pend_j, pend_valid > 0)
    states = merge(states, [mj_ref[slot_prev, h] for h in heads], o_prev, prev, True)
    states = merge(states, [mj_ref[slot_own, h] for h in heads], o_own, c, True, always=True)

    outs = [(acc_ref[h, :HEAD_DIM, :] / acc_ref[h, HEAD_DIM:HEAD_DIM + 1, :]).T for h in heads]
    o_ref[...] = jnp.concatenate(outs, axis=1).astype(o_ref.dtype)


def _moba(rel_bias, qat, ka, vt, gate, bias, batch, seq):
    d_a = ka.shape[1]
    n_blocks = seq // MOBA_BLOCK
    ka3 = ka.reshape(batch, seq, d_a)
    nh = MOBA_HEADS
    width = nh * HEAD_DIM
    out = pl.pallas_call(
        functools.partial(_moba_kernel, n_blocks=n_blocks),
        grid=(batch, d_a // width, n_blocks),
        in_specs=[pl.BlockSpec(memory_space=pltpu.SMEM),
                  pl.BlockSpec((None, width, MOBA_BLOCK), lambda b, g, c: (b, g, c)),
                  pl.BlockSpec((None, seq, width), lambda b, g, c: (b, 0, g)),
                  pl.BlockSpec((None, nh, V_ROWS, seq), lambda b, g, c: (b, g, 0, 0)),
                  pl.BlockSpec((None, nh * n_blocks, MOBA_BLOCK), lambda b, g, c: (b, g, c)),
                  pl.BlockSpec((nh, 2, MOBA_BLOCK, MOBA_BLOCK), lambda b, g, c: (g, 0, 0, 0),
                               pipeline_mode=pl.Buffered(1))],
        out_specs=pl.BlockSpec((None, MOBA_BLOCK, width), lambda b, g, c: (b, c, g)),
        out_shape=jax.ShapeDtypeStruct((batch, seq, d_a), jnp.bfloat16),
        scratch_shapes=[pltpu.VMEM((nh, n_blocks, MOBA_BLOCK), jnp.float32),
                        pltpu.VMEM((FAR_RING + 2, nh, MOBA_BLOCK, MOBA_BLOCK), jnp.float32),
                        pltpu.VMEM((FAR_RING + 2, nh, 1, MOBA_BLOCK), jnp.float32),
                        pltpu.VMEM((nh, MOBA_BLOCK, MOBA_BLOCK), jnp.bfloat16),
                        pltpu.VMEM((nh, V_ROWS, MOBA_BLOCK), jnp.float32)],
        compiler_params=pltpu.CompilerParams(
            dimension_semantics=("arbitrary", "arbitrary", "arbitrary"),
            vmem_limit_bytes=VMEM_LIMIT),
        name="moba",
    )(rel_bias, qat, ka3, vt, gate, bias)
    return out.reshape(batch * seq, d_a)


def _ret_kernel(q_ref, k_ref, v_ref, g_ref, dmask_ref, dstart_ref, dend_ref, cdec_ref,
                o_ref, state_ref):
    @pl.when(pl.program_id(1) == 0)
    def _():
        state_ref[...] = jnp.zeros_like(state_ref)

    n_pairs = state_ref.shape[0]
    bf = jnp.bfloat16
    f32 = jnp.float32
    items = [(p, s) for p in range(n_pairs) for s in range(RET_SUB)]
    cols = [slice(p * PAIR, (p + 1) * PAIR) for p in range(n_pairs)]
    rows = [slice(s * RET_CHUNK, (s + 1) * RET_CHUNK) for s in range(RET_SUB)]
    lane = lax.broadcasted_iota(jnp.int32, (RET_CHUNK, PAIR), 1)
    in_head = [lane < HEAD_DIM, lane >= HEAD_DIM]
    r = lax.broadcasted_iota(jnp.int32, (PAIR, PAIR), 0) // HEAD_DIM
    cc = lax.broadcasted_iota(jnp.int32, (PAIR, PAIR), 1) // HEAD_DIM
    same_head = r == cc
    avg = jnp.where(same_head, 1.0 / HEAD_DIM, 0.0).astype(bf)

    q = {(p, s): q_ref[rows[s], cols[p]] for p, s in items}
    k = {(p, s): k_ref[rows[s], cols[p]] for p, s in items}
    v = {(p, s): v_ref[rows[s], cols[p]] for p, s in items}

    sc = {i: [lax.dot_general(jnp.where(in_head[h], q[i], jnp.zeros_like(q[i])), k[i], NT,
                              preferred_element_type=f32) for h in range(2)] for i in items}
    kv = {i: lax.dot_general((k[i].astype(f32) * dend_ref[:, cols[i[0]]]).astype(bf), v[i], TN,
                             preferred_element_type=f32) for i in items}
    cross = {}
    for p in range(n_pairs):
        state = state_ref[p]
        for s in range(RET_SUB):
            cross[p, s] = jnp.dot(q[p, s], state.astype(bf), preferred_element_type=f32)
            state = jnp.where(same_head, state * cdec_ref[:, cols[p]] + kv[p, s], 0.0)
        state_ref[p] = state

    y = {}
    for p, s in items:
        yh = [jnp.dot((sc[p, s][h] * dmask_ref[2 * p + h]).astype(bf), v[p, s],
                      preferred_element_type=f32) for h in range(2)]
        y[p, s] = jnp.where(in_head[0], yh[0], yh[1]) + cross[p, s] * dstart_ref[:, cols[p]]

    def head_mean(t):
        return jnp.dot(t.astype(bf), avg, preferred_element_type=f32)

    y_hi = {i: y[i].astype(bf) for i in items}
    mu = {i: jnp.dot(y_hi[i], avg, preferred_element_type=f32)
          + head_mean(y[i] - y_hi[i].astype(f32)) for i in items}
    d = {i: y[i] - mu[i] for i in items}
    var = {i: head_mean(d[i] * d[i]) for i in items}
    for p, s in items:
        g = g_ref[rows[s], cols[p]].astype(f32)
        o_ref[rows[s], cols[p]] = (g * jax.nn.sigmoid(g) * d[p, s]
                                   * lax.rsqrt(var[p, s] + EPS)).astype(bf)


def _retention(qr, kr, vr, gr, dmask, dstart, dend, cdec, batch, seq):
    d_r = qr.shape[1]
    n_steps = seq // (RET_SUB * RET_CHUNK)
    blk = pl.BlockSpec((None, RET_SUB * RET_CHUNK, d_r), lambda b, c: (b, c, 0))
    r3 = lambda a: a.reshape(batch, seq, d_r)
    out = pl.pallas_call(
        _ret_kernel,
        grid=(batch, n_steps),
        in_specs=[blk, blk, blk, blk, _const_spec(dmask.shape), _const_spec(dstart.shape),
                  _const_spec(dend.shape), _const_spec(cdec.shape)],
        out_specs=blk,
        out_shape=jax.ShapeDtypeStruct((batch, seq, d_r), jnp.bfloat16),
        scratch_shapes=[pltpu.VMEM((d_r // PAIR, PAIR, PAIR), jnp.float32)],
        compiler_params=pltpu.CompilerParams(
            dimension_semantics=("arbitrary", "arbitrary"), vmem_limit_bytes=VMEM_LIMIT),
        name="retention",
    )(r3(qr), r3(kr), r3(vr), r3(gr), dmask, dstart, dend, cdec)
    return out.reshape(batch * seq, d_r)


def _ffn_kernel(ya_ref, yr_ref, x_ref, woa_ref, wor_ref, g1_ref, g2_ref, wup_ref, cw_ref, cb_ref,
                wdn_ref, g3_ref, o_ref, u_ref, f_ref, *, tiles_per_batch, d_ff):
    tm = x_ref.shape[0]

    @pl.when(pl.program_id(0) % tiles_per_batch == 0)
    def _():
        u_ref[0:HALO, :] = jnp.zeros((HALO, u_ref.shape[1]), jnp.float32)

    y = (jnp.dot(ya_ref[...], woa_ref[...], preferred_element_type=jnp.float32)
         + jnp.dot(yr_ref[...], wor_ref[...], preferred_element_type=jnp.float32))
    x1 = x_ref[...] + _rms(y, g1_ref[...])
    h2 = _rms(x1, g2_ref[...]).astype(jnp.bfloat16)

    def chunk_cols(ci):
        lo = ci * FF_CHUNK
        return slice(lo, lo + FF_CHUNK), slice(d_ff + lo, d_ff + lo + FF_CHUNK)

    def up(ci):
        for cols in chunk_cols(ci):
            u_ref[HALO:HALO + tm, cols] = jnp.dot(h2, wup_ref[:, cols],
                                                  preferred_element_type=jnp.float32)

    def conv(cols):
        w = cw_ref[:, cols]
        out = cb_ref[:, cols]
        for j in reversed(range(CONV_WIDTH)):
            lo = HALO - (CONV_WIDTH - 1 - j)
            out = out + w[j:j + 1] * u_ref[lo:lo + tm, cols]
        return out

    n_chunks = d_ff // FF_CHUNK
    group_ends = [(g + 1) * n_chunks // FF_GROUPS for g in range(FF_GROUPS)]
    acc = None
    start = 0
    up(0)
    for ci in range(n_chunks):
        if ci + 1 < n_chunks:
            up(ci + 1)
        cols_a, cols_b = chunk_cols(ci)
        f_ref[:, cols_a] = (jax.nn.gelu(conv(cols_a), approximate=True)
                            * conv(cols_b)).astype(jnp.bfloat16)
        if ci + 1 in group_ends:
            rows = slice(start * FF_CHUNK, (ci + 1) * FF_CHUNK)
            part = jnp.dot(f_ref[:, rows], wdn_ref[rows, :], preferred_element_type=jnp.float32)
            acc = part if acc is None else acc + part
            start = ci + 1
    u_ref[0:HALO, :] = u_ref[tm:tm + HALO, :]
    o_ref[...] = x1 + _rms(acc, g3_ref[...])


def _out_ffn(ya, yr, x2, woa, wor, g1, g2, wup, cw, cb, wdn, g3, seq):
    t, d = x2.shape
    d_ff = wdn.shape[0]
    tm = ROW_TILE
    row = lambda w: pl.BlockSpec((tm, w), lambda i: (i, 0))
    return pl.pallas_call(
        functools.partial(_ffn_kernel, tiles_per_batch=seq // tm, d_ff=d_ff),
        grid=(t // tm,),
        in_specs=[row(ya.shape[1]), row(yr.shape[1]), row(d),
                  _const_spec(woa.shape), _const_spec(wor.shape),
                  _const_spec((1, d)), _const_spec((1, d)), _const_spec(wup.shape),
                  _const_spec(cw.shape), _const_spec(cb.shape), _const_spec(wdn.shape),
                  _const_spec((1, d))],
        out_specs=row(d),
        out_shape=jax.ShapeDtypeStruct((t, d), jnp.float32),
        scratch_shapes=[pltpu.VMEM((HALO + tm, 2 * d_ff), jnp.float32),
                        pltpu.VMEM((tm, d_ff), jnp.bfloat16)],
        compiler_params=pltpu.CompilerParams(
            dimension_semantics=("arbitrary",), vmem_limit_bytes=VMEM_LIMIT),
        name="out_ffn",
    )(ya, yr, x2, woa, wor, g1, g2, wup, cw, cb, wdn, g3)


def _rotary_tables(seq):
    inv = ROPE_BASE ** (-jnp.arange(0, HEAD_DIM, 2, dtype=jnp.float32) / HEAD_DIM)
    ang = jnp.arange(seq, dtype=jnp.float32)[:, None] * inv[None, :]
    cos, sin = jnp.cos(ang), jnp.sin(ang)
    cos_t = jnp.tile(jnp.concatenate([cos, cos], axis=1), (1, PAIR // HEAD_DIM))
    sin_t = jnp.tile(jnp.concatenate([-sin, sin], axis=1), (1, PAIR // HEAD_DIM))
    return cos_t, sin_t


def _decay_tables(n_heads):
    c = RET_CHUNK
    log_gamma = jnp.log(1.0 - 2.0 ** (-5.0 - jnp.arange(n_heads, dtype=jnp.float32)))
    n = jnp.arange(c, dtype=jnp.float32)
    rel = n[:, None] - n[None, :]
    dmask = jnp.where(rel >= 0, jnp.exp(jnp.maximum(rel, 0.0)[None] * log_gamma[:, None, None]), 0.0)
    dend = jnp.exp((c - 1.0 - n)[:, None] * log_gamma[None, :])
    dstart = jnp.exp((n + 1.0)[:, None] * log_gamma[None, :])
    cdec = jnp.exp(c * log_gamma)[None, :]
    wide = lambda a: jnp.repeat(a, HEAD_DIM, axis=1)
    return dmask, wide(dstart), wide(dend), wide(cdec)


def kernel(x, norm_mix_pre, w_in, rel_bias, w_out, norm_mix_post, norm_ffn_pre, w_up, conv_w,
           conv_b, w_down, norm_ffn_post):
    batch, seq, d = x.shape
    depth = w_in.shape[0]
    d_a = w_out.shape[1] // 2
    d_r = w_out.shape[1] - d_a
    assert seq % PROJ_TILE == 0 and PROJ_TILE % MOBA_BLOCK == 0 and MOBA_BLOCK == RET_CHUNK
    assert seq % ROW_TILE == 0 and seq % (RET_SUB * RET_CHUNK) == 0
    assert w_in.shape[2] == 3 * d_a + 4 * d_r and d_a % PAIR == 0 and d_r % PAIR == 0
    d_ff = w_down.shape[1]
    assert d_ff % FF_CHUNK == 0 and d_ff // FF_CHUNK >= FF_GROUPS and w_up.shape[2] == 2 * d_ff
    assert conv_w.shape[1] == CONV_WIDTH <= HALO
    assert d_a % (MOBA_HEADS * HEAD_DIM) == 0 and MOBA_HEADS % 2 == 0
    assert all(t % FAR_RING == 0 for t in FAR_TRIPS) and FAR_AHEAD < FAR_RING

    bf = jnp.bfloat16
    cos_t, sin_t = _rotary_tables(seq)
    dmask, dstart, dend, cdec = _decay_tables(d_r // HEAD_DIM)
    bias = _bias_tables(rel_bias)

    x2 = x.reshape(batch * seq, d)
    for l in range(depth):
        w = w_in[l].astype(bf)
        wqvt = jnp.concatenate([w[:, :d_a], w[:, 2 * d_a:3 * d_a]], axis=1).T
        wk = w[:, d_a:2 * d_a]
        wqkr = w[:, 3 * d_a:3 * d_a + 2 * d_r]
        wvgr = w[:, 3 * d_a + 2 * d_r:]
        qat, ka, vt, gate, qr, kr, vr, gr = _projection(
            x2, norm_mix_pre[l][None], wqvt, wk, wqkr, wvgr, cos_t, sin_t, batch, seq)
        ya = _moba(rel_bias, qat, ka, vt, gate, bias, batch, seq)
        yr = _retention(qr, kr, vr, gr, dmask, dstart, dend, cdec, batch, seq)
        wo = w_out[l].astype(bf)
        x2 = _out_ffn(ya, yr, x2, wo[:d_a], wo[d_a:], norm_mix_post[l][None],
                      norm_ffn_pre[l][None], w_up[l].astype(bf), conv_w[l], conv_b[l][None],
                      w_down[l].astype(bf), norm_ffn_post[l][None], seq)
    return x2.reshape(batch, seq, d)
```

```python
import functools
import math

import jax
import jax.numpy as jnp
from jax import lax
from jax.experimental import pallas as pl
from jax.experimental.pallas import tpu as pltpu

HEAD_DIM = 64
PAIR = 2 * HEAD_DIM
MOBA_BLOCK = 256
MOBA_TOPK = 3
RET_CHUNK = 256
REL_BUCKETS = 32
REL_MAX_DIST = 128
ROPE_BASE = 10000.0
CONV_WIDTH = 3
EPS = 1e-6
NEG = -1e30
LOG2E = math.log2(math.e)
Q_SCALE = HEAD_DIM ** -0.5 * LOG2E
V_ROWS = HEAD_DIM + 16
PROJ_TILE = 1024
ROW_TILE = 512
RET_SUB = 4
MOBA_HEADS = 4
FAR_TRIPS = (16, 8, 4)
FAR_AHEAD = 2
FAR_RING = 4
FF_CHUNK = 256
FF_GROUPS = 2
HALO = 8
VMEM_LIMIT = 56 * 1024 * 1024

NT = (((1,), (1,)), ((), ()))
TN = (((0,), (0,)), ((), ()))


def _rms(x, g):
    return x * lax.rsqrt(jnp.mean(x * x, axis=-1, keepdims=True) + EPS) * g


def _bias_kernel(rb_ref, o_ref):
    h = pl.program_id(0)
    key = lax.broadcasted_iota(jnp.int32, (MOBA_BLOCK, MOBA_BLOCK), 0)
    qry = lax.broadcasted_iota(jnp.int32, (MOBA_BLOCK, MOBA_BLOCK), 1)
    max_exact = REL_BUCKETS // 2
    for which in range(2):
        rel = qry - key + which * MOBA_BLOCK
        n = jnp.maximum(rel, 0)
        n_f = jnp.maximum(n, 1).astype(jnp.float32)
        large = max_exact + (jnp.log(n_f / max_exact) / math.log(REL_MAX_DIST / max_exact)
                             * (REL_BUCKETS - max_exact)).astype(jnp.int32)
        large = jnp.minimum(large, REL_BUCKETS - 1)
        bucket = jnp.where(n < max_exact, n, large)
        val = jnp.zeros((MOBA_BLOCK, MOBA_BLOCK), jnp.float32)
        for b in range(REL_BUCKETS):
            val = jnp.where(bucket == b, rb_ref[b, h], val)
        o_ref[0, which] = jnp.where(rel >= 0, val * LOG2E, NEG)


def _bias_tables(rel_bias):
    n_heads = rel_bias.shape[1]
    return pl.pallas_call(
        _bias_kernel,
        grid=(n_heads,),
        in_specs=[pl.BlockSpec(memory_space=pltpu.SMEM)],
        out_specs=pl.BlockSpec((1, 2, MOBA_BLOCK, MOBA_BLOCK), lambda h: (h, 0, 0, 0)),
        out_shape=jax.ShapeDtypeStruct((n_heads, 2, MOBA_BLOCK, MOBA_BLOCK), jnp.float32),
        name="bias_tables",
    )(rel_bias)


def _proj_kernel(x_ref, g_ref, wqvt_ref, wk_ref, wqkr_ref, wvgr_ref, cos_ref, sin_ref,
                 qat_ref, ka_ref, vt_ref, gate_ref, qr_ref, kr_ref, vr_ref, gr_ref,
                 kmt_ref, *, tiles_per_batch, d_a, d_r):
    t_in_b = pl.program_id(0) % tiles_per_batch
    n_heads = d_a // HEAD_DIM
    tm = x_ref.shape[0]
    n_blocks = tiles_per_batch * (tm // MOBA_BLOCK)

    @pl.when(t_in_b == 0)
    def _():
        kmt_ref[...] = jnp.zeros_like(kmt_ref)

    hb = _rms(x_ref[...], g_ref[...]).astype(jnp.bfloat16)

    qvt = lax.dot_general(wqvt_ref[...], hb, NT, preferred_element_type=jnp.float32)
    qt = qvt[:d_a]
    qat_ref[...] = (qt * Q_SCALE).astype(jnp.bfloat16)
    k = jnp.dot(hb, wk_ref[...], preferred_element_type=jnp.float32)
    ka_ref[...] = k.astype(jnp.bfloat16)

    lane_head = lax.broadcasted_iota(jnp.int32, (1, d_a), 1) // HEAD_DIM
    for half in range(tm // MOBA_BLOCK):
        rows = slice(half * MOBA_BLOCK, (half + 1) * MOBA_BLOCK)
        gate_ref[:, rows] = jnp.dot(
            kmt_ref[...].astype(jnp.bfloat16), qt[:, rows].astype(jnp.bfloat16),
            preferred_element_type=jnp.float32)
        k_mean = jnp.mean(k[rows], axis=0, keepdims=True)
        j = t_in_b * (tm // MOBA_BLOCK) + half
        for hh in range(n_heads):
            kmt_ref[pl.ds(hh * n_blocks + j, 1), :] = jnp.where(lane_head == hh, k_mean, 0.0)

    vt = qvt[d_a:].astype(jnp.bfloat16)
    for hh in range(n_heads):
        vt_ref[hh, 0:HEAD_DIM, :] = vt[hh * HEAD_DIM:(hh + 1) * HEAD_DIM, :]
        vt_ref[hh, HEAD_DIM:V_ROWS, :] = jnp.ones((V_ROWS - HEAD_DIM, tm), jnp.bfloat16)

    qkr = jnp.dot(hb, wqkr_ref[...], preferred_element_type=jnp.float32)
    cos = cos_ref[...]
    sin = sin_ref[...]
    first_half = (lax.broadcasted_iota(jnp.int32, (tm, PAIR), 1) % HEAD_DIM) < HEAD_DIM // 2
    for cidx in range(2 * d_r // PAIR):
        xc = qkr[:, cidx * PAIR:(cidx + 1) * PAIR]
        sw = jnp.where(first_half,
                       pltpu.roll(xc, PAIR - HEAD_DIM // 2, axis=1),
                       pltpu.roll(xc, HEAD_DIM // 2, axis=1))
        rc = xc * cos + sw * sin
        if cidx < d_r // PAIR:
            qr_ref[:, cidx * PAIR:(cidx + 1) * PAIR] = rc.astype(jnp.bfloat16)
        else:
            c2 = cidx - d_r // PAIR
            kr_ref[:, c2 * PAIR:(c2 + 1) * PAIR] = (rc * HEAD_DIM ** -0.5).astype(jnp.bfloat16)

    vg = jnp.dot(hb, wvgr_ref[...], preferred_element_type=jnp.float32)
    vr_ref[...] = vg[:, :d_r].astype(jnp.bfloat16)
    gr_ref[...] = vg[:, d_r:].astype(jnp.bfloat16)


def _const_spec(shape):
    zeros = (0,) * len(shape)
    return pl.BlockSpec(shape, lambda *_: zeros, pipeline_mode=pl.Buffered(1))


def _projection(x2, g, wqvt, wk, wqkr, wvgr, cos_t, sin_t, batch, seq):
    t, d = x2.shape
    d_a = wk.shape[1]
    d_r = wqkr.shape[1] // 2
    n_heads = d_a // HEAD_DIM
    tm = PROJ_TILE
    tpb = seq // tm
    n_blocks = seq // MOBA_BLOCK
    row = lambda w: pl.BlockSpec((tm, w), lambda i: (i, 0))
    bf = jnp.bfloat16
    return pl.pallas_call(
        functools.partial(_proj_kernel, tiles_per_batch=tpb, d_a=d_a, d_r=d_r),
        grid=(t // tm,),
        in_specs=[row(d), _const_spec((1, d)), _const_spec(wqvt.shape), _const_spec(wk.shape),
                  _const_spec(wqkr.shape), _const_spec(wvgr.shape),
                  pl.BlockSpec((tm, PAIR), lambda i: (i % tpb, 0)),
                  pl.BlockSpec((tm, PAIR), lambda i: (i % tpb, 0))],
        out_specs=[pl.BlockSpec((None, d_a, tm), lambda i: (i // tpb, 0, i % tpb)), row(d_a),
                   pl.BlockSpec((None, n_heads, V_ROWS, tm), lambda i: (i // tpb, 0, 0, i % tpb)),
                   pl.BlockSpec((None, n_heads * n_blocks, tm), lambda i: (i // tpb, 0, i % tpb)),
                   row(d_r), row(d_r), row(d_r), row(d_r)],
        out_shape=[jax.ShapeDtypeStruct((batch, d_a, seq), bf), jax.ShapeDtypeStruct((t, d_a), bf),
                   jax.ShapeDtypeStruct((batch, n_heads, V_ROWS, seq), bf),
                   jax.ShapeDtypeStruct((batch, n_heads * n_blocks, seq), jnp.float32),
                   jax.ShapeDtypeStruct((t, d_r), bf), jax.ShapeDtypeStruct((t, d_r), bf),
                   jax.ShapeDtypeStruct((t, d_r), bf), jax.ShapeDtypeStruct((t, d_r), bf)],
        scratch_shapes=[pltpu.VMEM((n_heads * n_blocks, d_a), jnp.float32)],
        compiler_params=pltpu.CompilerParams(
            dimension_semantics=("arbitrary",), vmem_limit_bytes=VMEM_LIMIT),
        name="projection",
    )(x2, g, wqvt, wk, wqkr, wvgr, cos_t, sin_t)


def _moba_kernel(rb_ref, qt_ref, k_ref, vt_ref, gate_ref, bias_ref, o_ref, sel_ref, s_ref, mj_ref,
                 p_ref, acc_ref, *, n_blocks):
    nh = vt_ref.shape[0]
    heads = range(nh)
    group = pl.program_id(1)
    c = pl.program_id(2)

    def pair_cols(h):
        return slice((h // 2) * PAIR, (h // 2 + 1) * PAIR)

    feat = lax.broadcasted_iota(jnp.int32, (PAIR, MOBA_BLOCK), 0)
    qm = []
    for h in heads:
        qt = qt_ref[pair_cols(h), :]
        own_feat = (feat >= HEAD_DIM) if h % 2 else (feat < HEAD_DIM)
        qm.append(jnp.where(own_feat, qt, jnp.zeros_like(qt)))

    slot_prev, slot_own = FAR_RING, FAR_RING + 1
    far_bias = [rb_ref[REL_BUCKETS - 1, nh * group + h] * LOG2E for h in heads]
    prev = jnp.maximum(c - 1, 0)
    n_far = prev
    last = jnp.maximum(n_far - 1, 0)

    def block_rows(j):
        return pl.ds(pl.multiple_of(j * MOBA_BLOCK, MOBA_BLOCK), MOBA_BLOCK)

    def scores(slot, j, bias_index=None):
        for h in heads:
            s = jnp.dot(k_ref[block_rows(j), pair_cols(h)], qm[h],
                        preferred_element_type=jnp.float32)
            if bias_index is not None:
                s = s + bias_ref[h, bias_index]
            s_ref[slot, h] = s.astype(jnp.bfloat16)
            mj_ref[slot, h] = jnp.max(s, axis=0, keepdims=True)

    def prob(slot, h):
        return jnp.exp2(s_ref[slot, h] - mj_ref[slot, h].astype(jnp.bfloat16))

    def weigh(j, h, p):
        return jnp.dot(vt_ref[h, :, block_rows(j)], p, preferred_element_type=jnp.float32)

    def weighted(j, slot):
        return [weigh(j, h, prob(slot, h)) for h in heads]

    def merge(states, mj, oj, j, valid, always=False):
        out = []
        for h in heads:
            m = states[h]
            if always:
                m_new = jnp.maximum(m, mj[h])
                beta = jnp.exp2(mj[h] - m_new)
            else:
                chosen = (sel_ref[h, pl.ds(j, 1), :] > 0.5) & valid
                m_new = jnp.where(chosen, jnp.maximum(m, mj[h]), m)
                beta = jnp.where(chosen, jnp.exp2(mj[h] - m_new), 0.0)
            acc_ref[h] = acc_ref[h] * jnp.exp2(m - m_new) + oj[h] * beta
            out.append(m_new)
        return tuple(out)

    scores(slot_own, c, 0)
    scores(slot_prev, prev, 1)
    for i in range(FAR_AHEAD):
        scores(i, jnp.minimum(i, last))

    blk = lax.broadcasted_iota(jnp.int32, (n_blocks, MOBA_BLOCK), 0)
    for h in heads:
        g = jnp.where(blk < c, gate_ref[h * n_blocks:(h + 1) * n_blocks, :], -jnp.inf)
        sel = jnp.zeros(g.shape, jnp.float32)
        for _ in range(MOBA_TOPK):
            top = jnp.max(g, axis=0, keepdims=True)
            idx = jnp.min(jnp.where(g == top, blk, n_blocks), axis=0, keepdims=True)
            pick = (blk == idx) & (blk < c)
            sel = jnp.where(pick, 1.0, sel)
            g = jnp.where(pick, -jnp.inf, g)
        sel_ref[h] = sel

    states = tuple(jnp.full((1, MOBA_BLOCK), NEG, jnp.float32) for _ in heads)
    acc_ref[...] = jnp.zeros_like(acc_ref)

    p_ref[...] = jnp.zeros_like(p_ref)
    pend_m = tuple(jnp.zeros((1, MOBA_BLOCK), jnp.float32) for _ in heads)

    def trip(unroll, first):
        def body(t, carry):
            states, pend_m, pend_j, pend_valid = carry
            base = first + unroll * t
            js = [jnp.minimum(base + i, last) for i in range(unroll + FAR_AHEAD)]
            pending = (pend_m, [weigh(pend_j, h, p_ref[h]) for h in heads], pend_j,
                       pend_valid > 0)
            for i in range(unroll):
                scores((i + FAR_AHEAD) % FAR_RING, js[i + FAR_AHEAD])
                m_i = tuple(mj_ref[i % FAR_RING, h] + far_bias[h] for h in heads)
                if i + 1 < unroll:
                    o_i = weighted(js[i], i % FAR_RING)
                else:
                    for h in heads:
                        p_ref[h] = prob(i % FAR_RING, h)
                states = merge(states, *pending)
                if i + 1 < unroll:
                    pending = (m_i, o_i, js[i], base + i < n_far)
            return (states, m_i, js[unroll - 1], (base + unroll - 1 < n_far).astype(jnp.int32))
        return body

    carry = (states, pend_m, jnp.int32(0), jnp.int32(0))
    done = 0
    for unroll in FAR_TRIPS:
        left = n_far - done
        n_trips = left // unroll if unroll != FAR_TRIPS[-1] else (left + unroll - 1) // unroll
        carry = lax.fori_loop(0, n_trips, trip(unroll, done), carry)
        done = done + n_trips * unroll
    states, pend_m, pend_j, pend_valid = carry
    o_pend = [weigh(pend_j, h, p_ref[h]) for h in heads]
    o_prev = weighted(prev, slot_prev)
    o_own = weighted(c, slot_own)
    states = merge(states, pend_m, o_pend, pend_j, pend_valid > 0)
    states = merge(states, [mj_ref[slot_prev, h] for h in heads], o_prev, prev, True)
    states = merge(states, [mj_ref[slot_own, h] for h in heads], o_own, c, True, always=True)

    outs = [(acc_ref[h, :HEAD_DIM, :] / acc_ref[h, HEAD_DIM:HEAD_DIM + 1, :]).T for h in heads]
    o_ref[...] = jnp.concatenate(outs, axis=1).astype(o_ref.dtype)


def _moba(rel_bias, qat, ka, vt, gate, bias, batch, seq):
    d_a = ka.shape[1]
    n_blocks = seq // MOBA_BLOCK
    ka3 = ka.reshape(batch, seq, d_a)
    nh = MOBA_HEADS
    width = nh * HEAD_DIM
    out = pl.pallas_call(
        functools.partial(_moba_kernel, n_blocks=n_blocks),
        grid=(batch, d_a // width, n_blocks),
        in_specs=[pl.BlockSpec(memory_space=pltpu.SMEM),
                  pl.BlockSpec((None, width, MOBA_BLOCK), lambda b, g, c: (b, g, c)),
                  pl.BlockSpec((None, seq, width), lambda b, g, c: (b, 0, g)),
                  pl.BlockSpec((None, nh, V_ROWS, seq), lambda b, g, c: (b, g, 0, 0)),
                  pl.BlockSpec((None, nh * n_blocks, MOBA_BLOCK), lambda b, g, c: (b, g, c)),
                  pl.BlockSpec((nh, 2, MOBA_BLOCK, MOBA_BLOCK), lambda b, g, c: (g, 0, 0, 0),
                               pipeline_mode=pl.Buffered(1))],
        out_specs=pl.BlockSpec((None, MOBA_BLOCK, width), lambda b, g, c: (b, c, g)),
        out_shape=jax.ShapeDtypeStruct((batch, seq, d_a), jnp.bfloat16),
        scratch_shapes=[pltpu.VMEM((nh, n_blocks, MOBA_BLOCK), jnp.float32),
                        pltpu.VMEM((FAR_RING + 2, nh, MOBA_BLOCK, MOBA_BLOCK), jnp.bfloat16),
                        pltpu.VMEM((FAR_RING + 2, nh, 1, MOBA_BLOCK), jnp.float32),
                        pltpu.VMEM((nh, MOBA_BLOCK, MOBA_BLOCK), jnp.bfloat16),
                        pltpu.VMEM((nh, V_ROWS, MOBA_BLOCK), jnp.float32)],
        compiler_params=pltpu.CompilerParams(
            dimension_semantics=("arbitrary", "arbitrary", "arbitrary"),
            vmem_limit_bytes=VMEM_LIMIT),
        name="moba",
    )(rel_bias, qat, ka3, vt, gate, bias)
    return out.reshape(batch * seq, d_a)


def _ret_kernel(q_ref, k_ref, v_ref, g_ref, dmask_ref, dstart_ref, dend_ref, cdec_ref,
                o_ref, state_ref):
    @pl.when(pl.program_id(1) == 0)
    def _():
        state_ref[...] = jnp.zeros_like(state_ref)

    n_pairs = state_ref.shape[0]
    bf = jnp.bfloat16
    f32 = jnp.float32
    items = [(p, s) for p in range(n_pairs) for s in range(RET_SUB)]
    cols = [slice(p * PAIR, (p + 1) * PAIR) for p in range(n_pairs)]
    rows = [slice(s * RET_CHUNK, (s + 1) * RET_CHUNK) for s in range(RET_SUB)]
    lane = lax.broadcasted_iota(jnp.int32, (RET_CHUNK, PAIR), 1)
    in_head = [lane < HEAD_DIM, lane >= HEAD_DIM]
    r = lax.broadcasted_iota(jnp.int32, (PAIR, PAIR), 0) // HEAD_DIM
    cc = lax.broadcasted_iota(jnp.int32, (PAIR, PAIR), 1) // HEAD_DIM
    same_head = r == cc
    avg = jnp.where(same_head, 1.0 / HEAD_DIM, 0.0).astype(bf)

    q = {(p, s): q_ref[rows[s], cols[p]] for p, s in items}
    k = {(p, s): k_ref[rows[s], cols[p]] for p, s in items}
    v = {(p, s): v_ref[rows[s], cols[p]] for p, s in items}

    sc = {i: [lax.dot_general(jnp.where(in_head[h], q[i], jnp.zeros_like(q[i])), k[i], NT,
                              preferred_element_type=f32) for h in range(2)] for i in items}
    kv = {i: lax.dot_general((k[i].astype(f32) * dend_ref[:, cols[i[0]]]).astype(bf), v[i], TN,
                             preferred_element_type=f32) for i in items}
    cross = {}
    for p in range(n_pairs):
        state = state_ref[p]
        for s in range(RET_SUB):
            cross[p, s] = jnp.dot(q[p, s], state.astype(bf), preferred_element_type=f32)
            state = jnp.where(same_head, state * cdec_ref[:, cols[p]] + kv[p, s], 0.0)
        state_ref[p] = state

    y = {}
    for p, s in items:
        yh = [jnp.dot((sc[p, s][h] * dmask_ref[2 * p + h]).astype(bf), v[p, s],
                      preferred_element_type=f32) for h in range(2)]
        y[p, s] = jnp.where(in_head[0], yh[0], yh[1]) + cross[p, s] * dstart_ref[:, cols[p]]

    def head_mean(t):
        return jnp.dot(t.astype(bf), avg, preferred_element_type=f32)

    y_hi = {i: y[i].astype(bf) for i in items}
    mu = {i: jnp.dot(y_hi[i], avg, preferred_element_type=f32)
          + head_mean(y[i] - y_hi[i].astype(f32)) for i in items}
    d = {i: y[i] - mu[i] for i in items}
    var = {i: head_mean(d[i] * d[i]) for i in items}
    for p, s in items:
        g = g_ref[rows[s], cols[p]].astype(f32)
        o_ref[rows[s], cols[p]] = (g * jax.nn.sigmoid(g) * d[p, s]
                                   * lax.rsqrt(var[p, s] + EPS)).astype(bf)


def _retention(qr, kr, vr, gr, dmask, dstart, dend, cdec, batch, seq):
    d_r = qr.shape[1]
    n_steps = seq // (RET_SUB * RET_CHUNK)
    blk = pl.BlockSpec((None, RET_SUB * RET_CHUNK, d_r), lambda b, c: (b, c, 0))
    r3 = lambda a: a.reshape(batch, seq, d_r)
    out = pl.pallas_call(
        _ret_kernel,
        grid=(batch, n_steps),
        in_specs=[blk, blk, blk, blk, _const_spec(dmask.shape), _const_spec(dstart.shape),
                  _const_spec(dend.shape), _const_spec(cdec.shape)],
        out_specs=blk,
        out_shape=jax.ShapeDtypeStruct((batch, seq, d_r), jnp.bfloat16),
        scratch_shapes=[pltpu.VMEM((d_r // PAIR, PAIR, PAIR), jnp.float32)],
        compiler_params=pltpu.CompilerParams(
            dimension_semantics=("arbitrary", "arbitrary"), vmem_limit_bytes=VMEM_LIMIT),
        name="retention",
    )(r3(qr), r3(kr), r3(vr), r3(gr), dmask, dstart, dend, cdec)
    return out.reshape(batch * seq, d_r)


def _ffn_kernel(ya_ref, yr_ref, x_ref, woa_ref, wor_ref, g1_ref, g2_ref, wup_ref, cw_ref, cb_ref,
                wdn_ref, g3_ref, o_ref, u_ref, f_ref, *, tiles_per_batch, d_ff):
    tm = x_ref.shape[0]

    @pl.when(pl.program_id(0) % tiles_per_batch == 0)
    def _():
        u_ref[0:HALO, :] = jnp.zeros((HALO, u_ref.shape[1]), jnp.float32)

    y = (jnp.dot(ya_ref[...], woa_ref[...], preferred_element_type=jnp.float32)
         + jnp.dot(yr_ref[...], wor_ref[...], preferred_element_type=jnp.float32))
    x1 = x_ref[...] + _rms(y, g1_ref[...])
    h2 = _rms(x1, g2_ref[...]).astype(jnp.bfloat16)

    def chunk_cols(ci):
        lo = ci * FF_CHUNK
        return slice(lo, lo + FF_CHUNK), slice(d_ff + lo, d_ff + lo + FF_CHUNK)

    def up(ci):
        for cols in chunk_cols(ci):
            u_ref[HALO:HALO + tm, cols] = jnp.dot(h2, wup_ref[:, cols],
                                                  preferred_element_type=jnp.float32)

    def conv(cols):
        w = cw_ref[:, cols]
        out = cb_ref[:, cols]
        for j in reversed(range(CONV_WIDTH)):
            lo = HALO - (CONV_WIDTH - 1 - j)
            out = out + w[j:j + 1] * u_ref[lo:lo + tm, cols]
        return out

    n_chunks = d_ff // FF_CHUNK
    group_ends = [(g + 1) * n_chunks // FF_GROUPS for g in range(FF_GROUPS)]
    acc = None
    start = 0
    up(0)
    for ci in range(n_chunks):
        if ci + 1 < n_chunks:
            up(ci + 1)
        cols_a, cols_b = chunk_cols(ci)
        f_ref[:, cols_a] = (jax.nn.gelu(conv(cols_a), approximate=True)
                            * conv(cols_b)).astype(jnp.bfloat16)
        if ci + 1 in group_ends:
            rows = slice(start * FF_CHUNK, (ci + 1) * FF_CHUNK)
            part = jnp.dot(f_ref[:, rows], wdn_ref[rows, :], preferred_element_type=jnp.float32)
            acc = part if acc is None else acc + part
            start = ci + 1
    u_ref[0:HALO, :] = u_ref[tm:tm + HALO, :]
    o_ref[...] = x1 + _rms(acc, g3_ref[...])


def _out_ffn(ya, yr, x2, woa, wor, g1, g2, wup, cw, cb, wdn, g3, seq):
    t, d = x2.shape
    d_ff = wdn.shape[0]
    tm = ROW_TILE
    row = lambda w: pl.BlockSpec((tm, w), lambda i: (i, 0))
    return pl.pallas_call(
        functools.partial(_ffn_kernel, tiles_per_batch=seq // tm, d_ff=d_ff),
        grid=(t // tm,),
        in_specs=[row(ya.shape[1]), row(yr.shape[1]), row(d),
                  _const_spec(woa.shape), _const_spec(wor.shape),
                  _const_spec((1, d)), _const_spec((1, d)), _const_spec(wup.shape),
                  _const_spec(cw.shape), _const_spec(cb.shape), _const_spec(wdn.shape),
                  _const_spec((1, d))],
        out_specs=row(d),
        out_shape=jax.ShapeDtypeStruct((t, d), jnp.float32),
        scratch_shapes=[pltpu.VMEM((HALO + tm, 2 * d_ff), jnp.float32),
                        pltpu.VMEM((tm, d_ff), jnp.bfloat16)],
        compiler_params=pltpu.CompilerParams(
            dimension_semantics=("arbitrary",), vmem_limit_bytes=VMEM_LIMIT),
        name="out_ffn",
    )(ya, yr, x2, woa, wor, g1, g2, wup, cw, cb, wdn, g3)


def _rotary_tables(seq):
    inv = ROPE_BASE ** (-jnp.arange(0, HEAD_DIM, 2, dtype=jnp.float32) / HEAD_DIM)
    ang = jnp.arange(seq, dtype=jnp.float32)[:, None] * inv[None, :]
    cos, sin = jnp.cos(ang), jnp.sin(ang)
    cos_t = jnp.tile(jnp.concatenate([cos, cos], axis=1), (1, PAIR // HEAD_DIM))
    sin_t = jnp.tile(jnp.concatenate([-sin, sin], axis=1), (1, PAIR // HEAD_DIM))
    return cos_t, sin_t


def _decay_tables(n_heads):
    c = RET_CHUNK
    log_gamma = jnp.log(1.0 - 2.0 ** (-5.0 - jnp.arange(n_heads, dtype=jnp.float32)))
    n = jnp.arange(c, dtype=jnp.float32)
    rel = n[:, None] - n[None, :]
    dmask = jnp.where(rel >= 0, jnp.exp(jnp.maximum(rel, 0.0)[None] * log_gamma[:, None, None]), 0.0)
    dend = jnp.exp((c - 1.0 - n)[:, None] * log_gamma[None, :])
    dstart = jnp.exp((n + 1.0)[:, None] * log_gamma[None, :])
    cdec = jnp.exp(c * log_gamma)[None, :]
    wide = lambda a: jnp.repeat(a, HEAD_DIM, axis=1)
    return dmask, wide(dstart), wide(dend), wide(cdec)


def kernel(x, norm_mix_pre, w_in, rel_bias, w_out, norm_mix_post, norm_ffn_pre, w_up, conv_w,
           conv_b, w_down, norm_ffn_post):
    batch, seq, d = x.shape
    depth = w_in.shape[0]
    d_a = w_out.shape[1] // 2
    d_r = w_out.shape[1] - d_a
    assert seq % PROJ_TILE == 0 and PROJ_TILE % MOBA_BLOCK == 0 and MOBA_BLOCK == RET_CHUNK
    assert seq % ROW_TILE == 0 and seq % (RET_SUB * RET_CHUNK) == 0
    assert w_in.shape[2] == 3 * d_a + 4 * d_r and d_a % PAIR == 0 and d_r % PAIR == 0
    d_ff = w_down.shape[1]
    assert d_ff % FF_CHUNK == 0 and d_ff // FF_CHUNK >= FF_GROUPS and w_up.shape[2] == 2 * d_ff
    assert conv_w.shape[1] == CONV_WIDTH <= HALO
    assert d_a % (MOBA_HEADS * HEAD_DIM) == 0 and MOBA_HEADS % 2 == 0
    assert all(t % FAR_RING == 0 for t in FAR_TRIPS) and FAR_AHEAD < FAR_RING

    bf = jnp.bfloat16
    cos_t, sin_t = _rotary_tables(seq)
    dmask, dstart, dend, cdec = _decay_tables(d_r // HEAD_DIM)
    bias = _bias_tables(rel_bias)

    x2 = x.reshape(batch * seq, d)
    for l in range(depth):
        w = w_in[l].astype(bf)
        wqvt = jnp.concatenate([w[:, :d_a], w[:, 2 * d_a:3 * d_a]], axis=1).T
        wk = w[:, d_a:2 * d_a]
        wqkr = w[:, 3 * d_a:3 * d_a + 2 * d_r]
        wvgr = w[:, 3 * d_a + 2 * d_r:]
        qat, ka, vt, gate, qr, kr, vr, gr = _projection(
            x2, norm_mix_pre[l][None], wqvt, wk, wqkr, wvgr, cos_t, sin_t, batch, seq)
        ya = _moba(rel_bias, qat, ka, vt, gate, bias, batch, seq)
        yr = _retention(qr, kr, vr, gr, dmask, dstart, dend, cdec, batch, seq)
        wo = w_out[l].astype(bf)
        x2 = _out_ffn(ya, yr, x2, wo[:d_a], wo[d_a:], norm_mix_post[l][None],
                      norm_ffn_pre[l][None], w_up[l].astype(bf), conv_w[l], conv_b[l][None],
                      w_down[l].astype(bf), norm_ffn_post[l][None], seq)
    return x2.reshape(batch, seq, d)
```

```python
import functools
import math

import jax
import jax.numpy as jnp
from jax import lax
from jax.experimental import pallas as pl
from jax.experimental.pallas import tpu as pltpu

HEAD_DIM = 64
PAIR = 2 * HEAD_DIM
MOBA_BLOCK = 256
MOBA_TOPK = 3
RET_CHUNK = 256
REL_BUCKETS = 32
REL_MAX_DIST = 128
ROPE_BASE = 10000.0
CONV_WIDTH = 3
EPS = 1e-6
NEG = -1e30
LOG2E = math.log2(math.e)
Q_SCALE = HEAD_DIM ** -0.5 * LOG2E
V_ROWS = HEAD_DIM + 16
PROJ_TILE = 1024
ROW_TILE = 512
RET_SUB = 4
MOBA_HEADS = 4
FAR_TRIPS = (16, 8, 4)
FAR_AHEAD = 2
FAR_RING = 4
FF_CHUNK = 256
FF_GROUPS = 2
HALO = 8
VMEM_LIMIT = 56 * 1024 * 1024

NT = (((1,), (1,)), ((), ()))
TN = (((0,), (0,)), ((), ()))


def _rms(x, g):
    return x * lax.rsqrt(jnp.mean(x * x, axis=-1, keepdims=True) + EPS) * g


def _bias_kernel(rb_ref, o_ref):
    h = pl.program_id(0)
    key = lax.broadcasted_iota(jnp.int32, (MOBA_BLOCK, MOBA_BLOCK), 0)
    qry = lax.broadcasted_iota(jnp.int32, (MOBA_BLOCK, MOBA_BLOCK), 1)
    max_exact = REL_BUCKETS // 2
    for which in range(2):
        rel = qry - key + which * MOBA_BLOCK
        n = jnp.maximum(rel, 0)
        n_f = jnp.maximum(n, 1).astype(jnp.float32)
        large = max_exact + (jnp.log(n_f / max_exact) / math.log(REL_MAX_DIST / max_exact)
                             * (REL_BUCKETS - max_exact)).astype(jnp.int32)
        large = jnp.minimum(large, REL_BUCKETS - 1)
        bucket = jnp.where(n < max_exact, n, large)
        val = jnp.zeros((MOBA_BLOCK, MOBA_BLOCK), jnp.float32)
        for b in range(REL_BUCKETS):
            val = jnp.where(bucket == b, rb_ref[b, h], val)
        o_ref[0, which] = jnp.where(rel >= 0, val * LOG2E, NEG)


def _bias_tables(rel_bias):
    n_heads = rel_bias.shape[1]
    return pl.pallas_call(
        _bias_kernel,
        grid=(n_heads,),
        in_specs=[pl.BlockSpec(memory_space=pltpu.SMEM)],
        out_specs=pl.BlockSpec((1, 2, MOBA_BLOCK, MOBA_BLOCK), lambda h: (h, 0, 0, 0)),
        out_shape=jax.ShapeDtypeStruct((n_heads, 2, MOBA_BLOCK, MOBA_BLOCK), jnp.float32),
        name="bias_tables",
    )(rel_bias)


def _proj_kernel(x_ref, g_ref, wqvt_ref, wk_ref, wqkr_ref, wvgr_ref, cos_ref, sin_ref,
                 qat_ref, ka_ref, vt_ref, gate_ref, qr_ref, kr_ref, vr_ref, gr_ref,
                 kmt_ref, *, tiles_per_batch, d_a, d_r):
    t_in_b = pl.program_id(0) % tiles_per_batch
    n_heads = d_a // HEAD_DIM
    tm = x_ref.shape[0]
    n_blocks = tiles_per_batch * (tm // MOBA_BLOCK)

    @pl.when(t_in_b == 0)
    def _():
        kmt_ref[...] = jnp.zeros_like(kmt_ref)

    hb = _rms(x_ref[...], g_ref[...]).astype(jnp.bfloat16)

    qvt = lax.dot_general(wqvt_ref[...], hb, NT, preferred_element_type=jnp.float32)
    qt = qvt[:d_a]
    qat_ref[...] = (qt * Q_SCALE).astype(jnp.bfloat16)
    k = jnp.dot(hb, wk_ref[...], preferred_element_type=jnp.float32)
    ka_ref[...] = k.astype(jnp.bfloat16)

    lane_head = lax.broadcasted_iota(jnp.int32, (1, d_a), 1) // HEAD_DIM
    for half in range(tm // MOBA_BLOCK):
        rows = slice(half * MOBA_BLOCK, (half + 1) * MOBA_BLOCK)
        gate_ref[:, rows] = jnp.dot(
            kmt_ref[...].astype(jnp.bfloat16), qt[:, rows].astype(jnp.bfloat16),
            preferred_element_type=jnp.float32)
        k_mean = jnp.mean(k[rows], axis=0, keepdims=True)
        j = t_in_b * (tm // MOBA_BLOCK) + half
        for hh in range(n_heads):
            kmt_ref[pl.ds(hh * n_blocks + j, 1), :] = jnp.where(lane_head == hh, k_mean, 0.0)

    vt = qvt[d_a:].astype(jnp.bfloat16)
    for hh in range(n_heads):
        vt_ref[hh, 0:HEAD_DIM, :] = vt[hh * HEAD_DIM:(hh + 1) * HEAD_DIM, :]
        vt_ref[hh, HEAD_DIM:V_ROWS, :] = jnp.ones((V_ROWS - HEAD_DIM, tm), jnp.bfloat16)

    qkr = jnp.dot(hb, wqkr_ref[...], preferred_element_type=jnp.float32)
    cos = cos_ref[...]
    sin = sin_ref[...]
    first_half = (lax.broadcasted_iota(jnp.int32, (tm, PAIR), 1) % HEAD_DIM) < HEAD_DIM // 2
    for cidx in range(2 * d_r // PAIR):
        xc = qkr[:, cidx * PAIR:(cidx + 1) * PAIR]
        sw = jnp.where(first_half,
                       pltpu.roll(xc, PAIR - HEAD_DIM // 2, axis=1),
                       pltpu.roll(xc, HEAD_DIM // 2, axis=1))
        rc = xc * cos + sw * sin
        if cidx < d_r // PAIR:
            qr_ref[:, cidx * PAIR:(cidx + 1) * PAIR] = rc.astype(jnp.bfloat16)
        else:
            c2 = cidx - d_r // PAIR
            kr_ref[:, c2 * PAIR:(c2 + 1) * PAIR] = (rc * HEAD_DIM ** -0.5).astype(jnp.bfloat16)

    vg = jnp.dot(hb, wvgr_ref[...], preferred_element_type=jnp.float32)
    vr_ref[...] = vg[:, :d_r].astype(jnp.bfloat16)
    gr_ref[...] = vg[:, d_r:].astype(jnp.bfloat16)


def _const_spec(shape):
    zeros = (0,) * len(shape)
    return pl.BlockSpec(shape, lambda *_: zeros, pipeline_mode=pl.Buffered(1))


def _projection(x2, g, wqvt, wk, wqkr, wvgr, cos_t, sin_t, batch, seq):
    t, d = x2.shape
    d_a = wk.shape[1]
    d_r = wqkr.shape[1] // 2
    n_heads = d_a // HEAD_DIM
    tm = PROJ_TILE
    tpb = seq // tm
    n_blocks = seq // MOBA_BLOCK
    row = lambda w: pl.BlockSpec((tm, w), lambda i: (i, 0))
    bf = jnp.bfloat16
    return pl.pallas_call(
        functools.partial(_proj_kernel, tiles_per_batch=tpb, d_a=d_a, d_r=d_r),
        grid=(t // tm,),
        in_specs=[row(d), _const_spec((1, d)), _const_spec(wqvt.shape), _const_spec(wk.shape),
                  _const_spec(wqkr.shape), _const_spec(wvgr.shape),
                  pl.BlockSpec((tm, PAIR), lambda i: (i % tpb, 0)),
                  pl.BlockSpec((tm, PAIR), lambda i: (i % tpb, 0))],
        out_specs=[pl.BlockSpec((None, d_a, tm), lambda i: (i // tpb, 0, i % tpb)), row(d_a),
                   pl.BlockSpec((None, n_heads, V_ROWS, tm), lambda i: (i // tpb, 0, 0, i % tpb)),
                   pl.BlockSpec((None, n_heads * n_blocks, tm), lambda i: (i // tpb, 0, i % tpb)),
                   row(d_r), row(d_r), row(d_r), row(d_r)],
        out_shape=[jax.ShapeDtypeStruct((batch, d_a, seq), bf), jax.ShapeDtypeStruct((t, d_a), bf),
                   jax.ShapeDtypeStruct((batch, n_heads, V_ROWS, seq), bf),
                   jax.ShapeDtypeStruct((batch, n_heads * n_blocks, seq), jnp.float32),
                   jax.ShapeDtypeStruct((t, d_r), bf), jax.ShapeDtypeStruct((t, d_r), bf),
                   jax.ShapeDtypeStruct((t, d_r), bf), jax.ShapeDtypeStruct((t, d_r), bf)],
        scratch_shapes=[pltpu.VMEM((n_heads * n_blocks, d_a), jnp.float32)],
        compiler_params=pltpu.CompilerParams(
            dimension_semantics=("arbitrary",), vmem_limit_bytes=VMEM_LIMIT),
        name="projection",
    )(x2, g, wqvt, wk, wqkr, wvgr, cos_t, sin_t)


def _moba_kernel(rb_ref, qt_ref, k_ref, vt_ref, gate_ref, bias_ref, o_ref, sel_ref, s_ref, mj_ref,
                 p_ref, acc_ref, sn_ref, off_ref, *, n_blocks):
    nh = vt_ref.shape[0]
    heads = range(nh)
    group = pl.program_id(1)
    c = pl.program_id(2)

    def pair_cols(h):
        return slice((h // 2) * PAIR, (h // 2 + 1) * PAIR)

    feat = lax.broadcasted_iota(jnp.int32, (PAIR, MOBA_BLOCK), 0)
    qm = []
    for h in heads:
        qt = qt_ref[pair_cols(h), :]
        own_feat = (feat >= HEAD_DIM) if h % 2 else (feat < HEAD_DIM)
        qm.append(jnp.where(own_feat, qt, jnp.zeros_like(qt)))

    slot_prev, slot_own = FAR_RING, FAR_RING + 1
    far_bias = [rb_ref[REL_BUCKETS - 1, nh * group + h] * LOG2E for h in heads]
    prev = jnp.maximum(c - 1, 0)
    n_far = prev
    last = jnp.maximum(n_far - 1, 0)

    def block_rows(j):
        return pl.ds(pl.multiple_of(j * MOBA_BLOCK, MOBA_BLOCK), MOBA_BLOCK)

    def scores(slot, j, bias_index=None, like=None):
        for h in heads:
            s = jnp.dot(k_ref[block_rows(j), pair_cols(h)], qm[h],
                        preferred_element_type=jnp.float32)
            if bias_index is not None:
                s = s + bias_ref[h, bias_index]
                sn_ref[slot - FAR_RING, h] = s
            else:
                off = mj_ref[like, h]
                s_ref[slot, h] = (s - off).astype(jnp.bfloat16)
                off_ref[slot, h] = off
            mj_ref[slot, h] = jnp.max(s, axis=0, keepdims=True)

    def prob(slot, h):
        if slot >= FAR_RING:
            return jnp.exp2((sn_ref[slot - FAR_RING, h] - mj_ref[slot, h]).astype(jnp.bfloat16))
        return jnp.exp2(s_ref[slot, h] - (mj_ref[slot, h] - off_ref[slot, h]).astype(jnp.bfloat16))

    def weigh(j, h, p):
        return jnp.dot(vt_ref[h, :, block_rows(j)], p, preferred_element_type=jnp.float32)

    def weighted(j, slot):
        return [weigh(j, h, prob(slot, h)) for h in heads]

    def merge(states, mj, oj, j, valid, always=False):
        out = []
        for h in heads:
            m = states[h]
            if always:
                m_new = jnp.maximum(m, mj[h])
                beta = jnp.exp2(mj[h] - m_new)
            else:
                chosen = (sel_ref[h, pl.ds(j, 1), :] > 0.5) & valid
                m_new = jnp.where(chosen, jnp.maximum(m, mj[h]), m)
                beta = jnp.where(chosen, jnp.exp2(mj[h] - m_new), 0.0)
            acc_ref[h] = acc_ref[h] * jnp.exp2(m - m_new) + oj[h] * beta
            out.append(m_new)
        return tuple(out)

    scores(slot_own, c, 0)
    scores(slot_prev, prev, 1)
    for i in range(FAR_AHEAD):
        scores(i, jnp.minimum(i, last), like=slot_prev if i == 0 else i - 1)

    blk = lax.broadcasted_iota(jnp.int32, (n_blocks, MOBA_BLOCK), 0)
    for h in heads:
        g = jnp.where(blk < c, gate_ref[h * n_blocks:(h + 1) * n_blocks, :], -jnp.inf)
        sel = jnp.zeros(g.shape, jnp.float32)
        for _ in range(MOBA_TOPK):
            top = jnp.max(g, axis=0, keepdims=True)
            idx = jnp.min(jnp.where(g == top, blk, n_blocks), axis=0, keepdims=True)
            pick = (blk == idx) & (blk < c)
            sel = jnp.where(pick, 1.0, sel)
            g = jnp.where(pick, -jnp.inf, g)
        sel_ref[h] = sel

    states = tuple(jnp.full((1, MOBA_BLOCK), NEG, jnp.float32) for _ in heads)
    acc_ref[...] = jnp.zeros_like(acc_ref)

    p_ref[...] = jnp.zeros_like(p_ref)
    pend_m = tuple(jnp.zeros((1, MOBA_BLOCK), jnp.float32) for _ in heads)

    def trip(unroll, first):
        def body(t, carry):
            states, pend_m, pend_j, pend_valid = carry
            base = first + unroll * t
            js = [jnp.minimum(base + i, last) for i in range(unroll + FAR_AHEAD)]
            pending = (pend_m, [weigh(pend_j, h, p_ref[h]) for h in heads], pend_j,
                       pend_valid > 0)
            for i in range(unroll):
                scores((i + FAR_AHEAD) % FAR_RING, js[i + FAR_AHEAD],
                       like=(i + FAR_AHEAD - 1) % FAR_RING)
                m_i = tuple(mj_ref[i % FAR_RING, h] + far_bias[h] for h in heads)
                if i + 1 < unroll:
                    o_i = weighted(js[i], i % FAR_RING)
                else:
                    for h in heads:
                        p_ref[h] = prob(i % FAR_RING, h)
                states = merge(states, *pending)
                if i + 1 < unroll:
                    pending = (m_i, o_i, js[i], base + i < n_far)
            return (states, m_i, js[unroll - 1], (base + unroll - 1 < n_far).astype(jnp.int32))
        return body

    carry = (states, pend_m, jnp.int32(0), jnp.int32(0))
    done = 0
    for unroll in FAR_TRIPS:
        left = n_far - done
        n_trips = left // unroll if unroll != FAR_TRIPS[-1] else (left + unroll - 1) // unroll
        carry = lax.fori_loop(0, n_trips, trip(unroll, done), carry)
        done = done + n_trips * unroll
    states, pend_m, pend_j, pend_valid = carry
    o_pend = [weigh(pend_j, h, p_ref[h]) for h in heads]
    o_prev = weighted(prev, slot_prev)
    o_own = weighted(c, slot_own)
    states = merge(states, pend_m, o_pend, pend_j, pend_valid > 0)
    states = merge(states, [mj_ref[slot_prev, h] for h in heads], o_prev, prev, True)
    states = merge(states, [mj_ref[slot_own, h] for h in heads], o_own, c, True, always=True)

    outs = [(acc_ref[h, :HEAD_DIM, :] / acc_ref[h, HEAD_DIM:HEAD_DIM + 1, :]).T for h in heads]
    o_ref[...] = jnp.concatenate(outs, axis=1).astype(o_ref.dtype)


def _moba(rel_bias, qat, ka, vt, gate, bias, batch, seq):
    d_a = ka.shape[1]
    n_blocks = seq // MOBA_BLOCK
    ka3 = ka.reshape(batch, seq, d_a)
    nh = MOBA_HEADS
    width = nh * HEAD_DIM
    out = pl.pallas_call(
        functools.partial(_moba_kernel, n_blocks=n_blocks),
        grid=(batch, d_a // width, n_blocks),
        in_specs=[pl.BlockSpec(memory_space=pltpu.SMEM),
                  pl.BlockSpec((None, width, MOBA_BLOCK), lambda b, g, c: (b, g, c)),
                  pl.BlockSpec((None, seq, width), lambda b, g, c: (b, 0, g)),
                  pl.BlockSpec((None, nh, V_ROWS, seq), lambda b, g, c: (b, g, 0, 0)),
                  pl.BlockSpec((None, nh * n_blocks, MOBA_BLOCK), lambda b, g, c: (b, g, c)),
                  pl.BlockSpec((nh, 2, MOBA_BLOCK, MOBA_BLOCK), lambda b, g, c: (g, 0, 0, 0),
                               pipeline_mode=pl.Buffered(1))],
        out_specs=pl.BlockSpec((None, MOBA_BLOCK, width), lambda b, g, c: (b, c, g)),
        out_shape=jax.ShapeDtypeStruct((batch, seq, d_a), jnp.bfloat16),
        scratch_shapes=[pltpu.VMEM((nh, n_blocks, MOBA_BLOCK), jnp.float32),
                        pltpu.VMEM((FAR_RING, nh, MOBA_BLOCK, MOBA_BLOCK), jnp.bfloat16),
                        pltpu.VMEM((FAR_RING + 2, nh, 1, MOBA_BLOCK), jnp.float32),
                        pltpu.VMEM((nh, MOBA_BLOCK, MOBA_BLOCK), jnp.bfloat16),
                        pltpu.VMEM((nh, V_ROWS, MOBA_BLOCK), jnp.float32),
                        pltpu.VMEM((2, nh, MOBA_BLOCK, MOBA_BLOCK), jnp.float32),
                        pltpu.VMEM((FAR_RING, nh, 1, MOBA_BLOCK), jnp.float32)],
        compiler_params=pltpu.CompilerParams(
            dimension_semantics=("arbitrary", "arbitrary", "arbitrary"),
            vmem_limit_bytes=VMEM_LIMIT),
        name="moba",
    )(rel_bias, qat, ka3, vt, gate, bias)
    return out.reshape(batch * seq, d_a)


def _ret_kernel(q_ref, k_ref, v_ref, g_ref, dmask_ref, dstart_ref, dend_ref, cdec_ref,
                o_ref, state_ref):
    @pl.when(pl.program_id(1) == 0)
    def _():
        state_ref[...] = jnp.zeros_like(state_ref)

    n_pairs = state_ref.shape[0]
    bf = jnp.bfloat16
    f32 = jnp.float32
    items = [(p, s) for p in range(n_pairs) for s in range(RET_SUB)]
    cols = [slice(p * PAIR, (p + 1) * PAIR) for p in range(n_pairs)]
    rows = [slice(s * RET_CHUNK, (s + 1) * RET_CHUNK) for s in range(RET_SUB)]
    lane = lax.broadcasted_iota(jnp.int32, (RET_CHUNK, PAIR), 1)
    in_head = [lane < HEAD_DIM, lane >= HEAD_DIM]
    r = lax.broadcasted_iota(jnp.int32, (PAIR, PAIR), 0) // HEAD_DIM
    cc = lax.broadcasted_iota(jnp.int32, (PAIR, PAIR), 1) // HEAD_DIM
    same_head = r == cc
    avg = jnp.where(same_head, 1.0 / HEAD_DIM, 0.0).astype(bf)

    q = {(p, s): q_ref[rows[s], cols[p]] for p, s in items}
    k = {(p, s): k_ref[rows[s], cols[p]] for p, s in items}
    v = {(p, s): v_ref[rows[s], cols[p]] for p, s in items}

    sc = {i: [lax.dot_general(jnp.where(in_head[h], q[i], jnp.zeros_like(q[i])), k[i], NT,
                              preferred_element_type=f32) for h in range(2)] for i in items}
    kv = {i: lax.dot_general((k[i].astype(f32) * dend_ref[:, cols[i[0]]]).astype(bf), v[i], TN,
                             preferred_element_type=f32) for i in items}
    cross = {}
    for p in range(n_pairs):
        state = state_ref[p]
        for s in range(RET_SUB):
            cross[p, s] = jnp.dot(q[p, s], state.astype(bf), preferred_element_type=f32)
            state = jnp.where(same_head, state * cdec_ref[:, cols[p]] + kv[p, s], 0.0)
        state_ref[p] = state

    y = {}
    for p, s in items:
        yh = [jnp.dot((sc[p, s][h] * dmask_ref[2 * p + h]).astype(bf), v[p, s],
                      preferred_element_type=f32) for h in range(2)]
        y[p, s] = jnp.where(in_head[0], yh[0], yh[1]) + cross[p, s] * dstart_ref[:, cols[p]]

    def head_mean(t):
        return jnp.dot(t.astype(bf), avg, preferred_element_type=f32)

    y_hi = {i: y[i].astype(bf) for i in items}
    mu = {i: jnp.dot(y_hi[i], avg, preferred_element_type=f32)
          + head_mean(y[i] - y_hi[i].astype(f32)) for i in items}
    d = {i: y[i] - mu[i] for i in items}
    var = {i: head_mean(d[i] * d[i]) for i in items}
    for p, s in items:
        g = g_ref[rows[s], cols[p]].astype(f32)
        o_ref[rows[s], cols[p]] = (g * jax.nn.sigmoid(g) * d[p, s]
                                   * lax.rsqrt(var[p, s] + EPS)).astype(bf)


def _retention(qr, kr, vr, gr, dmask, dstart, dend, cdec, batch, seq):
    d_r = qr.shape[1]
    n_steps = seq // (RET_SUB * RET_CHUNK)
    blk = pl.BlockSpec((None, RET_SUB * RET_CHUNK, d_r), lambda b, c: (b, c, 0))
    r3 = lambda a: a.reshape(batch, seq, d_r)
    out = pl.pallas_call(
        _ret_kernel,
        grid=(batch, n_steps),
        in_specs=[blk, blk, blk, blk, _const_spec(dmask.shape), _const_spec(dstart.shape),
                  _const_spec(dend.shape), _const_spec(cdec.shape)],
        out_specs=blk,
        out_shape=jax.ShapeDtypeStruct((batch, seq, d_r), jnp.bfloat16),
        scratch_shapes=[pltpu.VMEM((d_r // PAIR, PAIR, PAIR), jnp.float32)],
        compiler_params=pltpu.CompilerParams(
            dimension_semantics=("arbitrary", "arbitrary"), vmem_limit_bytes=VMEM_LIMIT),
        name="retention",
    )(r3(qr), r3(kr), r3(vr), r3(gr), dmask, dstart, dend, cdec)
    return out.reshape(batch * seq, d_r)


def _ffn_kernel(ya_ref, yr_ref, x_ref, woa_ref, wor_ref, g1_ref, g2_ref, wup_ref, cw_ref, cb_ref,
                wdn_ref, g3_ref, o_ref, u_ref, f_ref, *, tiles_per_batch, d_ff):
    tm = x_ref.shape[0]

    @pl.when(pl.program_id(0) % tiles_per_batch == 0)
    def _():
        u_ref[0:HALO, :] = jnp.zeros((HALO, u_ref.shape[1]), jnp.float32)

    y = (jnp.dot(ya_ref[...], woa_ref[...], preferred_element_type=jnp.float32)
         + jnp.dot(yr_ref[...], wor_ref[...], preferred_element_type=jnp.float32))
    x1 = x_ref[...] + _rms(y, g1_ref[...])
    h2 = _rms(x1, g2_ref[...]).astype(jnp.bfloat16)

    def chunk_cols(ci):
        lo = ci * FF_CHUNK
        return slice(lo, lo + FF_CHUNK), slice(d_ff + lo, d_ff + lo + FF_CHUNK)

    def up(ci):
        for cols in chunk_cols(ci):
            u_ref[HALO:HALO + tm, cols] = jnp.dot(h2, wup_ref[:, cols],
                                                  preferred_element_type=jnp.float32)

    def conv(cols):
        w = cw_ref[:, cols]
        out = cb_ref[:, cols]
        for j in reversed(range(CONV_WIDTH)):
            lo = HALO - (CONV_WIDTH - 1 - j)
            out = out + w[j:j + 1] * u_ref[lo:lo + tm, cols]
        return out

    n_chunks = d_ff // FF_CHUNK
    group_ends = [(g + 1) * n_chunks // FF_GROUPS for g in range(FF_GROUPS)]
    acc = None
    start = 0
    up(0)
    for ci in range(n_chunks):
        if ci + 1 < n_chunks:
            up(ci + 1)
        cols_a, cols_b = chunk_cols(ci)
        f_ref[:, cols_a] = (jax.nn.gelu(conv(cols_a), approximate=True)
                            * conv(cols_b)).astype(jnp.bfloat16)
        if ci + 1 in group_ends:
            rows = slice(start * FF_CHUNK, (ci + 1) * FF_CHUNK)
            part = jnp.dot(f_ref[:, rows], wdn_ref[rows, :], preferred_element_type=jnp.float32)
            acc = part if acc is None else acc + part
            start = ci + 1
    u_ref[0:HALO, :] = u_ref[tm:tm + HALO, :]
    o_ref[...] = x1 + _rms(acc, g3_ref[...])


def _out_ffn(ya, yr, x2, woa, wor, g1, g2, wup, cw, cb, wdn, g3, seq):
    t, d = x2.shape
    d_ff = wdn.shape[0]
    tm = ROW_TILE
    row = lambda w: pl.BlockSpec((tm, w), lambda i: (i, 0))
    return pl.pallas_call(
        functools.partial(_ffn_kernel, tiles_per_batch=seq // tm, d_ff=d_ff),
        grid=(t // tm,),
        in_specs=[row(ya.shape[1]), row(yr.shape[1]), row(d),
                  _const_spec(woa.shape), _const_spec(wor.shape),
                  _const_spec((1, d)), _const_spec((1, d)), _const_spec(wup.shape),
                  _const_spec(cw.shape), _const_spec(cb.shape), _const_spec(wdn.shape),
                  _const_spec((1, d))],
        out_specs=row(d),
        out_shape=jax.ShapeDtypeStruct((t, d), jnp.float32),
        scratch_shapes=[pltpu.VMEM((HALO + tm, 2 * d_ff), jnp.float32),
                        pltpu.VMEM((tm, d_ff), jnp.bfloat16)],
        compiler_params=pltpu.CompilerParams(
            dimension_semantics=("arbitrary",), vmem_limit_bytes=VMEM_LIMIT),
        name="out_ffn",
    )(ya, yr, x2, woa, wor, g1, g2, wup, cw, cb, wdn, g3)


def _rotary_tables(seq):
    inv = ROPE_BASE ** (-jnp.arange(0, HEAD_DIM, 2, dtype=jnp.float32) / HEAD_DIM)
    ang = jnp.arange(seq, dtype=jnp.float32)[:, None] * inv[None, :]
    cos, sin = jnp.cos(ang), jnp.sin(ang)
    cos_t = jnp.tile(jnp.concatenate([cos, cos], axis=1), (1, PAIR // HEAD_DIM))
    sin_t = jnp.tile(jnp.concatenate([-sin, sin], axis=1), (1, PAIR // HEAD_DIM))
    return cos_t, sin_t


def _decay_tables(n_heads):
    c = RET_CHUNK
    log_gamma = jnp.log(1.0 - 2.0 ** (-5.0 - jnp.arange(n_heads, dtype=jnp.float32)))
    n = jnp.arange(c, dtype=jnp.float32)
    rel = n[:, None] - n[None, :]
    dmask = jnp.where(rel >= 0, jnp.exp(jnp.maximum(rel, 0.0)[None] * log_gamma[:, None, None]), 0.0)
    dend = jnp.exp((c - 1.0 - n)[:, None] * log_gamma[None, :])
    dstart = jnp.exp((n + 1.0)[:, None] * log_gamma[None, :])
    cdec = jnp.exp(c * log_gamma)[None, :]
    wide = lambda a: jnp.repeat(a, HEAD_DIM, axis=1)
    return dmask, wide(dstart), wide(dend), wide(cdec)


def kernel(x, norm_mix_pre, w_in, rel_bias, w_out, norm_mix_post, norm_ffn_pre, w_up, conv_w,
           conv_b, w_down, norm_ffn_post):
    batch, seq, d = x.shape
    depth = w_in.shape[0]
    d_a = w_out.shape[1] // 2
    d_r = w_out.shape[1] - d_a
    assert seq % PROJ_TILE == 0 and PROJ_TILE % MOBA_BLOCK == 0 and MOBA_BLOCK == RET_CHUNK
    assert seq % ROW_TILE == 0 and seq % (RET_SUB * RET_CHUNK) == 0
    assert w_in.shape[2] == 3 * d_a + 4 * d_r and d_a % PAIR == 0 and d_r % PAIR == 0
    d_ff = w_down.shape[1]
    assert d_ff % FF_CHUNK == 0 and d_ff // FF_CHUNK >= FF_GROUPS and w_up.shape[2] == 2 * d_ff
    assert conv_w.shape[1] == CONV_WIDTH <= HALO
    assert d_a % (MOBA_HEADS * HEAD_DIM) == 0 and MOBA_HEADS % 2 == 0
    assert all(t % FAR_RING == 0 for t in FAR_TRIPS) and FAR_AHEAD < FAR_RING

    bf = jnp.bfloat16
    cos_t, sin_t = _rotary_tables(seq)
    dmask, dstart, dend, cdec = _decay_tables(d_r // HEAD_DIM)
    bias = _bias_tables(rel_bias)

    x2 = x.reshape(batch * seq, d)
    for l in range(depth):
        w = w_in[l].astype(bf)
        wqvt = jnp.concatenate([w[:, :d_a], w[:, 2 * d_a:3 * d_a]], axis=1).T
        wk = w[:, d_a:2 * d_a]
        wqkr = w[:, 3 * d_a:3 * d_a + 2 * d_r]
        wvgr = w[:, 3 * d_a + 2 * d_r:]
        qat, ka, vt, gate, qr, kr, vr, gr = _projection(
            x2, norm_mix_pre[l][None], wqvt, wk, wqkr, wvgr, cos_t, sin_t, batch, seq)
        ya = _moba(rel_bias, qat, ka, vt, gate, bias, batch, seq)
        yr = _retention(qr, kr, vr, gr, dmask, dstart, dend, cdec, batch, seq)
        wo = w_out[l].astype(bf)
        x2 = _out_ffn(ya, yr, x2, wo[:d_a], wo[d_a:], norm_mix_post[l][None],
                      norm_ffn_pre[l][None], w_up[l].astype(bf), conv_w[l], conv_b[l][None],
                      w_down[l].astype(bf), norm_ffn_post[l][None], seq)
    return x2.reshape(batch, seq, d)
```

```python
import functools
import math

import jax
import jax.numpy as jnp
from jax import lax
from jax.experimental import pallas as pl
from jax.experimental.pallas import tpu as pltpu

HEAD_DIM = 64
PAIR = 2 * HEAD_DIM
MOBA_BLOCK = 256
MOBA_TOPK = 3
RET_CHUNK = 256
REL_BUCKETS = 32
REL_MAX_DIST = 128
ROPE_BASE = 10000.0
CONV_WIDTH = 3
EPS = 1e-6
NEG = -1e30
LOG2E = math.log2(math.e)
Q_SCALE = HEAD_DIM ** -0.5 * LOG2E
V_ROWS = HEAD_DIM + 16
PROJ_TILE = 1024
ROW_TILE = 512
RET_SUB = 4
MOBA_HEADS = 4
FAR_TRIPS = (16, 8, 4)
FAR_AHEAD = 2
FAR_RING = 4
FF_CHUNK = 256
FF_GROUPS = 2
HALO = 8
VMEM_LIMIT = 56 * 1024 * 1024

NT = (((1,), (1,)), ((), ()))
TN = (((0,), (0,)), ((), ()))


def _rms(x, g):
    return x * lax.rsqrt(jnp.mean(x * x, axis=-1, keepdims=True) + EPS) * g


def _bias_kernel(rb_ref, o_ref):
    h = pl.program_id(0)
    key = lax.broadcasted_iota(jnp.int32, (MOBA_BLOCK, MOBA_BLOCK), 0)
    qry = lax.broadcasted_iota(jnp.int32, (MOBA_BLOCK, MOBA_BLOCK), 1)
    max_exact = REL_BUCKETS // 2
    for which in range(2):
        rel = qry - key + which * MOBA_BLOCK
        n = jnp.maximum(rel, 0)
        n_f = jnp.maximum(n, 1).astype(jnp.float32)
        large = max_exact + (jnp.log(n_f / max_exact) / math.log(REL_MAX_DIST / max_exact)
                             * (REL_BUCKETS - max_exact)).astype(jnp.int32)
        large = jnp.minimum(large, REL_BUCKETS - 1)
        bucket = jnp.where(n < max_exact, n, large)
        val = jnp.zeros((MOBA_BLOCK, MOBA_BLOCK), jnp.float32)
        for b in range(REL_BUCKETS):
            val = jnp.where(bucket == b, rb_ref[b, h], val)
        o_ref[0, which] = jnp.where(rel >= 0, val * LOG2E, NEG)


def _bias_tables(rel_bias):
    n_heads = rel_bias.shape[1]
    return pl.pallas_call(
        _bias_kernel,
        grid=(n_heads,),
        in_specs=[pl.BlockSpec(memory_space=pltpu.SMEM)],
        out_specs=pl.BlockSpec((1, 2, MOBA_BLOCK, MOBA_BLOCK), lambda h: (h, 0, 0, 0)),
        out_shape=jax.ShapeDtypeStruct((n_heads, 2, MOBA_BLOCK, MOBA_BLOCK), jnp.float32),
        name="bias_tables",
    )(rel_bias)


def _proj_kernel(x_ref, g_ref, wqvt_ref, wk_ref, wqkr_ref, wvgr_ref, cos_ref, sin_ref,
                 qat_ref, ka_ref, vt_ref, gate_ref, qr_ref, kr_ref, vr_ref, gr_ref,
                 kmt_ref, *, tiles_per_batch, d_a, d_r):
    t_in_b = pl.program_id(0) % tiles_per_batch
    n_heads = d_a // HEAD_DIM
    tm = x_ref.shape[0]
    n_blocks = tiles_per_batch * (tm // MOBA_BLOCK)

    @pl.when(t_in_b == 0)
    def _():
        kmt_ref[...] = jnp.zeros_like(kmt_ref)

    hb = _rms(x_ref[...], g_ref[...]).astype(jnp.bfloat16)

    qvt = lax.dot_general(wqvt_ref[...], hb, NT, preferred_element_type=jnp.float32)
    qt = qvt[:d_a]
    qat_ref[...] = (qt * Q_SCALE).astype(jnp.bfloat16)
    k = jnp.dot(hb, wk_ref[...], preferred_element_type=jnp.float32)
    ka_ref[...] = k.astype(jnp.bfloat16)

    lane_head = lax.broadcasted_iota(jnp.int32, (1, d_a), 1) // HEAD_DIM
    for half in range(tm // MOBA_BLOCK):
        rows = slice(half * MOBA_BLOCK, (half + 1) * MOBA_BLOCK)
        gate_ref[:, rows] = jnp.dot(
            kmt_ref[...].astype(jnp.bfloat16), qt[:, rows].astype(jnp.bfloat16),
            preferred_element_type=jnp.float32)
        k_mean = jnp.mean(k[rows], axis=0, keepdims=True)
        j = t_in_b * (tm // MOBA_BLOCK) + half
        for hh in range(n_heads):
            kmt_ref[pl.ds(hh * n_blocks + j, 1), :] = jnp.where(lane_head == hh, k_mean, 0.0)

    vt = qvt[d_a:].astype(jnp.bfloat16)
    for hh in range(n_heads):
        vt_ref[hh, 0:HEAD_DIM, :] = vt[hh * HEAD_DIM:(hh + 1) * HEAD_DIM, :]
        vt_ref[hh, HEAD_DIM:V_ROWS, :] = jnp.ones((V_ROWS - HEAD_DIM, tm), jnp.bfloat16)

    qkr = jnp.dot(hb, wqkr_ref[...], preferred_element_type=jnp.float32)
    cos = cos_ref[...]
    sin = sin_ref[...]
    first_half = (lax.broadcasted_iota(jnp.int32, (tm, PAIR), 1) % HEAD_DIM) < HEAD_DIM // 2
    for cidx in range(2 * d_r // PAIR):
        xc = qkr[:, cidx * PAIR:(cidx + 1) * PAIR]
        sw = jnp.where(first_half,
                       pltpu.roll(xc, PAIR - HEAD_DIM // 2, axis=1),
                       pltpu.roll(xc, HEAD_DIM // 2, axis=1))
        rc = xc * cos + sw * sin
        if cidx < d_r // PAIR:
            qr_ref[:, cidx * PAIR:(cidx + 1) * PAIR] = rc.astype(jnp.bfloat16)
        else:
            c2 = cidx - d_r // PAIR
            kr_ref[:, c2 * PAIR:(c2 + 1) * PAIR] = (rc * HEAD_DIM ** -0.5).astype(jnp.bfloat16)

    vg = jnp.dot(hb, wvgr_ref[...], preferred_element_type=jnp.float32)
    vr_ref[...] = vg[:, :d_r].astype(jnp.bfloat16)
    gr_ref[...] = vg[:, d_r:].astype(jnp.bfloat16)


def _const_spec(shape):
    zeros = (0,) * len(shape)
    return pl.BlockSpec(shape, lambda *_: zeros, pipeline_mode=pl.Buffered(1))


def _projection(x2, g, wqvt, wk, wqkr, wvgr, cos_t, sin_t, batch, seq):
    t, d = x2.shape
    d_a = wk.shape[1]
    d_r = wqkr.shape[1] // 2
    n_heads = d_a // HEAD_DIM
    tm = PROJ_TILE
    tpb = seq // tm
    n_blocks = seq // MOBA_BLOCK
    row = lambda w: pl.BlockSpec((tm, w), lambda i: (i, 0))
    bf = jnp.bfloat16
    return pl.pallas_call(
        functools.partial(_proj_kernel, tiles_per_batch=tpb, d_a=d_a, d_r=d_r),
        grid=(t // tm,),
        in_specs=[row(d), _const_spec((1, d)), _const_spec(wqvt.shape), _const_spec(wk.shape),
                  _const_spec(wqkr.shape), _const_spec(wvgr.shape),
                  pl.BlockSpec((tm, PAIR), lambda i: (i % tpb, 0)),
                  pl.BlockSpec((tm, PAIR), lambda i: (i % tpb, 0))],
        out_specs=[pl.BlockSpec((None, d_a, tm), lambda i: (i // tpb, 0, i % tpb)), row(d_a),
                   pl.BlockSpec((None, n_heads, V_ROWS, tm), lambda i: (i // tpb, 0, 0, i % tpb)),
                   pl.BlockSpec((None, n_heads * n_blocks, tm), lambda i: (i // tpb, 0, i % tpb)),
                   row(d_r), row(d_r), row(d_r), row(d_r)],
        out_shape=[jax.ShapeDtypeStruct((batch, d_a, seq), bf), jax.ShapeDtypeStruct((t, d_a), bf),
                   jax.ShapeDtypeStruct((batch, n_heads, V_ROWS, seq), bf),
                   jax.ShapeDtypeStruct((batch, n_heads * n_blocks, seq), jnp.float32),
                   jax.ShapeDtypeStruct((t, d_r), bf), jax.ShapeDtypeStruct((t, d_r), bf),
                   jax.ShapeDtypeStruct((t, d_r), bf), jax.ShapeDtypeStruct((t, d_r), bf)],
        scratch_shapes=[pltpu.VMEM((n_heads * n_blocks, d_a), jnp.float32)],
        compiler_params=pltpu.CompilerParams(
            dimension_semantics=("arbitrary",), vmem_limit_bytes=VMEM_LIMIT),
        name="projection",
    )(x2, g, wqvt, wk, wqkr, wvgr, cos_t, sin_t)


def _moba_kernel(rb_ref, qt_ref, k_ref, vt_ref, gate_ref, bias_ref, o_ref, sel_ref, s_ref, mj_ref,
                 p_ref, acc_ref, sn_ref, *, n_blocks):
    nh = vt_ref.shape[0]
    heads = range(nh)
    group = pl.program_id(1)
    c = pl.program_id(2)

    def pair_cols(h):
        return slice((h // 2) * PAIR, (h // 2 + 1) * PAIR)

    feat = lax.broadcasted_iota(jnp.int32, (PAIR, MOBA_BLOCK), 0)
    qm = []
    for h in heads:
        qt = qt_ref[pair_cols(h), :]
        own_feat = (feat >= HEAD_DIM) if h % 2 else (feat < HEAD_DIM)
        qm.append(jnp.where(own_feat, qt, jnp.zeros_like(qt)))

    slot_prev, slot_own = FAR_RING, FAR_RING + 1
    far_bias = [rb_ref[REL_BUCKETS - 1, nh * group + h] * LOG2E for h in heads]
    prev = jnp.maximum(c - 1, 0)
    n_far = prev
    last = jnp.maximum(n_far - 1, 0)

    def block_rows(j):
        return pl.ds(pl.multiple_of(j * MOBA_BLOCK, MOBA_BLOCK), MOBA_BLOCK)

    def scores(slot, j, bias_index=None):
        for h in heads:
            s = jnp.dot(k_ref[block_rows(j), pair_cols(h)], qm[h],
                        preferred_element_type=jnp.float32)
            if bias_index is not None:
                s = s + bias_ref[h, bias_index]
                sn_ref[slot - FAR_RING, h] = s
            else:
                s_ref[slot, h] = s.astype(jnp.bfloat16)
            mj_ref[slot, h] = jnp.max(s, axis=0, keepdims=True)

    def prob(slot, h):
        if slot >= FAR_RING:
            return jnp.exp2((sn_ref[slot - FAR_RING, h] - mj_ref[slot, h]).astype(jnp.bfloat16))
        return jnp.exp2(s_ref[slot, h] - mj_ref[slot, h].astype(jnp.bfloat16))

    def weigh(j, h, p):
        return jnp.dot(vt_ref[h, :, block_rows(j)], p, preferred_element_type=jnp.float32)

    def weighted(j, slot):
        return [weigh(j, h, prob(slot, h)) for h in heads]

    def merge(states, mj, oj, j, valid, always=False):
        out = []
        for h in heads:
            m = states[h]
            if always:
                m_new = jnp.maximum(m, mj[h])
                beta = jnp.exp2(mj[h] - m_new)
            else:
                chosen = (sel_ref[h, pl.ds(j, 1), :] > 0.5) & valid
                m_new = jnp.where(chosen, jnp.maximum(m, mj[h]), m)
                beta = jnp.where(chosen, jnp.exp2(mj[h] - m_new), 0.0)
            acc_ref[h] = acc_ref[h] * jnp.exp2(m - m_new) + oj[h] * beta
            out.append(m_new)
        return tuple(out)

    scores(slot_own, c, 0)
    scores(slot_prev, prev, 1)
    for i in range(FAR_AHEAD):
        scores(i, jnp.minimum(i, last))

    blk = lax.broadcasted_iota(jnp.int32, (n_blocks, MOBA_BLOCK), 0)
    for h in heads:
        g = jnp.where(blk < c, gate_ref[h * n_blocks:(h + 1) * n_blocks, :], -jnp.inf)
        sel = jnp.zeros(g.shape, jnp.float32)
        for _ in range(MOBA_TOPK):
            top = jnp.max(g, axis=0, keepdims=True)
            idx = jnp.min(jnp.where(g == top, blk, n_blocks), axis=0, keepdims=True)
            pick = (blk == idx) & (blk < c)
            sel = jnp.where(pick, 1.0, sel)
            g = jnp.where(pick, -jnp.inf, g)
        sel_ref[h] = sel

    states = tuple(jnp.full((1, MOBA_BLOCK), NEG, jnp.float32) for _ in heads)
    acc_ref[...] = jnp.zeros_like(acc_ref)

    p_ref[...] = jnp.zeros_like(p_ref)
    pend_m = tuple(jnp.zeros((1, MOBA_BLOCK), jnp.float32) for _ in heads)

    def trip(unroll, first):
        def body(t, carry):
            states, pend_m, pend_j, pend_valid = carry
            base = first + unroll * t
            js = [jnp.minimum(base + i, last) for i in range(unroll + FAR_AHEAD)]
            pending = (pend_m, [weigh(pend_j, h, p_ref[h]) for h in heads], pend_j,
                       pend_valid > 0)
            for i in range(unroll):
                scores((i + FAR_AHEAD) % FAR_RING, js[i + FAR_AHEAD])
                m_i = tuple(mj_ref[i % FAR_RING, h] + far_bias[h] for h in heads)
                if i + 1 < unroll:
                    o_i = weighted(js[i], i % FAR_RING)
                else:
                    for h in heads:
                        p_ref[h] = prob(i % FAR_RING, h)
                states = merge(states, *pending)
                if i + 1 < unroll:
                    pending = (m_i, o_i, js[i], base + i < n_far)
            return (states, m_i, js[unroll - 1], (base + unroll - 1 < n_far).astype(jnp.int32))
        return body

    carry = (states, pend_m, jnp.int32(0), jnp.int32(0))
    done = 0
    for unroll in FAR_TRIPS:
        left = n_far - done
        n_trips = left // unroll if unroll != FAR_TRIPS[-1] else (left + unroll - 1) // unroll
        carry = lax.fori_loop(0, n_trips, trip(unroll, done), carry)
        done = done + n_trips * unroll
    states, pend_m, pend_j, pend_valid = carry
    o_pend = [weigh(pend_j, h, p_ref[h]) for h in heads]
    o_prev = weighted(prev, slot_prev)
    o_own = weighted(c, slot_own)
    states = merge(states, pend_m, o_pend, pend_j, pend_valid > 0)
    states = merge(states, [mj_ref[slot_prev, h] for h in heads], o_prev, prev, True)
    states = merge(states, [mj_ref[slot_own, h] for h in heads], o_own, c, True, always=True)

    outs = [(acc_ref[h, :HEAD_DIM, :] / acc_ref[h, HEAD_DIM:HEAD_DIM + 1, :]).T for h in heads]
    o_ref[...] = jnp.concatenate(outs, axis=1).astype(o_ref.dtype)


def _moba(rel_bias, qat, ka, vt, gate, bias, batch, seq):
    d_a = ka.shape[1]
    n_blocks = seq // MOBA_BLOCK
    ka3 = ka.reshape(batch, seq, d_a)
    nh = MOBA_HEADS
    width = nh * HEAD_DIM
    out = pl.pallas_call(
        functools.partial(_moba_kernel, n_blocks=n_blocks),
        grid=(batch, d_a // width, n_blocks),
        in_specs=[pl.BlockSpec(memory_space=pltpu.SMEM),
                  pl.BlockSpec((None, width, MOBA_BLOCK), lambda b, g, c: (b, g, c)),
                  pl.BlockSpec((None, seq, width), lambda b, g, c: (b, 0, g)),
                  pl.BlockSpec((None, nh, V_ROWS, seq), lambda b, g, c: (b, g, 0, 0)),
                  pl.BlockSpec((None, nh * n_blocks, MOBA_BLOCK), lambda b, g, c: (b, g, c)),
                  pl.BlockSpec((nh, 2, MOBA_BLOCK, MOBA_BLOCK), lambda b, g, c: (g, 0, 0, 0),
                               pipeline_mode=pl.Buffered(1))],
        out_specs=pl.BlockSpec((None, MOBA_BLOCK, width), lambda b, g, c: (b, c, g)),
        out_shape=jax.ShapeDtypeStruct((batch, seq, d_a), jnp.bfloat16),
        scratch_shapes=[pltpu.VMEM((nh, n_blocks, MOBA_BLOCK), jnp.float32),
                        pltpu.VMEM((FAR_RING, nh, MOBA_BLOCK, MOBA_BLOCK), jnp.bfloat16),
                        pltpu.VMEM((FAR_RING + 2, nh, 1, MOBA_BLOCK), jnp.float32),
                        pltpu.VMEM((nh, MOBA_BLOCK, MOBA_BLOCK), jnp.bfloat16),
                        pltpu.VMEM((nh, V_ROWS, MOBA_BLOCK), jnp.float32),
                        pltpu.VMEM((2, nh, MOBA_BLOCK, MOBA_BLOCK), jnp.float32)],
        compiler_params=pltpu.CompilerParams(
            dimension_semantics=("arbitrary", "arbitrary", "arbitrary"),
            vmem_limit_bytes=VMEM_LIMIT),
        name="moba",
    )(rel_bias, qat, ka3, vt, gate, bias)
    return out.reshape(batch * seq, d_a)


def _ret_kernel(q_ref, k_ref, v_ref, g_ref, dmask_ref, dstart_ref, dend_ref, cdec_ref,
                o_ref, state_ref):
    @pl.when(pl.program_id(1) == 0)
    def _():
        state_ref[...] = jnp.zeros_like(state_ref)

    n_pairs = state_ref.shape[0]
    bf = jnp.bfloat16
    f32 = jnp.float32
    items = [(p, s) for p in range(n_pairs) for s in range(RET_SUB)]
    cols = [slice(p * PAIR, (p + 1) * PAIR) for p in range(n_pairs)]
    rows = [slice(s * RET_CHUNK, (s + 1) * RET_CHUNK) for s in range(RET_SUB)]
    lane = lax.broadcasted_iota(jnp.int32, (RET_CHUNK, PAIR), 1)
    in_head = [lane < HEAD_DIM, lane >= HEAD_DIM]
    r = lax.broadcasted_iota(jnp.int32, (PAIR, PAIR), 0) // HEAD_DIM
    cc = lax.broadcasted_iota(jnp.int32, (PAIR, PAIR), 1) // HEAD_DIM
    same_head = r == cc
    avg = jnp.where(same_head, 1.0 / HEAD_DIM, 0.0).astype(bf)

    q = {(p, s): q_ref[rows[s], cols[p]] for p, s in items}
    k = {(p, s): k_ref[rows[s], cols[p]] for p, s in items}
    v = {(p, s): v_ref[rows[s], cols[p]] for p, s in items}

    sc = {i: [lax.dot_general(jnp.where(in_head[h], q[i], jnp.zeros_like(q[i])), k[i], NT,
                              preferred_element_type=f32) for h in range(2)] for i in items}
    kv = {i: lax.dot_general((k[i].astype(f32) * dend_ref[:, cols[i[0]]]).astype(bf), v[i], TN,
                             preferred_element_type=f32) for i in items}
    cross = {}
    for p in range(n_pairs):
        state = state_ref[p]
        for s in range(RET_SUB):
            cross[p, s] = jnp.dot(q[p, s], state.astype(bf), preferred_element_type=f32)
            state = jnp.where(same_head, state * cdec_ref[:, cols[p]] + kv[p, s], 0.0)
        state_ref[p] = state

    y = {}
    for p, s in items:
        yh = [jnp.dot((sc[p, s][h] * dmask_ref[2 * p + h]).astype(bf), v[p, s],
                      preferred_element_type=f32) for h in range(2)]
        y[p, s] = jnp.where(in_head[0], yh[0], yh[1]) + cross[p, s] * dstart_ref[:, cols[p]]

    def head_mean(t):
        return jnp.dot(t.astype(bf), avg, preferred_element_type=f32)

    y_hi = {i: y[i].astype(bf) for i in items}
    mu = {i: jnp.dot(y_hi[i], avg, preferred_element_type=f32)
          + head_mean(y[i] - y_hi[i].astype(f32)) for i in items}
    d = {i: y[i] - mu[i] for i in items}
    var = {i: head_mean(d[i] * d[i]) for i in items}
    for p, s in items:
        g = g_ref[rows[s], cols[p]].astype(f32)
        o_ref[rows[s], cols[p]] = (g * jax.nn.sigmoid(g) * d[p, s]
                                   * lax.rsqrt(var[p, s] + EPS)).astype(bf)


def _retention(qr, kr, vr, gr, dmask, dstart, dend, cdec, batch, seq):
    d_r = qr.shape[1]
    n_steps = seq // (RET_SUB * RET_CHUNK)
    blk = pl.BlockSpec((None, RET_SUB * RET_CHUNK, d_r), lambda b, c: (b, c, 0))
    r3 = lambda a: a.reshape(batch, seq, d_r)
    out = pl.pallas_call(
        _ret_kernel,
        grid=(batch, n_steps),
        in_specs=[blk, blk, blk, blk, _const_spec(dmask.shape), _const_spec(dstart.shape),
                  _const_spec(dend.shape), _const_spec(cdec.shape)],
        out_specs=blk,
        out_shape=jax.ShapeDtypeStruct((batch, seq, d_r), jnp.bfloat16),
        scratch_shapes=[pltpu.VMEM((d_r // PAIR, PAIR, PAIR), jnp.float32)],
        compiler_params=pltpu.CompilerParams(
            dimension_semantics=("arbitrary", "arbitrary"), vmem_limit_bytes=VMEM_LIMIT),
        name="retention",
    )(r3(qr), r3(kr), r3(vr), r3(gr), dmask, dstart, dend, cdec)
    return out.reshape(batch * seq, d_r)


def _ffn_kernel(ya_ref, yr_ref, x_ref, woa_ref, wor_ref, g1_ref, g2_ref, wup_ref, cw_ref, cb_ref,
                wdn_ref, g3_ref, o_ref, u_ref, f_ref, *, tiles_per_batch, d_ff):
    tm = x_ref.shape[0]

    @pl.when(pl.program_id(0) % tiles_per_batch == 0)
    def _():
        u_ref[0:HALO, :] = jnp.zeros((HALO, u_ref.shape[1]), jnp.float32)

    y = (jnp.dot(ya_ref[...], woa_ref[...], preferred_element_type=jnp.float32)
         + jnp.dot(yr_ref[...], wor_ref[...], preferred_element_type=jnp.float32))
    x1 = x_ref[...] + _rms(y, g1_ref[...])
    h2 = _rms(x1, g2_ref[...]).astype(jnp.bfloat16)

    def chunk_cols(ci):
        lo = ci * FF_CHUNK
        return slice(lo, lo + FF_CHUNK), slice(d_ff + lo, d_ff + lo + FF_CHUNK)

    def up(ci):
        for cols in chunk_cols(ci):
            u_ref[HALO:HALO + tm, cols] = jnp.dot(h2, wup_ref[:, cols],
                                                  preferred_element_type=jnp.float32)

    def conv(cols):
        w = cw_ref[:, cols]
        out = cb_ref[:, cols]
        for j in reversed(range(CONV_WIDTH)):
            lo = HALO - (CONV_WIDTH - 1 - j)
            out = out + w[j:j + 1] * u_ref[lo:lo + tm, cols]
        return out

    n_chunks = d_ff // FF_CHUNK
    group_ends = [(g + 1) * n_chunks // FF_GROUPS for g in range(FF_GROUPS)]
    acc = None
    start = 0
    up(0)
    for ci in range(n_chunks):
        if ci + 1 < n_chunks:
            up(ci + 1)
        cols_a, cols_b = chunk_cols(ci)
        f_ref[:, cols_a] = (jax.nn.gelu(conv(cols_a), approximate=True)
                            * conv(cols_b)).astype(jnp.bfloat16)
        if ci + 1 in group_ends:
            rows = slice(start * FF_CHUNK, (ci + 1) * FF_CHUNK)
            part = jnp.dot(f_ref[:, rows], wdn_ref[rows, :], preferred_element_type=jnp.float32)
            acc = part if acc is None else acc + part
            start = ci + 1
    u_ref[0:HALO, :] = u_ref[tm:tm + HALO, :]
    o_ref[...] = x1 + _rms(acc, g3_ref[...])


def _out_ffn(ya, yr, x2, woa, wor, g1, g2, wup, cw, cb, wdn, g3, seq):
    t, d = x2.shape
    d_ff = wdn.shape[0]
    tm = ROW_TILE
    row = lambda w: pl.BlockSpec((tm, w), lambda i: (i, 0))
    return pl.pallas_call(
        functools.partial(_ffn_kernel, tiles_per_batch=seq // tm, d_ff=d_ff),
        grid=(t // tm,),
        in_specs=[row(ya.shape[1]), row(yr.shape[1]), row(d),
                  _const_spec(woa.shape), _const_spec(wor.shape),
                  _const_spec((1, d)), _const_spec((1, d)), _const_spec(wup.shape),
                  _const_spec(cw.shape), _const_spec(cb.shape), _const_spec(wdn.shape),
                  _const_spec((1, d))],
        out_specs=row(d),
        out_shape=jax.ShapeDtypeStruct((t, d), jnp.float32),
        scratch_shapes=[pltpu.VMEM((HALO + tm, 2 * d_ff), jnp.float32),
                        pltpu.VMEM((tm, d_ff), jnp.bfloat16)],
        compiler_params=pltpu.CompilerParams(
            dimension_semantics=("arbitrary",), vmem_limit_bytes=VMEM_LIMIT),
        name="out_ffn",
    )(ya, yr, x2, woa, wor, g1, g2, wup, cw, cb, wdn, g3)


def _rotary_tables(seq):
    inv = ROPE_BASE ** (-jnp.arange(0, HEAD_DIM, 2, dtype=jnp.float32) / HEAD_DIM)
    ang = jnp.arange(seq, dtype=jnp.float32)[:, None] * inv[None, :]
    cos, sin = jnp.cos(ang), jnp.sin(ang)
    cos_t = jnp.tile(jnp.concatenate([cos, cos], axis=1), (1, PAIR // HEAD_DIM))
    sin_t = jnp.tile(jnp.concatenate([-sin, sin], axis=1), (1, PAIR // HEAD_DIM))
    return cos_t, sin_t


def _decay_tables(n_heads):
    c = RET_CHUNK
    log_gamma = jnp.log(1.0 - 2.0 ** (-5.0 - jnp.arange(n_heads, dtype=jnp.float32)))
    n = jnp.arange(c, dtype=jnp.float32)
    rel = n[:, None] - n[None, :]
    dmask = jnp.where(rel >= 0, jnp.exp(jnp.maximum(rel, 0.0)[None] * log_gamma[:, None, None]), 0.0)
    dend = jnp.exp((c - 1.0 - n)[:, None] * log_gamma[None, :])
    dstart = jnp.exp((n + 1.0)[:, None] * log_gamma[None, :])
    cdec = jnp.exp(c * log_gamma)[None, :]
    wide = lambda a: jnp.repeat(a, HEAD_DIM, axis=1)
    return dmask, wide(dstart), wide(dend), wide(cdec)


def kernel(x, norm_mix_pre, w_in, rel_bias, w_out, norm_mix_post, norm_ffn_pre, w_up, conv_w,
           conv_b, w_down, norm_ffn_post):
    batch, seq, d = x.shape
    depth = w_in.shape[0]
    d_a = w_out.shape[1] // 2
    d_r = w_out.shape[1] - d_a
    assert seq % PROJ_TILE == 0 and PROJ_TILE % MOBA_BLOCK == 0 and MOBA_BLOCK == RET_CHUNK
    assert seq % ROW_TILE == 0 and seq % (RET_SUB * RET_CHUNK) == 0
    assert w_in.shape[2] == 3 * d_a + 4 * d_r and d_a % PAIR == 0 and d_r % PAIR == 0
    d_ff = w_down.shape[1]
    assert d_ff % FF_CHUNK == 0 and d_ff // FF_CHUNK >= FF_GROUPS and w_up.shape[2] == 2 * d_ff
    assert conv_w.shape[1] == CONV_WIDTH <= HALO
    assert d_a % (MOBA_HEADS * HEAD_DIM) == 0 and MOBA_HEADS % 2 == 0
    assert all(t % FAR_RING == 0 for t in FAR_TRIPS) and FAR_AHEAD < FAR_RING

    bf = jnp.bfloat16
    cos_t, sin_t = _rotary_tables(seq)
    dmask, dstart, dend, cdec = _decay_tables(d_r // HEAD_DIM)
    bias = _bias_tables(rel_bias)

    x2 = x.reshape(batch * seq, d)
    for l in range(depth):
        w = w_in[l].astype(bf)
        wqvt = jnp.concatenate([w[:, :d_a], w[:, 2 * d_a:3 * d_a]], axis=1).T
        wk = w[:, d_a:2 * d_a]
        wqkr = w[:, 3 * d_a:3 * d_a + 2 * d_r]
        wvgr = w[:, 3 * d_a + 2 * d_r:]
        qat, ka, vt, gate, qr, kr, vr, gr = _projection(
            x2, norm_mix_pre[l][None], wqvt, wk, wqkr, wvgr, cos_t, sin_t, batch, seq)
        ya = _moba(rel_bias, qat, ka, vt, gate, bias, batch, seq)
        yr = _retention(qr, kr, vr, gr, dmask, dstart, dend, cdec, batch, seq)
        wo = w_out[l].astype(bf)
        x2 = _out_ffn(ya, yr, x2, wo[:d_a], wo[d_a:], norm_mix_post[l][None],
                      norm_ffn_pre[l][None], w_up[l].astype(bf), conv_w[l], conv_b[l][None],
                      w_down[l].astype(bf), norm_ffn_post[l][None], seq)
    return x2.reshape(batch, seq, d)
```

```python
import functools
import math

import jax
import jax.numpy as jnp
from jax import lax
from jax.experimental import pallas as pl
from jax.experimental.pallas import tpu as pltpu

HEAD_DIM = 64
PAIR = 2 * HEAD_DIM
MOBA_BLOCK = 256
MOBA_TOPK = 3
RET_CHUNK = 256
REL_BUCKETS = 32
REL_MAX_DIST = 128
ROPE_BASE = 10000.0
CONV_WIDTH = 3
EPS = 1e-6
NEG = -1e30
LOG2E = math.log2(math.e)
Q_SCALE = HEAD_DIM ** -0.5 * LOG2E
V_ROWS = HEAD_DIM + 16
PROJ_TILE = 1024
ROW_TILE = 512
RET_SUB = 4
MOBA_HEADS = 4
FAR_TRIPS = (16, 8, 4)
FAR_AHEAD = 2
FAR_RING = 4
FF_CHUNK = 256
FF_GROUPS = 2
HALO = 8
VMEM_LIMIT = 56 * 1024 * 1024

QUERY_HALVES = (slice(0, MOBA_BLOCK // 2), slice(MOBA_BLOCK // 2, MOBA_BLOCK))

NT = (((1,), (1,)), ((), ()))
TN = (((0,), (0,)), ((), ()))


def _rms(x, g):
    return x * lax.rsqrt(jnp.mean(x * x, axis=-1, keepdims=True) + EPS) * g


def _bias_kernel(rb_ref, o_ref):
    h = pl.program_id(0)
    key = lax.broadcasted_iota(jnp.int32, (MOBA_BLOCK, MOBA_BLOCK), 0)
    qry = lax.broadcasted_iota(jnp.int32, (MOBA_BLOCK, MOBA_BLOCK), 1)
    max_exact = REL_BUCKETS // 2
    for which in range(2):
        rel = qry - key + which * MOBA_BLOCK
        n = jnp.maximum(rel, 0)
        n_f = jnp.maximum(n, 1).astype(jnp.float32)
        large = max_exact + (jnp.log(n_f / max_exact) / math.log(REL_MAX_DIST / max_exact)
                             * (REL_BUCKETS - max_exact)).astype(jnp.int32)
        large = jnp.minimum(large, REL_BUCKETS - 1)
        bucket = jnp.where(n < max_exact, n, large)
        val = jnp.zeros((MOBA_BLOCK, MOBA_BLOCK), jnp.float32)
        for b in range(REL_BUCKETS):
            val = jnp.where(bucket == b, rb_ref[b, h], val)
        o_ref[0, which] = jnp.where(rel >= 0, val * LOG2E, NEG)


def _bias_tables(rel_bias):
    n_heads = rel_bias.shape[1]
    return pl.pallas_call(
        _bias_kernel,
        grid=(n_heads,),
        in_specs=[pl.BlockSpec(memory_space=pltpu.SMEM)],
        out_specs=pl.BlockSpec((1, 2, MOBA_BLOCK, MOBA_BLOCK), lambda h: (h, 0, 0, 0)),
        out_shape=jax.ShapeDtypeStruct((n_heads, 2, MOBA_BLOCK, MOBA_BLOCK), jnp.float32),
        name="bias_tables",
    )(rel_bias)


def _proj_kernel(x_ref, g_ref, wqvt_ref, wk_ref, wqkr_ref, wvgr_ref, cos_ref, sin_ref,
                 qat_ref, ka_ref, vt_ref, gate_ref, qr_ref, kr_ref, vr_ref, gr_ref,
                 kmt_ref, *, tiles_per_batch, d_a, d_r):
    t_in_b = pl.program_id(0) % tiles_per_batch
    n_heads = d_a // HEAD_DIM
    tm = x_ref.shape[0]
    n_blocks = tiles_per_batch * (tm // MOBA_BLOCK)

    @pl.when(t_in_b == 0)
    def _():
        kmt_ref[...] = jnp.zeros_like(kmt_ref)

    hb = _rms(x_ref[...], g_ref[...]).astype(jnp.bfloat16)

    qvt = lax.dot_general(wqvt_ref[...], hb, NT, preferred_element_type=jnp.float32)
    qt = qvt[:d_a]
    qat_ref[...] = (qt * Q_SCALE).astype(jnp.bfloat16)
    k = jnp.dot(hb, wk_ref[...], preferred_element_type=jnp.float32)
    ka_ref[...] = k.astype(jnp.bfloat16)

    lane_head = lax.broadcasted_iota(jnp.int32, (1, d_a), 1) // HEAD_DIM
    for half in range(tm // MOBA_BLOCK):
        rows = slice(half * MOBA_BLOCK, (half + 1) * MOBA_BLOCK)
        gate_ref[:, rows] = jnp.dot(
            kmt_ref[...].astype(jnp.bfloat16), qt[:, rows].astype(jnp.bfloat16),
            preferred_element_type=jnp.float32)
        k_mean = jnp.mean(k[rows], axis=0, keepdims=True)
        j = t_in_b * (tm // MOBA_BLOCK) + half
        for hh in range(n_heads):
            kmt_ref[pl.ds(hh * n_blocks + j, 1), :] = jnp.where(lane_head == hh, k_mean, 0.0)

    vt = qvt[d_a:].astype(jnp.bfloat16)
    for hh in range(n_heads):
        vt_ref[hh, 0:HEAD_DIM, :] = vt[hh * HEAD_DIM:(hh + 1) * HEAD_DIM, :]
        vt_ref[hh, HEAD_DIM:V_ROWS, :] = jnp.ones((V_ROWS - HEAD_DIM, tm), jnp.bfloat16)

    qkr = jnp.dot(hb, wqkr_ref[...], preferred_element_type=jnp.float32)
    cos = cos_ref[...]
    sin = sin_ref[...]
    first_half = (lax.broadcasted_iota(jnp.int32, (tm, PAIR), 1) % HEAD_DIM) < HEAD_DIM // 2
    for cidx in range(2 * d_r // PAIR):
        xc = qkr[:, cidx * PAIR:(cidx + 1) * PAIR]
        sw = jnp.where(first_half,
                       pltpu.roll(xc, PAIR - HEAD_DIM // 2, axis=1),
                       pltpu.roll(xc, HEAD_DIM // 2, axis=1))
        rc = xc * cos + sw * sin
        if cidx < d_r // PAIR:
            qr_ref[:, cidx * PAIR:(cidx + 1) * PAIR] = rc.astype(jnp.bfloat16)
        else:
            c2 = cidx - d_r // PAIR
            kr_ref[:, c2 * PAIR:(c2 + 1) * PAIR] = (rc * HEAD_DIM ** -0.5).astype(jnp.bfloat16)

    vg = jnp.dot(hb, wvgr_ref[...], preferred_element_type=jnp.float32)
    vr_ref[...] = vg[:, :d_r].astype(jnp.bfloat16)
    gr_ref[...] = vg[:, d_r:].astype(jnp.bfloat16)


def _const_spec(shape):
    zeros = (0,) * len(shape)
    return pl.BlockSpec(shape, lambda *_: zeros, pipeline_mode=pl.Buffered(1))


def _projection(x2, g, wqvt, wk, wqkr, wvgr, cos_t, sin_t, batch, seq):
    t, d = x2.shape
    d_a = wk.shape[1]
    d_r = wqkr.shape[1] // 2
    n_heads = d_a // HEAD_DIM
    tm = PROJ_TILE
    tpb = seq // tm
    n_blocks = seq // MOBA_BLOCK
    row = lambda w: pl.BlockSpec((tm, w), lambda i: (i, 0))
    bf = jnp.bfloat16
    return pl.pallas_call(
        functools.partial(_proj_kernel, tiles_per_batch=tpb, d_a=d_a, d_r=d_r),
        grid=(t // tm,),
        in_specs=[row(d), _const_spec((1, d)), _const_spec(wqvt.shape), _const_spec(wk.shape),
                  _const_spec(wqkr.shape), _const_spec(wvgr.shape),
                  pl.BlockSpec((tm, PAIR), lambda i: (i % tpb, 0)),
                  pl.BlockSpec((tm, PAIR), lambda i: (i % tpb, 0))],
        out_specs=[pl.BlockSpec((None, d_a, tm), lambda i: (i // tpb, 0, i % tpb)), row(d_a),
                   pl.BlockSpec((None, n_heads, V_ROWS, tm), lambda i: (i // tpb, 0, 0, i % tpb)),
                   pl.BlockSpec((None, n_heads * n_blocks, tm), lambda i: (i // tpb, 0, i % tpb)),
                   row(d_r), row(d_r), row(d_r), row(d_r)],
        out_shape=[jax.ShapeDtypeStruct((batch, d_a, seq), bf), jax.ShapeDtypeStruct((t, d_a), bf),
                   jax.ShapeDtypeStruct((batch, n_heads, V_ROWS, seq), bf),
                   jax.ShapeDtypeStruct((batch, n_heads * n_blocks, seq), jnp.float32),
                   jax.ShapeDtypeStruct((t, d_r), bf), jax.ShapeDtypeStruct((t, d_r), bf),
                   jax.ShapeDtypeStruct((t, d_r), bf), jax.ShapeDtypeStruct((t, d_r), bf)],
        scratch_shapes=[pltpu.VMEM((n_heads * n_blocks, d_a), jnp.float32)],
        compiler_params=pltpu.CompilerParams(
            dimension_semantics=("arbitrary",), vmem_limit_bytes=VMEM_LIMIT),
        name="projection",
    )(x2, g, wqvt, wk, wqkr, wvgr, cos_t, sin_t)


def _moba_kernel(rb_ref, qt_ref, k_ref, vt_ref, gate_ref, bias_ref, o_ref, sel_ref, s_ref, mj_ref,
                 p_ref, acc_ref, sn_ref, *, n_blocks):
    nh = vt_ref.shape[0]
    heads = range(nh)
    group = pl.program_id(1)
    c = pl.program_id(2)

    def pair_cols(h):
        return slice((h // 2) * PAIR, (h // 2 + 1) * PAIR)

    feat = lax.broadcasted_iota(jnp.int32, (PAIR, MOBA_BLOCK), 0)
    qm = []
    for h in heads:
        qt = qt_ref[pair_cols(h), :]
        own_feat = (feat >= HEAD_DIM) if h % 2 else (feat < HEAD_DIM)
        qm.append(jnp.where(own_feat, qt, jnp.zeros_like(qt)))

    slot_prev, slot_own = FAR_RING, FAR_RING + 1
    far_bias = [rb_ref[REL_BUCKETS - 1, nh * group + h] * LOG2E for h in heads]
    prev = jnp.maximum(c - 1, 0)
    n_far = prev
    last = jnp.maximum(n_far - 1, 0)

    def block_rows(j):
        return pl.ds(pl.multiple_of(j * MOBA_BLOCK, MOBA_BLOCK), MOBA_BLOCK)

    def scores(slot, j, bias_index=None):
        for h in heads:
            s = jnp.dot(k_ref[block_rows(j), pair_cols(h)], qm[h],
                        preferred_element_type=jnp.float32)
            if bias_index is not None:
                s = s + bias_ref[h, bias_index]
            for half in QUERY_HALVES:
                sh = s[:, half]
                if bias_index is not None:
                    sn_ref[slot - FAR_RING, h, :, half] = sh
                else:
                    s_ref[slot, h, :, half] = sh.astype(jnp.bfloat16)
                mj_ref[slot, h, :, half] = jnp.max(sh, axis=0, keepdims=True)

    def prob(slot, h):
        parts = []
        for half in QUERY_HALVES:
            m = mj_ref[slot, h, :, half]
            if slot >= FAR_RING:
                parts.append(jnp.exp2((sn_ref[slot - FAR_RING, h, :, half] - m)
                                      .astype(jnp.bfloat16)))
            else:
                parts.append(jnp.exp2(s_ref[slot, h, :, half] - m.astype(jnp.bfloat16)))
        return jnp.concatenate(parts, axis=1)

    def weigh(j, h, p):
        return jnp.dot(vt_ref[h, :, block_rows(j)], p, preferred_element_type=jnp.float32)

    def weighted(j, slot):
        return [weigh(j, h, prob(slot, h)) for h in heads]

    def merge(states, mj, oj, j, valid, always=False):
        out = []
        for h in heads:
            m = states[h]
            if always:
                m_new = jnp.maximum(m, mj[h])
                beta = jnp.exp2(mj[h] - m_new)
            else:
                chosen = (sel_ref[h, pl.ds(j, 1), :] > 0.5) & valid
                m_new = jnp.where(chosen, jnp.maximum(m, mj[h]), m)
                beta = jnp.where(chosen, jnp.exp2(mj[h] - m_new), 0.0)
            acc_ref[h] = acc_ref[h] * jnp.exp2(m - m_new) + oj[h] * beta
            out.append(m_new)
        return tuple(out)

    scores(slot_own, c, 0)
    scores(slot_prev, prev, 1)
    for i in range(FAR_AHEAD):
        scores(i, jnp.minimum(i, last))

    blk = lax.broadcasted_iota(jnp.int32, (n_blocks, MOBA_BLOCK), 0)
    for h in heads:
        g = jnp.where(blk < c, gate_ref[h * n_blocks:(h + 1) * n_blocks, :], -jnp.inf)
        sel = jnp.zeros(g.shape, jnp.float32)
        for _ in range(MOBA_TOPK):
            top = jnp.max(g, axis=0, keepdims=True)
            idx = jnp.min(jnp.where(g == top, blk, n_blocks), axis=0, keepdims=True)
            pick = (blk == idx) & (blk < c)
            sel = jnp.where(pick, 1.0, sel)
            g = jnp.where(pick, -jnp.inf, g)
        sel_ref[h] = sel

    states = tuple(jnp.full((1, MOBA_BLOCK), NEG, jnp.float32) for _ in heads)
    acc_ref[...] = jnp.zeros_like(acc_ref)

    p_ref[...] = jnp.zeros_like(p_ref)
    pend_m = tuple(jnp.zeros((1, MOBA_BLOCK), jnp.float32) for _ in heads)

    def trip(unroll, first):
        def body(t, carry):
            states, pend_m, pend_j, pend_valid = carry
            base = first + unroll * t
            js = [jnp.minimum(base + i, last) for i in range(unroll + FAR_AHEAD)]
            pending = (pend_m, [weigh(pend_j, h, p_ref[h]) for h in heads], pend_j,
                       pend_valid > 0)
            for i in range(unroll):
                scores((i + FAR_AHEAD) % FAR_RING, js[i + FAR_AHEAD])
                m_i = tuple(mj_ref[i % FAR_RING, h] + far_bias[h] for h in heads)
                if i + 1 < unroll:
                    o_i = weighted(js[i], i % FAR_RING)
                else:
                    for h in heads:
                        p_ref[h] = prob(i % FAR_RING, h)
                states = merge(states, *pending)
                if i + 1 < unroll:
                    pending = (m_i, o_i, js[i], base + i < n_far)
            return (states, m_i, js[unroll - 1], (base + unroll - 1 < n_far).astype(jnp.int32))
        return body

    carry = (states, pend_m, jnp.int32(0), jnp.int32(0))
    done = 0
    for unroll in FAR_TRIPS:
        left = n_far - done
        n_trips = left // unroll if unroll != FAR_TRIPS[-1] else (left + unroll - 1) // unroll
        carry = lax.fori_loop(0, n_trips, trip(unroll, done), carry)
        done = done + n_trips * unroll
    states, pend_m, pend_j, pend_valid = carry
    o_pend = [weigh(pend_j, h, p_ref[h]) for h in heads]
    o_prev = weighted(prev, slot_prev)
    o_own = weighted(c, slot_own)
    states = merge(states, pend_m, o_pend, pend_j, pend_valid > 0)
    states = merge(states, [mj_ref[slot_prev, h] for h in heads], o_prev, prev, True)
    states = merge(states, [mj_ref[slot_own, h] for h in heads], o_own, c, True, always=True)

    outs = [(acc_ref[h, :HEAD_DIM, :] / acc_ref[h, HEAD_DIM:HEAD_DIM + 1, :]).T for h in heads]
    o_ref[...] = jnp.concatenate(outs, axis=1).astype(o_ref.dtype)


def _moba(rel_bias, qat, ka, vt, gate, bias, batch, seq):
    d_a = ka.shape[1]
    n_blocks = seq // MOBA_BLOCK
    ka3 = ka.reshape(batch, seq, d_a)
    nh = MOBA_HEADS
    width = nh * HEAD_DIM
    out = pl.pallas_call(
        functools.partial(_moba_kernel, n_blocks=n_blocks),
        grid=(batch, d_a // width, n_blocks),
        in_specs=[pl.BlockSpec(memory_space=pltpu.SMEM),
                  pl.BlockSpec((None, width, MOBA_BLOCK), lambda b, g, c: (b, g, c)),
                  pl.BlockSpec((None, seq, width), lambda b, g, c: (b, 0, g)),
                  pl.BlockSpec((None, nh, V_ROWS, seq), lambda b, g, c: (b, g, 0, 0)),
                  pl.BlockSpec((None, nh * n_blocks, MOBA_BLOCK), lambda b, g, c: (b, g, c)),
                  pl.BlockSpec((nh, 2, MOBA_BLOCK, MOBA_BLOCK), lambda b, g, c: (g, 0, 0, 0),
                               pipeline_mode=pl.Buffered(1))],
        out_specs=pl.BlockSpec((None, MOBA_BLOCK, width), lambda b, g, c: (b, c, g)),
        out_shape=jax.ShapeDtypeStruct((batch, seq, d_a), jnp.bfloat16),
        scratch_shapes=[pltpu.VMEM((nh, n_blocks, MOBA_BLOCK), jnp.float32),
                        pltpu.VMEM((FAR_RING, nh, MOBA_BLOCK, MOBA_BLOCK), jnp.bfloat16),
                        pltpu.VMEM((FAR_RING + 2, nh, 1, MOBA_BLOCK), jnp.float32),
                        pltpu.VMEM((nh, MOBA_BLOCK, MOBA_BLOCK), jnp.bfloat16),
                        pltpu.VMEM((nh, V_ROWS, MOBA_BLOCK), jnp.float32),
                        pltpu.VMEM((2, nh, MOBA_BLOCK, MOBA_BLOCK), jnp.float32)],
        compiler_params=pltpu.CompilerParams(
            dimension_semantics=("arbitrary", "arbitrary", "arbitrary"),
            vmem_limit_bytes=VMEM_LIMIT),
        name="moba",
    )(rel_bias, qat, ka3, vt, gate, bias)
    return out.reshape(batch * seq, d_a)


def _ret_kernel(q_ref, k_ref, v_ref, g_ref, dmask_ref, dstart_ref, dend_ref, cdec_ref,
                o_ref, state_ref):
    @pl.when(pl.program_id(1) == 0)
    def _():
        state_ref[...] = jnp.zeros_like(state_ref)

    n_pairs = state_ref.shape[0]
    bf = jnp.bfloat16
    f32 = jnp.float32
    items = [(p, s) for p in range(n_pairs) for s in range(RET_SUB)]
    cols = [slice(p * PAIR, (p + 1) * PAIR) for p in range(n_pairs)]
    rows = [slice(s * RET_CHUNK, (s + 1) * RET_CHUNK) for s in range(RET_SUB)]
    lane = lax.broadcasted_iota(jnp.int32, (RET_CHUNK, PAIR), 1)
    in_head = [lane < HEAD_DIM, lane >= HEAD_DIM]
    r = lax.broadcasted_iota(jnp.int32, (PAIR, PAIR), 0) // HEAD_DIM
    cc = lax.broadcasted_iota(jnp.int32, (PAIR, PAIR), 1) // HEAD_DIM
    same_head = r == cc
    avg = jnp.where(same_head, 1.0 / HEAD_DIM, 0.0).astype(bf)

    q = {(p, s): q_ref[rows[s], cols[p]] for p, s in items}
    k = {(p, s): k_ref[rows[s], cols[p]] for p, s in items}
    v = {(p, s): v_ref[rows[s], cols[p]] for p, s in items}

    sc = {i: [lax.dot_general(jnp.where(in_head[h], q[i], jnp.zeros_like(q[i])), k[i], NT,
                              preferred_element_type=f32) for h in range(2)] for i in items}
    kv = {i: lax.dot_general((k[i].astype(f32) * dend_ref[:, cols[i[0]]]).astype(bf), v[i], TN,
                             preferred_element_type=f32) for i in items}
    cross = {}
    for p in range(n_pairs):
        state = state_ref[p]
        for s in range(RET_SUB):
            cross[p, s] = jnp.dot(q[p, s], state.astype(bf), preferred_element_type=f32)
            state = jnp.where(same_head, state * cdec_ref[:, cols[p]] + kv[p, s], 0.0)
        state_ref[p] = state

    y = {}
    for p, s in items:
        yh = [jnp.dot((sc[p, s][h] * dmask_ref[2 * p + h]).astype(bf), v[p, s],
                      preferred_element_type=f32) for h in range(2)]
        y[p, s] = jnp.where(in_head[0], yh[0], yh[1]) + cross[p, s] * dstart_ref[:, cols[p]]

    def head_mean(t):
        return jnp.dot(t.astype(bf), avg, preferred_element_type=f32)

    y_hi = {i: y[i].astype(bf) for i in items}
    mu = {i: jnp.dot(y_hi[i], avg, preferred_element_type=f32)
          + head_mean(y[i] - y_hi[i].astype(f32)) for i in items}
    d = {i: y[i] - mu[i] for i in items}
    var = {i: head_mean(d[i] * d[i]) for i in items}
    for p, s in items:
        g = g_ref[rows[s], cols[p]].astype(f32)
        o_ref[rows[s], cols[p]] = (g * jax.nn.sigmoid(g) * d[p, s]
                                   * lax.rsqrt(var[p, s] + EPS)).astype(bf)


def _retention(qr, kr, vr, gr, dmask, dstart, dend, cdec, batch, seq):
    d_r = qr.shape[1]
    n_steps = seq // (RET_SUB * RET_CHUNK)
    blk = pl.BlockSpec((None, RET_SUB * RET_CHUNK, d_r), lambda b, c: (b, c, 0))
    r3 = lambda a: a.reshape(batch, seq, d_r)
    out = pl.pallas_call(
        _ret_kernel,
        grid=(batch, n_steps),
        in_specs=[blk, blk, blk, blk, _const_spec(dmask.shape), _const_spec(dstart.shape),
                  _const_spec(dend.shape), _const_spec(cdec.shape)],
        out_specs=blk,
        out_shape=jax.ShapeDtypeStruct((batch, seq, d_r), jnp.bfloat16),
        scratch_shapes=[pltpu.VMEM((d_r // PAIR, PAIR, PAIR), jnp.float32)],
        compiler_params=pltpu.CompilerParams(
            dimension_semantics=("arbitrary", "arbitrary"), vmem_limit_bytes=VMEM_LIMIT),
        name="retention",
    )(r3(qr), r3(kr), r3(vr), r3(gr), dmask, dstart, dend, cdec)
    return out.reshape(batch * seq, d_r)


def _ffn_kernel(ya_ref, yr_ref, x_ref, woa_ref, wor_ref, g1_ref, g2_ref, wup_ref, cw_ref, cb_ref,
                wdn_ref, g3_ref, o_ref, u_ref, f_ref, *, tiles_per_batch, d_ff):
    tm = x_ref.shape[0]

    @pl.when(pl.program_id(0) % tiles_per_batch == 0)
    def _():
        u_ref[0:HALO, :] = jnp.zeros((HALO, u_ref.shape[1]), jnp.float32)

    y = (jnp.dot(ya_ref[...], woa_ref[...], preferred_element_type=jnp.float32)
         + jnp.dot(yr_ref[...], wor_ref[...], preferred_element_type=jnp.float32))
    x1 = x_ref[...] + _rms(y, g1_ref[...])
    h2 = _rms(x1, g2_ref[...]).astype(jnp.bfloat16)

    def chunk_cols(ci):
        lo = ci * FF_CHUNK
        return slice(lo, lo + FF_CHUNK), slice(d_ff + lo, d_ff + lo + FF_CHUNK)

    def up(ci):
        for cols in chunk_cols(ci):
            u_ref[HALO:HALO + tm, cols] = jnp.dot(h2, wup_ref[:, cols],
                                                  preferred_element_type=jnp.float32)

    def conv(cols):
        w = cw_ref[:, cols]
        out = cb_ref[:, cols]
        for j in reversed(range(CONV_WIDTH)):
            lo = HALO - (CONV_WIDTH - 1 - j)
            out = out + w[j:j + 1] * u_ref[lo:lo + tm, cols]
        return out

    n_chunks = d_ff // FF_CHUNK
    group_ends = [(g + 1) * n_chunks // FF_GROUPS for g in range(FF_GROUPS)]
    acc = None
    start = 0
    up(0)
    for ci in range(n_chunks):
        if ci + 1 < n_chunks:
            up(ci + 1)
        cols_a, cols_b = chunk_cols(ci)
        f_ref[:, cols_a] = (jax.nn.gelu(conv(cols_a), approximate=True)
                            * conv(cols_b)).astype(jnp.bfloat16)
        if ci + 1 in group_ends:
            rows = slice(start * FF_CHUNK, (ci + 1) * FF_CHUNK)
            part = jnp.dot(f_ref[:, rows], wdn_ref[rows, :], preferred_element_type=jnp.float32)
            acc = part if acc is None else acc + part
            start = ci + 1
    u_ref[0:HALO, :] = u_ref[tm:tm + HALO, :]
    o_ref[...] = x1 + _rms(acc, g3_ref[...])


def _out_ffn(ya, yr, x2, woa, wor, g1, g2, wup, cw, cb, wdn, g3, seq):
    t, d = x2.shape
    d_ff = wdn.shape[0]
    tm = ROW_TILE
    row = lambda w: pl.BlockSpec((tm, w), lambda i: (i, 0))
    return pl.pallas_call(
        functools.partial(_ffn_kernel, tiles_per_batch=seq // tm, d_ff=d_ff),
        grid=(t // tm,),
        in_specs=[row(ya.shape[1]), row(yr.shape[1]), row(d),
                  _const_spec(woa.shape), _const_spec(wor.shape),
                  _const_spec((1, d)), _const_spec((1, d)), _const_spec(wup.shape),
                  _const_spec(cw.shape), _const_spec(cb.shape), _const_spec(wdn.shape),
                  _const_spec((1, d))],
        out_specs=row(d),
        out_shape=jax.ShapeDtypeStruct((t, d), jnp.float32),
        scratch_shapes=[pltpu.VMEM((HALO + tm, 2 * d_ff), jnp.float32),
                        pltpu.VMEM((tm, d_ff), jnp.bfloat16)],
        compiler_params=pltpu.CompilerParams(
            dimension_semantics=("arbitrary",), vmem_limit_bytes=VMEM_LIMIT),
        name="out_ffn",
    )(ya, yr, x2, woa, wor, g1, g2, wup, cw, cb, wdn, g3)


def _rotary_tables(seq):
    inv = ROPE_BASE ** (-jnp.arange(0, HEAD_DIM, 2, dtype=jnp.float32) / HEAD_DIM)
    ang = jnp.arange(seq, dtype=jnp.float32)[:, None] * inv[None, :]
    cos, sin = jnp.cos(ang), jnp.sin(ang)
    cos_t = jnp.tile(jnp.concatenate([cos, cos], axis=1), (1, PAIR // HEAD_DIM))
    sin_t = jnp.tile(jnp.concatenate([-sin, sin], axis=1), (1, PAIR // HEAD_DIM))
    return cos_t, sin_t


def _decay_tables(n_heads):
    c = RET_CHUNK
    log_gamma = jnp.log(1.0 - 2.0 ** (-5.0 - jnp.arange(n_heads, dtype=jnp.float32)))
    n = jnp.arange(c, dtype=jnp.float32)
    rel = n[:, None] - n[None, :]
    dmask = jnp.where(rel >= 0, jnp.exp(jnp.maximum(rel, 0.0)[None] * log_gamma[:, None, None]), 0.0)
    dend = jnp.exp((c - 1.0 - n)[:, None] * log_gamma[None, :])
    dstart = jnp.exp((n + 1.0)[:, None] * log_gamma[None, :])
    cdec = jnp.exp(c * log_gamma)[None, :]
    wide = lambda a: jnp.repeat(a, HEAD_DIM, axis=1)
    return dmask, wide(dstart), wide(dend), wide(cdec)


def kernel(x, norm_mix_pre, w_in, rel_bias, w_out, norm_mix_post, norm_ffn_pre, w_up, conv_w,
           conv_b, w_down, norm_ffn_post):
    batch, seq, d = x.shape
    depth = w_in.shape[0]
    d_a = w_out.shape[1] // 2
    d_r = w_out.shape[1] - d_a
    assert seq % PROJ_TILE == 0 and PROJ_TILE % MOBA_BLOCK == 0 and MOBA_BLOCK == RET_CHUNK
    assert seq % ROW_TILE == 0 and seq % (RET_SUB * RET_CHUNK) == 0
    assert w_in.shape[2] == 3 * d_a + 4 * d_r and d_a % PAIR == 0 and d_r % PAIR == 0
    d_ff = w_down.shape[1]
    assert d_ff % FF_CHUNK == 0 and d_ff // FF_CHUNK >= FF_GROUPS and w_up.shape[2] == 2 * d_ff
    assert conv_w.shape[1] == CONV_WIDTH <= HALO
    assert d_a % (MOBA_HEADS * HEAD_DIM) == 0 and MOBA_HEADS % 2 == 0
    assert all(t % FAR_RING == 0 for t in FAR_TRIPS) and FAR_AHEAD < FAR_RING

    bf = jnp.bfloat16
    cos_t, sin_t = _rotary_tables(seq)
    dmask, dstart, dend, cdec = _decay_tables(d_r // HEAD_DIM)
    bias = _bias_tables(rel_bias)

    x2 = x.reshape(batch * seq, d)
    for l in range(depth):
        w = w_in[l].astype(bf)
        wqvt = jnp.concatenate([w[:, :d_a], w[:, 2 * d_a:3 * d_a]], axis=1).T
        wk = w[:, d_a:2 * d_a]
        wqkr = w[:, 3 * d_a:3 * d_a + 2 * d_r]
        wvgr = w[:, 3 * d_a + 2 * d_r:]
        qat, ka, vt, gate, qr, kr, vr, gr = _projection(
            x2, norm_mix_pre[l][None], wqvt, wk, wqkr, wvgr, cos_t, sin_t, batch, seq)
        ya = _moba(rel_bias, qat, ka, vt, gate, bias, batch, seq)
        yr = _retention(qr, kr, vr, gr, dmask, dstart, dend, cdec, batch, seq)
        wo = w_out[l].astype(bf)
        x2 = _out_ffn(ya, yr, x2, wo[:d_a], wo[d_a:], norm_mix_post[l][None],
                      norm_ffn_pre[l][None], w_up[l].astype(bf), conv_w[l], conv_b[l][None],
                      w_down[l].astype(bf), norm_ffn_post[l][None], seq)
    return x2.reshape(batch, seq, d)
```

```python
import functools
import math

import jax
import jax.numpy as jnp
from jax import lax
from jax.experimental import pallas as pl
from jax.experimental.pallas import tpu as pltpu

HEAD_DIM = 64
PAIR = 2 * HEAD_DIM
MOBA_BLOCK = 256
MOBA_TOPK = 3
RET_CHUNK = 256
REL_BUCKETS = 32
REL_MAX_DIST = 128
ROPE_BASE = 10000.0
CONV_WIDTH = 3
EPS = 1e-6
NEG = -1e30
LOG2E = math.log2(math.e)
Q_SCALE = HEAD_DIM ** -0.5 * LOG2E
V_ROWS = HEAD_DIM + 16
PROJ_TILE = 1024
ROW_TILE = 512
RET_SUB = 4
MOBA_HEADS = 4
FAR_TRIPS = (16, 8, 4)
FAR_AHEAD = 2
FAR_RING = 4
FF_CHUNK = 256
FF_GROUPS = 2
HALO = 8
VMEM_LIMIT = 56 * 1024 * 1024

NT = (((1,), (1,)), ((), ()))
TN = (((0,), (0,)), ((), ()))


def _rms(x, g):
    return x * lax.rsqrt(jnp.mean(x * x, axis=-1, keepdims=True) + EPS) * g


def _bias_kernel(rb_ref, o_ref):
    h = pl.program_id(0)
    key = lax.broadcasted_iota(jnp.int32, (MOBA_BLOCK, MOBA_BLOCK), 0)
    qry = lax.broadcasted_iota(jnp.int32, (MOBA_BLOCK, MOBA_BLOCK), 1)
    max_exact = REL_BUCKETS // 2
    for which in range(2):
        rel = qry - key + which * MOBA_BLOCK
        n = jnp.maximum(rel, 0)
        n_f = jnp.maximum(n, 1).astype(jnp.float32)
        large = max_exact + (jnp.log(n_f / max_exact) / math.log(REL_MAX_DIST / max_exact)
                             * (REL_BUCKETS - max_exact)).astype(jnp.int32)
        large = jnp.minimum(large, REL_BUCKETS - 1)
        bucket = jnp.where(n < max_exact, n, large)
        val = jnp.zeros((MOBA_BLOCK, MOBA_BLOCK), jnp.float32)
        for b in range(REL_BUCKETS):
            val = jnp.where(bucket == b, rb_ref[b, h], val)
        o_ref[0, which] = jnp.where(rel >= 0, val * LOG2E, NEG)


def _bias_tables(rel_bias):
    n_heads = rel_bias.shape[1]
    return pl.pallas_call(
        _bias_kernel,
        grid=(n_heads,),
        in_specs=[pl.BlockSpec(memory_space=pltpu.SMEM)],
        out_specs=pl.BlockSpec((1, 2, MOBA_BLOCK, MOBA_BLOCK), lambda h: (h, 0, 0, 0)),
        out_shape=jax.ShapeDtypeStruct((n_heads, 2, MOBA_BLOCK, MOBA_BLOCK), jnp.float32),
        name="bias_tables",
    )(rel_bias)


def _proj_kernel(x_ref, g_ref, wqvt_ref, wk_ref, wqkr_ref, wvgr_ref, cos_ref, sin_ref,
                 qat_ref, ka_ref, vt_ref, gate_ref, qr_ref, kr_ref, vr_ref, gr_ref,
                 kmt_ref, *, tiles_per_batch, d_a, d_r):
    t_in_b = pl.program_id(0) % tiles_per_batch
    n_heads = d_a // HEAD_DIM
    tm = x_ref.shape[0]
    n_blocks = tiles_per_batch * (tm // MOBA_BLOCK)

    @pl.when(t_in_b == 0)
    def _():
        kmt_ref[...] = jnp.zeros_like(kmt_ref)

    hb = _rms(x_ref[...], g_ref[...]).astype(jnp.bfloat16)

    qvt = lax.dot_general(wqvt_ref[...], hb, NT, preferred_element_type=jnp.float32)
    qt = qvt[:d_a]
    qat_ref[...] = (qt * Q_SCALE).astype(jnp.bfloat16)
    k = jnp.dot(hb, wk_ref[...], preferred_element_type=jnp.float32)
    ka_ref[...] = k.astype(jnp.bfloat16)

    lane_head = lax.broadcasted_iota(jnp.int32, (1, d_a), 1) // HEAD_DIM
    for half in range(tm // MOBA_BLOCK):
        rows = slice(half * MOBA_BLOCK, (half + 1) * MOBA_BLOCK)
        gate_ref[:, rows] = jnp.dot(
            kmt_ref[...].astype(jnp.bfloat16), qt[:, rows].astype(jnp.bfloat16),
            preferred_element_type=jnp.float32)
        k_mean = jnp.mean(k[rows], axis=0, keepdims=True)
        j = t_in_b * (tm // MOBA_BLOCK) + half
        for hh in range(n_heads):
            kmt_ref[pl.ds(hh * n_blocks + j, 1), :] = jnp.where(lane_head == hh, k_mean, 0.0)

    vt = qvt[d_a:].astype(jnp.bfloat16)
    for hh in range(n_heads):
        vt_ref[hh, 0:HEAD_DIM, :] = vt[hh * HEAD_DIM:(hh + 1) * HEAD_DIM, :]
        vt_ref[hh, HEAD_DIM:V_ROWS, :] = jnp.ones((V_ROWS - HEAD_DIM, tm), jnp.bfloat16)

    qkr = jnp.dot(hb, wqkr_ref[...], preferred_element_type=jnp.float32)
    cos = cos_ref[...]
    sin = sin_ref[...]
    first_half = (lax.broadcasted_iota(jnp.int32, (tm, PAIR), 1) % HEAD_DIM) < HEAD_DIM // 2
    for cidx in range(2 * d_r // PAIR):
        xc = qkr[:, cidx * PAIR:(cidx + 1) * PAIR]
        sw = jnp.where(first_half,
                       pltpu.roll(xc, PAIR - HEAD_DIM // 2, axis=1),
                       pltpu.roll(xc, HEAD_DIM // 2, axis=1))
        rc = xc * cos + sw * sin
        if cidx < d_r // PAIR:
            qr_ref[:, cidx * PAIR:(cidx + 1) * PAIR] = rc.astype(jnp.bfloat16)
        else:
            c2 = cidx - d_r // PAIR
            kr_ref[:, c2 * PAIR:(c2 + 1) * PAIR] = (rc * HEAD_DIM ** -0.5).astype(jnp.bfloat16)

    vg = jnp.dot(hb, wvgr_ref[...], preferred_element_type=jnp.float32)
    vr_ref[...] = vg[:, :d_r].astype(jnp.bfloat16)
    gr_ref[...] = vg[:, d_r:].astype(jnp.bfloat16)


def _const_spec(shape):
    zeros = (0,) * len(shape)
    return pl.BlockSpec(shape, lambda *_: zeros, pipeline_mode=pl.Buffered(1))


def _projection(x2, g, wqvt, wk, wqkr, wvgr, cos_t, sin_t, batch, seq):
    t, d = x2.shape
    d_a = wk.shape[1]
    d_r = wqkr.shape[1] // 2
    n_heads = d_a // HEAD_DIM
    tm = PROJ_TILE
    tpb = seq // tm
    n_blocks = seq // MOBA_BLOCK
    row = lambda w: pl.BlockSpec((tm, w), lambda i: (i, 0))
    bf = jnp.bfloat16
    return pl.pallas_call(
        functools.partial(_proj_kernel, tiles_per_batch=tpb, d_a=d_a, d_r=d_r),
        grid=(t // tm,),
        in_specs=[row(d), _const_spec((1, d)), _const_spec(wqvt.shape), _const_spec(wk.shape),
                  _const_spec(wqkr.shape), _const_spec(wvgr.shape),
                  pl.BlockSpec((tm, PAIR), lambda i: (i % tpb, 0)),
                  pl.BlockSpec((tm, PAIR), lambda i: (i % tpb, 0))],
        out_specs=[pl.BlockSpec((None, d_a, tm), lambda i: (i // tpb, 0, i % tpb)), row(d_a),
                   pl.BlockSpec((None, n_heads, V_ROWS, tm), lambda i: (i // tpb, 0, 0, i % tpb)),
                   pl.BlockSpec((None, n_heads * n_blocks, tm), lambda i: (i // tpb, 0, i % tpb)),
                   row(d_r), row(d_r), row(d_r), row(d_r)],
        out_shape=[jax.ShapeDtypeStruct((batch, d_a, seq), bf), jax.ShapeDtypeStruct((t, d_a), bf),
                   jax.ShapeDtypeStruct((batch, n_heads, V_ROWS, seq), bf),
                   jax.ShapeDtypeStruct((batch, n_heads * n_blocks, seq), jnp.float32),
                   jax.ShapeDtypeStruct((t, d_r), bf), jax.ShapeDtypeStruct((t, d_r), bf),
                   jax.ShapeDtypeStruct((t, d_r), bf), jax.ShapeDtypeStruct((t, d_r), bf)],
        scratch_shapes=[pltpu.VMEM((n_heads * n_blocks, d_a), jnp.float32)],
        compiler_params=pltpu.CompilerParams(
            dimension_semantics=("arbitrary",), vmem_limit_bytes=VMEM_LIMIT),
        name="projection",
    )(x2, g, wqvt, wk, wqkr, wvgr, cos_t, sin_t)


def _moba_kernel(rb_ref, qt_ref, k_ref, vt_ref, gate_ref, bias_ref, o_ref, sel_ref, s_ref, mj_ref,
                 p_ref, acc_ref, sn_ref, qm, *, n_blocks):
    nh = vt_ref.shape[0]
    heads = range(nh)
    group = pl.program_id(1)
    c = pl.program_id(2)

    def pair_cols(h):
        return slice((h // 2) * PAIR, (h // 2 + 1) * PAIR)

    feat = lax.broadcasted_iota(jnp.int32, (PAIR, MOBA_BLOCK), 0)
    for h in heads:
        qt = qt_ref[pair_cols(h), :]
        own_feat = (feat >= HEAD_DIM) if h % 2 else (feat < HEAD_DIM)
        qm[h] = jnp.where(own_feat, qt, jnp.zeros_like(qt))

    slot_prev, slot_own = FAR_RING, FAR_RING + 1
    far_bias = [rb_ref[REL_BUCKETS - 1, nh * group + h] * LOG2E for h in heads]
    prev = jnp.maximum(c - 1, 0)
    n_far = prev
    last = jnp.maximum(n_far - 1, 0)

    def block_rows(j):
        return pl.ds(pl.multiple_of(j * MOBA_BLOCK, MOBA_BLOCK), MOBA_BLOCK)

    def scores(slot, j, bias_index=None):
        for h in heads:
            s = jnp.dot(k_ref[block_rows(j), pair_cols(h)], qm[h],
                        preferred_element_type=jnp.float32)
            if bias_index is not None:
                s = s + bias_ref[h, bias_index]
                sn_ref[slot - FAR_RING, h] = s
            else:
                s_ref[slot, h] = s.astype(jnp.bfloat16)
            mj_ref[slot, h] = jnp.max(s, axis=0, keepdims=True)

    def prob(slot, h):
        if slot >= FAR_RING:
            return jnp.exp2((sn_ref[slot - FAR_RING, h] - mj_ref[slot, h]).astype(jnp.bfloat16))
        return jnp.exp2(s_ref[slot, h] - mj_ref[slot, h].astype(jnp.bfloat16))

    def weigh(j, h, p):
        return jnp.dot(vt_ref[h, :, block_rows(j)], p, preferred_element_type=jnp.float32)

    def weighted(j, slot):
        return [weigh(j, h, prob(slot, h)) for h in heads]

    def merge(states, mj, oj, j, valid, always=False):
        out = []
        for h in heads:
            m = states[h]
            if always:
                m_new = jnp.maximum(m, mj[h])
                beta = jnp.exp2(mj[h] - m_new)
            else:
                chosen = (sel_ref[h, pl.ds(j, 1), :] > 0.5) & valid
                m_new = jnp.where(chosen, jnp.maximum(m, mj[h]), m)
                beta = jnp.where(chosen, jnp.exp2(mj[h] - m_new), 0.0)
            acc_ref[h] = acc_ref[h] * jnp.exp2(m - m_new) + oj[h] * beta
            out.append(m_new)
        return tuple(out)

    scores(slot_own, c, 0)
    scores(slot_prev, prev, 1)
    for i in range(FAR_AHEAD):
        scores(i, jnp.minimum(i, last))

    blk = lax.broadcasted_iota(jnp.int32, (n_blocks, MOBA_BLOCK), 0)
    for h in heads:
        g = jnp.where(blk < c, gate_ref[h * n_blocks:(h + 1) * n_blocks, :], -jnp.inf)
        sel = jnp.zeros(g.shape, jnp.float32)
        for _ in range(MOBA_TOPK):
            top = jnp.max(g, axis=0, keepdims=True)
            idx = jnp.min(jnp.where(g == top, blk, n_blocks), axis=0, keepdims=True)
            pick = (blk == idx) & (blk < c)
            sel = jnp.where(pick, 1.0, sel)
            g = jnp.where(pick, -jnp.inf, g)
        sel_ref[h] = sel

    states = tuple(jnp.full((1, MOBA_BLOCK), NEG, jnp.float32) for _ in heads)
    acc_ref[...] = jnp.zeros_like(acc_ref)

    p_ref[...] = jnp.zeros_like(p_ref)
    pend_m = tuple(jnp.zeros((1, MOBA_BLOCK), jnp.float32) for _ in heads)

    def trip(unroll, first):
        def body(t, carry):
            states, pend_m, pend_j, pend_valid = carry
            base = first + unroll * t
            js = [jnp.minimum(base + i, last) for i in range(unroll + FAR_AHEAD)]
            pending = (pend_m, [weigh(pend_j, h, p_ref[h]) for h in heads], pend_j,
                       pend_valid > 0)
            for i in range(unroll):
                scores((i + FAR_AHEAD) % FAR_RING, js[i + FAR_AHEAD])
                m_i = tuple(mj_ref[i % FAR_RING, h] + far_bias[h] for h in heads)
                if i + 1 < unroll:
                    o_i = weighted(js[i], i % FAR_RING)
                else:
                    for h in heads:
                        p_ref[h] = prob(i % FAR_RING, h)
                states = merge(states, *pending)
                if i + 1 < unroll:
                    pending = (m_i, o_i, js[i], base + i < n_far)
            return (states, m_i, js[unroll - 1], (base + unroll - 1 < n_far).astype(jnp.int32))
        return body

    carry = (states, pend_m, jnp.int32(0), jnp.int32(0))
    done = 0
    for unroll in FAR_TRIPS:
        left = n_far - done
        n_trips = left // unroll if unroll != FAR_TRIPS[-1] else (left + unroll - 1) // unroll
        carry = lax.fori_loop(0, n_trips, trip(unroll, done), carry)
        done = done + n_trips * unroll
    states, pend_m, pend_j, pend_valid = carry
    o_pend = [weigh(pend_j, h, p_ref[h]) for h in heads]
    o_prev = weighted(prev, slot_prev)
    o_own = weighted(c, slot_own)
    states = merge(states, pend_m, o_pend, pend_j, pend_valid > 0)
    states = merge(states, [mj_ref[slot_prev, h] for h in heads], o_prev, prev, True)
    states = merge(states, [mj_ref[slot_own, h] for h in heads], o_own, c, True, always=True)

    outs = [(acc_ref[h, :HEAD_DIM, :] / acc_ref[h, HEAD_DIM:HEAD_DIM + 1, :]).T for h in heads]
    o_ref[...] = jnp.concatenate(outs, axis=1).astype(o_ref.dtype)


def _moba(rel_bias, qat, ka, vt, gate, bias, batch, seq):
    d_a = ka.shape[1]
    n_blocks = seq // MOBA_BLOCK
    ka3 = ka.reshape(batch, seq, d_a)
    nh = MOBA_HEADS
    width = nh * HEAD_DIM
    out = pl.pallas_call(
        functools.partial(_moba_kernel, n_blocks=n_blocks),
        grid=(batch, d_a // width, n_blocks),
        in_specs=[pl.BlockSpec(memory_space=pltpu.SMEM),
                  pl.BlockSpec((None, width, MOBA_BLOCK), lambda b, g, c: (b, g, c)),
                  pl.BlockSpec((None, seq, width), lambda b, g, c: (b, 0, g)),
                  pl.BlockSpec((None, nh, V_ROWS, seq), lambda b, g, c: (b, g, 0, 0)),
                  pl.BlockSpec((None, nh * n_blocks, MOBA_BLOCK), lambda b, g, c: (b, g, c)),
                  pl.BlockSpec((nh, 2, MOBA_BLOCK, MOBA_BLOCK), lambda b, g, c: (g, 0, 0, 0),
                               pipeline_mode=pl.Buffered(1))],
        out_specs=pl.BlockSpec((None, MOBA_BLOCK, width), lambda b, g, c: (b, c, g)),
        out_shape=jax.ShapeDtypeStruct((batch, seq, d_a), jnp.bfloat16),
        scratch_shapes=[pltpu.VMEM((nh, n_blocks, MOBA_BLOCK), jnp.float32),
                        pltpu.VMEM((FAR_RING, nh, MOBA_BLOCK, MOBA_BLOCK), jnp.bfloat16),
                        pltpu.VMEM((FAR_RING + 2, nh, 1, MOBA_BLOCK), jnp.float32),
                        pltpu.VMEM((nh, MOBA_BLOCK, MOBA_BLOCK), jnp.bfloat16),
                        pltpu.VMEM((nh, V_ROWS, MOBA_BLOCK), jnp.float32),
                        pltpu.VMEM((2, nh, MOBA_BLOCK, MOBA_BLOCK), jnp.float32),
                        pltpu.VMEM((nh, PAIR, MOBA_BLOCK), jnp.bfloat16)],
        compiler_params=pltpu.CompilerParams(
            dimension_semantics=("arbitrary", "arbitrary", "arbitrary"),
            vmem_limit_bytes=VMEM_LIMIT),
        name="moba",
    )(rel_bias, qat, ka3, vt, gate, bias)
    return out.reshape(batch * seq, d_a)


def _ret_kernel(q_ref, k_ref, v_ref, g_ref, dmask_ref, dstart_ref, dend_ref, cdec_ref,
                o_ref, state_ref):
    @pl.when(pl.program_id(1) == 0)
    def _():
        state_ref[...] = jnp.zeros_like(state_ref)

    n_pairs = state_ref.shape[0]
    bf = jnp.bfloat16
    f32 = jnp.float32
    items = [(p, s) for p in range(n_pairs) for s in range(RET_SUB)]
    cols = [slice(p * PAIR, (p + 1) * PAIR) for p in range(n_pairs)]
    rows = [slice(s * RET_CHUNK, (s + 1) * RET_CHUNK) for s in range(RET_SUB)]
    lane = lax.broadcasted_iota(jnp.int32, (RET_CHUNK, PAIR), 1)
    in_head = [lane < HEAD_DIM, lane >= HEAD_DIM]
    r = lax.broadcasted_iota(jnp.int32, (PAIR, PAIR), 0) // HEAD_DIM
    cc = lax.broadcasted_iota(jnp.int32, (PAIR, PAIR), 1) // HEAD_DIM
    same_head = r == cc
    avg = jnp.where(same_head, 1.0 / HEAD_DIM, 0.0).astype(bf)

    q = {(p, s): q_ref[rows[s], cols[p]] for p, s in items}
    k = {(p, s): k_ref[rows[s], cols[p]] for p, s in items}
    v = {(p, s): v_ref[rows[s], cols[p]] for p, s in items}

    sc = {i: [lax.dot_general(jnp.where(in_head[h], q[i], jnp.zeros_like(q[i])), k[i], NT,
                              preferred_element_type=f32) for h in range(2)] for i in items}
    kv = {i: lax.dot_general((k[i].astype(f32) * dend_ref[:, cols[i[0]]]).astype(bf), v[i], TN,
                             preferred_element_type=f32) for i in items}
    cross = {}
    for p in range(n_pairs):
        state = state_ref[p]
        for s in range(RET_SUB):
            cross[p, s] = jnp.dot(q[p, s], state.astype(bf), preferred_element_type=f32)
            state = jnp.where(same_head, state * cdec_ref[:, cols[p]] + kv[p, s], 0.0)
        state_ref[p] = state

    y = {}
    for p, s in items:
        yh = [jnp.dot((sc[p, s][h] * dmask_ref[2 * p + h]).astype(bf), v[p, s],
                      preferred_element_type=f32) for h in range(2)]
        y[p, s] = jnp.where(in_head[0], yh[0], yh[1]) + cross[p, s] * dstart_ref[:, cols[p]]

    def head_mean(t):
        return jnp.dot(t.astype(bf), avg, preferred_element_type=f32)

    y_hi = {i: y[i].astype(bf) for i in items}
    mu = {i: jnp.dot(y_hi[i], avg, preferred_element_type=f32)
          + head_mean(y[i] - y_hi[i].astype(f32)) for i in items}
    d = {i: y[i] - mu[i] for i in items}
    var = {i: head_mean(d[i] * d[i]) for i in items}
    for p, s in items:
        g = g_ref[rows[s], cols[p]].astype(f32)
        o_ref[rows[s], cols[p]] = (g * jax.nn.sigmoid(g) * d[p, s]
                                   * lax.rsqrt(var[p, s] + EPS)).astype(bf)


def _retention(qr, kr, vr, gr, dmask, dstart, dend, cdec, batch, seq):
    d_r = qr.shape[1]
    n_steps = seq // (RET_SUB * RET_CHUNK)
    blk = pl.BlockSpec((None, RET_SUB * RET_CHUNK, d_r), lambda b, c: (b, c, 0))
    r3 = lambda a: a.reshape(batch, seq, d_r)
    out = pl.pallas_call(
        _ret_kernel,
        grid=(batch, n_steps),
        in_specs=[blk, blk, blk, blk, _const_spec(dmask.shape), _const_spec(dstart.shape),
                  _const_spec(dend.shape), _const_spec(cdec.shape)],
        out_specs=blk,
        out_shape=jax.ShapeDtypeStruct((batch, seq, d_r), jnp.bfloat16),
        scratch_shapes=[pltpu.VMEM((d_r // PAIR, PAIR, PAIR), jnp.float32)],
        compiler_params=pltpu.CompilerParams(
            dimension_semantics=("arbitrary", "arbitrary"), vmem_limit_bytes=VMEM_LIMIT),
        name="retention",
    )(r3(qr), r3(kr), r3(vr), r3(gr), dmask, dstart, dend, cdec)
    return out.reshape(batch * seq, d_r)


def _ffn_kernel(ya_ref, yr_ref, x_ref, woa_ref, wor_ref, g1_ref, g2_ref, wup_ref, cw_ref, cb_ref,
                wdn_ref, g3_ref, o_ref, u_ref, f_ref, *, tiles_per_batch, d_ff):
    tm = x_ref.shape[0]

    @pl.when(pl.program_id(0) % tiles_per_batch == 0)
    def _():
        u_ref[0:HALO, :] = jnp.zeros((HALO, u_ref.shape[1]), jnp.float32)

    y = (jnp.dot(ya_ref[...], woa_ref[...], preferred_element_type=jnp.float32)
         + jnp.dot(yr_ref[...], wor_ref[...], preferred_element_type=jnp.float32))
    x1 = x_ref[...] + _rms(y, g1_ref[...])
    h2 = _rms(x1, g2_ref[...]).astype(jnp.bfloat16)

    def chunk_cols(ci):
        lo = ci * FF_CHUNK
        return slice(lo, lo + FF_CHUNK), slice(d_ff + lo, d_ff + lo + FF_CHUNK)

    def up(ci):
        for cols in chunk_cols(ci):
            u_ref[HALO:HALO + tm, cols] = jnp.dot(h2, wup_ref[:, cols],
                                                  preferred_element_type=jnp.float32)

    def conv(cols):
        w = cw_ref[:, cols]
        out = cb_ref[:, cols]
        for j in reversed(range(CONV_WIDTH)):
            lo = HALO - (CONV_WIDTH - 1 - j)
            out = out + w[j:j + 1] * u_ref[lo:lo + tm, cols]
        return out

    n_chunks = d_ff // FF_CHUNK
    group_ends = [(g + 1) * n_chunks // FF_GROUPS for g in range(FF_GROUPS)]
    acc = None
    start = 0
    up(0)
    for ci in range(n_chunks):
        if ci + 1 < n_chunks:
            up(ci + 1)
        cols_a, cols_b = chunk_cols(ci)
        f_ref[:, cols_a] = (jax.nn.gelu(conv(cols_a), approximate=True)
                            * conv(cols_b)).astype(jnp.bfloat16)
        if ci + 1 in group_ends:
            rows = slice(start * FF_CHUNK, (ci + 1) * FF_CHUNK)
            part = jnp.dot(f_ref[:, rows], wdn_ref[rows, :], preferred_element_type=jnp.float32)
            acc = part if acc is None else acc + part
            start = ci + 1
    u_ref[0:HALO, :] = u_ref[tm:tm + HALO, :]
    o_ref[...] = x1 + _rms(acc, g3_ref[...])


def _out_ffn(ya, yr, x2, woa, wor, g1, g2, wup, cw, cb, wdn, g3, seq):
    t, d = x2.shape
    d_ff = wdn.shape[0]
    tm = ROW_TILE
    row = lambda w: pl.BlockSpec((tm, w), lambda i: (i, 0))
    return pl.pallas_call(
        functools.partial(_ffn_kernel, tiles_per_batch=seq // tm, d_ff=d_ff),
        grid=(t // tm,),
        in_specs=[row(ya.shape[1]), row(yr.shape[1]), row(d),
                  _const_spec(woa.shape), _const_spec(wor.shape),
                  _const_spec((1, d)), _const_spec((1, d)), _const_spec(wup.shape),
                  _const_spec(cw.shape), _const_spec(cb.shape), _const_spec(wdn.shape),
                  _const_spec((1, d))],
        out_specs=row(d),
        out_shape=jax.ShapeDtypeStruct((t, d), jnp.float32),
        scratch_shapes=[pltpu.VMEM((HALO + tm, 2 * d_ff), jnp.float32),
                        pltpu.VMEM((tm, d_ff), jnp.bfloat16)],
        compiler_params=pltpu.CompilerParams(
            dimension_semantics=("arbitrary",), vmem_limit_bytes=VMEM_LIMIT),
        name="out_ffn",
    )(ya, yr, x2, woa, wor, g1, g2, wup, cw, cb, wdn, g3)


def _rotary_tables(seq):
    inv = ROPE_BASE ** (-jnp.arange(0, HEAD_DIM, 2, dtype=jnp.float32) / HEAD_DIM)
    ang = jnp.arange(seq, dtype=jnp.float32)[:, None] * inv[None, :]
    cos, sin = jnp.cos(ang), jnp.sin(ang)
    cos_t = jnp.tile(jnp.concatenate([cos, cos], axis=1), (1, PAIR // HEAD_DIM))
    sin_t = jnp.tile(jnp.concatenate([-sin, sin], axis=1), (1, PAIR // HEAD_DIM))
    return cos_t, sin_t


def _decay_tables(n_heads):
    c = RET_CHUNK
    log_gamma = jnp.log(1.0 - 2.0 ** (-5.0 - jnp.arange(n_heads, dtype=jnp.float32)))
    n = jnp.arange(c, dtype=jnp.float32)
    rel = n[:, None] - n[None, :]
    dmask = jnp.where(rel >= 0, jnp.exp(jnp.maximum(rel, 0.0)[None] * log_gamma[:, None, None]), 0.0)
    dend = jnp.exp((c - 1.0 - n)[:, None] * log_gamma[None, :])
    dstart = jnp.exp((n + 1.0)[:, None] * log_gamma[None, :])
    cdec = jnp.exp(c * log_gamma)[None, :]
    wide = lambda a: jnp.repeat(a, HEAD_DIM, axis=1)
    return dmask, wide(dstart), wide(dend), wide(cdec)


def kernel(x, norm_mix_pre, w_in, rel_bias, w_out, norm_mix_post, norm_ffn_pre, w_up, conv_w,
           conv_b, w_down, norm_ffn_post):
    batch, seq, d = x.shape
    depth = w_in.shape[0]
    d_a = w_out.shape[1] // 2
    d_r = w_out.shape[1] - d_a
    assert seq % PROJ_TILE == 0 and PROJ_TILE % MOBA_BLOCK == 0 and MOBA_BLOCK == RET_CHUNK
    assert seq % ROW_TILE == 0 and seq % (RET_SUB * RET_CHUNK) == 0
    assert w_in.shape[2] == 3 * d_a + 4 * d_r and d_a % PAIR == 0 and d_r % PAIR == 0
    d_ff = w_down.shape[1]
    assert d_ff % FF_CHUNK == 0 and d_ff // FF_CHUNK >= FF_GROUPS and w_up.shape[2] == 2 * d_ff
    assert conv_w.shape[1] == CONV_WIDTH <= HALO
    assert d_a % (MOBA_HEADS * HEAD_DIM) == 0 and MOBA_HEADS % 2 == 0
    assert all(t % FAR_RING == 0 for t in FAR_TRIPS) and FAR_AHEAD < FAR_RING

    bf = jnp.bfloat16
    cos_t, sin_t = _rotary_tables(seq)
    dmask, dstart, dend, cdec = _decay_tables(d_r // HEAD_DIM)
    bias = _bias_tables(rel_bias)

    x2 = x.reshape(batch * seq, d)
    for l in range(depth):
        w = w_in[l].astype(bf)
        wqvt = jnp.concatenate([w[:, :d_a], w[:, 2 * d_a:3 * d_a]], axis=1).T
        wk = w[:, d_a:2 * d_a]
        wqkr = w[:, 3 * d_a:3 * d_a + 2 * d_r]
        wvgr = w[:, 3 * d_a + 2 * d_r:]
        qat, ka, vt, gate, qr, kr, vr, gr = _projection(
            x2, norm_mix_pre[l][None], wqvt, wk, wqkr, wvgr, cos_t, sin_t, batch, seq)
        ya = _moba(rel_bias, qat, ka, vt, gate, bias, batch, seq)
        yr = _retention(qr, kr, vr, gr, dmask, dstart, dend, cdec, batch, seq)
        wo = w_out[l].astype(bf)
        x2 = _out_ffn(ya, yr, x2, wo[:d_a], wo[d_a:], norm_mix_post[l][None],
                      norm_ffn_pre[l][None], w_up[l].astype(bf), conv_w[l], conv_b[l][None],
                      w_down[l].astype(bf), norm_ffn_post[l][None], seq)
    return x2.reshape(batch, seq, d)
```
